```python
import math
import jax, jax.numpy as jnp
from jax import lax
import numpy as np

D_MODEL = 1024
BATCH = 16
SEQ = 256
DEPTH = 2
DEC_BATCH = 2
DEC_SEQ = 1024
PAST_LEN = 512

GRID_W = 64
HEAD_DIM = 64
BLOCK = 128
N_EVEN = (DEPTH + 1) // 2
N_ODD = DEPTH // 2
HY_W = 512
SHORT_CONV = 3
POS_BANDS = 16
POS_EMB = 1 + 2 * POS_BANDS
FILT_HID = 64
HY_FAST_DECAY = 0.3
HY_SLOW_DECAY = 1.5
HY_TARGET = 1e-2
B_HEADS = 8
B_KV = 2
WINDOW = 128
C_HEADS = 8
C_KV = 2
RET_HEADS = 4
RET_DK = 128
RET_DV = 128
CHUNK = 128
ROPE_BASE = 10000.0
D_FF = 2816
N_EXPERTS = 8
TOP_K = 2
EXPERT_FF = 1408
ALPHA = (2 * DEPTH) ** 0.25
BETA = (8 * DEPTH) ** -0.25
EPS = 1e-6
IN_AB = 3 * HY_W + (B_HEADS + 2 * B_KV) * HEAD_DIM
MIX_AB = HY_W + B_HEADS * HEAD_DIM
IN_CD = (C_HEADS + 2 * C_KV) * HEAD_DIM + RET_HEADS * (2 * RET_DK + 3 * RET_DV)
MIX_CD = C_HEADS * HEAD_DIM + RET_HEADS * RET_DV

kernel_name = 'hybrid_diffusion_prefix_trunk_step'


def layer_norm(x, g, b):
    xf = x.astype(jnp.float32)
    mu = jnp.mean(xf, -1, keepdims=True)
    var = jnp.mean(jnp.square(xf - mu), -1, keepdims=True)
    return ((xf - mu) * lax.rsqrt(var + EPS) * g + b).astype(x.dtype)


def rms_norm(x, g=None):
    xf = x.astype(jnp.float32)
    y = xf * lax.rsqrt(jnp.mean(jnp.square(xf), -1, keepdims=True) + EPS)
    if g is not None:
        y = y * g
    return y.astype(x.dtype)


def grid_positions(n_tokens):
    n_rows = n_tokens // GRID_W
    row = jnp.broadcast_to(jnp.arange(n_rows)[:, None], (n_rows, GRID_W)).reshape(-1)
    col = jnp.broadcast_to(jnp.arange(GRID_W)[None, :], (n_rows, GRID_W)).reshape(-1)
    return row, col


def axial_rope(x, row, col):
    d = x.shape[-1]
    half, nf = d // 2, d // 4
    inv = ROPE_BASE ** (-jnp.arange(nf, dtype=jnp.float32) / nf)
    shape = (x.shape[1],) + (1,) * (x.ndim - 3) + (nf,)

    def rot(xh, pos):
        ang = (pos.astype(jnp.float32)[:, None] * inv[None, :]).reshape(shape)
        cos, sin = jnp.cos(ang), jnp.sin(ang)
        x1, x2 = xh[..., :nf], xh[..., nf:]
        return jnp.concatenate([x1 * cos - x2 * sin, x1 * sin + x2 * cos], -1)

    return jnp.concatenate([rot(x[..., :half], row), rot(x[..., half:], col)], -1).astype(x.dtype)


def short_conv(x, w, b):
    pad = SHORT_CONV // 2
    l = x.shape[1]
    xp = jnp.pad(x, ((0, 0), (pad, pad), (0, 0)))
    return sum(xp[:, t:t + l] * w[t] for t in range(SHORT_CONV)) + b


def hyena_filters(l, f_w1, f_b1, f_freq, f_w2, f_b2, f_w3):
    t = jnp.linspace(0.0, 1.0, l, dtype=jnp.float32)[:, None]
    w = 2.0 * math.pi * jnp.arange(l, dtype=jnp.float32)[:, None] / l
    bands = jnp.linspace(1e-4, POS_BANDS - 1.0, POS_BANDS, dtype=jnp.float32)[None, :]
    z = jnp.concatenate([t, jnp.cos(bands * w), -jnp.sin(bands * w)], -1)
    h = jnp.sin(f_freq * (z @ f_w1 + f_b1))
    h = jnp.sin(f_freq * (h @ f_w2 + f_b2))
    h = (h @ f_w3).astype(jnp.float32)
    max_decay = math.log(HY_TARGET) / HY_FAST_DECAY
    min_decay = math.log(HY_TARGET) / HY_SLOW_DECAY
    deltas = jnp.linspace(min_decay, max_decay, HY_W, dtype=jnp.float32)
    window = jnp.exp(-t * jnp.abs(deltas)[None, :])
    h = h.reshape(l, 2, HY_W) * window[:, None, :]
    return h[:, 0], h[:, 1]


def fft_conv(v, h):
    l = v.shape[1]
    n = 2 * l
    vf = jnp.fft.rfft(v, n=n, axis=1)
    hf = jnp.fft.rfft(h, n=n, axis=0)
    return jnp.fft.irfft(vf * hf[None], n=n, axis=1)[:, :l]


def hyena(u, conv_w, conv_b, f_w1, f_b1, f_freq, f_w2, f_b2, f_w3, skip):
    u = short_conv(u, conv_w, conv_b)
    x0, x1, v = jnp.split(u, 3, axis=-1)
    h_fwd, h_bwd = hyena_filters(u.shape[1], f_w1, f_b1, f_freq, f_w2, f_b2, f_w3)
    v = v.astype(jnp.float32) * x1
    v = fft_conv(v, h_fwd) + jnp.flip(fft_conv(jnp.flip(v, 1), h_bwd), 1) + skip * v
    return (v * x0).astype(u.dtype)


def blocked_attention(q, k, v, sink=None):
    b, lq, hkv, g, d = q.shape
    nb = lq // BLOCK
    scale = d ** -0.5
    qb = jnp.moveaxis(q.reshape(b, nb, BLOCK, hkv, g, d), 1, 0)

    def one_block(qblk):
        s = jnp.einsum('bqhgd,bkhd->bhgqk', qblk, k, preferred_element_type=jnp.float32) * scale
        if sink is not None:
            col = jnp.broadcast_to(sink.astype(jnp.float32).reshape(1, hkv, g, 1, 1), s.shape[:-1] + (1,))
            p = jax.nn.softmax(jnp.concatenate([s, col], -1), -1)[..., :-1]
        else:
            p = jax.nn.softmax(s, -1)
        return jnp.einsum('bhgqk,bkhd->bqhgd', p.astype(v.dtype), v)

    out = lax.map(one_block, qb)
    return jnp.moveaxis(out, 0, 1).reshape(b, lq, hkv * g * d)


def window_attention(q, k, v, ck, cv, sink):
    b, l, hkv, g, d = q.shape
    nb = l // BLOCK
    lc = ck.shape[1]
    scale = d ** -0.5
    qb = q.reshape(b, nb, BLOCK, hkv, g, d)

    def bands(t):
        tp = jnp.pad(t, ((0, 0), (BLOCK, BLOCK), (0, 0), (0, 0))).reshape(b, nb + 2, BLOCK, hkv, d)
        return jnp.concatenate([tp[:, :-2], tp[:, 1:-1], tp[:, 2:]], axis=2)

    kw, vw = bands(k), bands(v)
    s_loc = jnp.einsum('bnqhgd,bnkhd->bnhgqk', qb, kw, preferred_element_type=jnp.float32) * scale
    s_ctx = jnp.einsum('bnqhgd,bchd->bnhgqc', qb, ck.astype(q.dtype), preferred_element_type=jnp.float32) * scale
    qpos = jnp.arange(nb)[:, None] * BLOCK + jnp.arange(BLOCK)[None, :]
    kpos = jnp.arange(nb)[:, None] * BLOCK - BLOCK + jnp.arange(3 * BLOCK)[None, :]
    valid = ((jnp.abs(qpos[:, :, None] - kpos[:, None, :]) <= WINDOW)
             & (kpos[:, None, :] >= 0) & (kpos[:, None, :] < l))
    s_loc = jnp.where(valid[None, :, None, None], s_loc, -jnp.inf)
    s_sink = jnp.broadcast_to(sink.astype(jnp.float32).reshape(1, 1, hkv, g, 1, 1), s_loc.shape[:-1] + (1,))
    p = jax.nn.softmax(jnp.concatenate([s_loc, s_ctx, s_sink], -1), -1)
    nl = 3 * BLOCK
    out = (jnp.einsum('bnhgqk,bnkhd->bnqhgd', p[..., :nl].astype(v.dtype), vw)
           + jnp.einsum('bnhgqc,bchd->bnqhgd', p[..., nl:nl + lc].astype(v.dtype), cv.astype(v.dtype)))
    return out.reshape(b, l, hkv * g * d)


def chunk_retention(q, k, v, gamma, s0):
    b, l, h, dk = q.shape
    dv = v.shape[-1]
    n = l // CHUNK
    log_g = jnp.log(gamma)
    idx = jnp.arange(CHUNK, dtype=jnp.float32)
    diff = idx[:, None] - idx[None, :]
    decay_mask = jnp.exp(jnp.where(diff[None] >= 0, diff[None] * log_g[:, None, None], -jnp.inf))
    q_decay = jnp.exp((idx[:, None] + 1.0) * log_g[None, :])
    k_decay = jnp.exp((CHUNK - 1.0 - idx[:, None]) * log_g[None, :])
    chunk_decay = jnp.exp(CHUNK * log_g)

    def to_chunks(t):
        return jnp.moveaxis(t.astype(jnp.float32).reshape(b, n, CHUNK, h, t.shape[-1]), 1, 0)

    def step(state, inp):
        qi, ki, vi = inp
        inner = jnp.einsum('bihd,bjhd->bhij', qi, ki) * decay_mask
        o = (jnp.einsum('bhij,bjhe->bihe', inner, vi)
             + jnp.einsum('bihd,bhde->bihe', qi * q_decay[:, :, None], state))
        state = state * chunk_decay[:, None, None] + jnp.einsum('bjhd,bjhe->bhde', ki * k_decay[:, :, None], vi)
        return state, o

    state, o = lax.scan(step, s0.astype(jnp.float32), (to_chunks(q), to_chunks(k), to_chunks(v)))
    return jnp.moveaxis(o, 0, 1).reshape(b, l, h, dv), state


def mixer_ab(h, w_in, conv_w, conv_b, f_w1, f_b1, f_freq, f_w2, f_b2, f_w3, skip, sink, w_out,
             ctx_k=None, ctx_v=None, row=None, col=None):
    b, l, _ = h.shape
    g = B_HEADS // B_KV
    cut1 = 3 * HY_W
    cut2 = cut1 + B_HEADS * HEAD_DIM
    cut3 = cut2 + B_KV * HEAD_DIM
    hy, q, k, v = jnp.split(h @ w_in, [cut1, cut2, cut3], axis=-1)
    y_a = hyena(hy, conv_w, conv_b, f_w1, f_b1, f_freq, f_w2, f_b2, f_w3, skip)
    q = q.reshape(b, l, B_KV, g, HEAD_DIM)
    k = k.reshape(b, l, B_KV, HEAD_DIM)
    v = v.reshape(b, l, B_KV, HEAD_DIM)
    sink = sink.reshape(B_KV, g)
    if ctx_k is None:
        y_b = blocked_attention(q, k, v, sink)
        ctx = (k, v)
    else:
        y_b = window_attention(axial_rope(q, row, col), axial_rope(k, row, col), v, ctx_k, ctx_v, sink)
        ctx = None
    return jnp.concatenate([y_a, y_b.astype(y_a.dtype)], -1) @ w_out, ctx


def mixer_cd(h, w_in, qn_g, kn_g, ret_decay, w_out, ctx_k=None, ctx_v=None, ctx_state=None, row=None, col=None):
    b, l, _ = h.shape
    g = C_HEADS // C_KV
    sizes = [C_HEADS * HEAD_DIM, C_KV * HEAD_DIM, C_KV * HEAD_DIM, RET_HEADS * RET_DK, RET_HEADS * RET_DK,
             RET_HEADS * RET_DV, RET_HEADS * RET_DV, RET_HEADS * RET_DV]
    cuts = [int(s) for s in np.cumsum(sizes)[:-1]]
    q, k, v, rq, rk, rv, g_f, g_b = jnp.split(h @ w_in, cuts, axis=-1)
    q = rms_norm(q.reshape(b, l, C_KV, g, HEAD_DIM), qn_g)
    k = rms_norm(k.reshape(b, l, C_KV, HEAD_DIM), kn_g)
    v = v.reshape(b, l, C_KV, HEAD_DIM)
    rq = rq.reshape(b, l, RET_HEADS, RET_DK) * RET_DK ** -0.5
    rk = rk.reshape(b, l, RET_HEADS, RET_DK)
    rv = rv.reshape(b, l, RET_HEADS, RET_DV)
    gamma = jax.nn.sigmoid(ret_decay.astype(jnp.float32))
    if ctx_k is None:
        y_c = blocked_attention(q, k, v)
        s0 = jnp.zeros((b, 2, RET_HEADS, RET_DK, RET_DV), jnp.float32)
    else:
        q_r, k_r = axial_rope(q, row, col), axial_rope(k, row, col)
        rq, rk = axial_rope(rq, row, col), axial_rope(rk, row, col)
        y_c = blocked_attention(q_r, jnp.concatenate([k_r, ctx_k.astype(k.dtype)], 1),
                                jnp.concatenate([v, ctx_v.astype(v.dtype)], 1))
        s0 = ctx_state
    o_f, s_f = chunk_retention(rq, rk, rv, gamma[0], s0[:, 0])
    o_b, s_b = chunk_retention(jnp.flip(rq, 1), jnp.flip(rk, 1), jnp.flip(rv, 1), gamma[1], s0[:, 1])
    o_b = jnp.flip(o_b, 1)
    gate_shape = (b, l, RET_HEADS, RET_DV)
    y_d = (jax.nn.silu(g_f.reshape(gate_shape)) * rms_norm(o_f)
           + jax.nn.silu(g_b.reshape(gate_shape)) * rms_norm(o_b))
    y = jnp.concatenate([y_c.astype(h.dtype), y_d.reshape(b, l, -1).astype(h.dtype)], -1)
    ctx = (k, v, jnp.stack([s_f, s_b], 1)) if ctx_k is None else None
    return y @ w_out, ctx


def swiglu(h, w1, w3, w2):
    return (jax.nn.silu(h @ w1) * (h @ w3)) @ w2


def moe_swiglu(h, router, w1, w3, w2):
    b, l, d = h.shape
    t = h.reshape(-1, d)
    logits = (t @ router).astype(jnp.float32)
    top_v, top_i = lax.top_k(logits, TOP_K)
    gates = jax.nn.softmax(top_v, -1)
    dense_g = jnp.sum(jax.nn.one_hot(top_i, N_EXPERTS, dtype=jnp.float32) * gates[..., None], axis=1)
    act = jax.nn.silu(jnp.einsum('td,edf->tef', t, w1)) * jnp.einsum('td,edf->tef', t, w3)
    y = jnp.einsum('tef,efd->td', (act * dense_g[..., None]).astype(h.dtype), w2)
    return y.reshape(b, l, d)


def adaln_block(x, mod, ln_g, ln_b, mix_fn, ffn_fn):
    sh1, sc1, g1, sh2, sc2, g2 = jnp.split(mod, 6, axis=-1)
    m, extra = mix_fn(x * (1 + sc1) + sh1)
    x = layer_norm(ALPHA * x + g1 * m, ln_g[0], ln_b[0])
    f = ffn_fn(x * (1 + sc2) + sh2)
    x = layer_norm(ALPHA * x + g2 * f, ln_g[1], ln_b[1])
    return x, extra


def setup_inputs(seed: int = 0) -> dict:
    key = jax.random.key(seed)
    ks = iter(jax.random.split(key, 48))

    def nrm(shape, scale=1.0):
        return jax.random.normal(next(ks), shape, jnp.float32) * scale

    D = D_MODEL
    inp = {}
    inp['x_prompt'] = nrm((BATCH, SEQ, D))
    inp['x_sample'] = nrm((DEC_BATCH, DEC_SEQ, D))
    inp['c'] = nrm((DEC_BATCH, D))
    inp['cache_k_b'] = nrm((DEC_BATCH, N_EVEN, PAST_LEN, B_KV, HEAD_DIM))
    inp['cache_v_b'] = nrm((DEC_BATCH, N_EVEN, PAST_LEN, B_KV, HEAD_DIM))
    inp['cache_k_c'] = nrm((DEC_BATCH, N_ODD, PAST_LEN, C_KV, HEAD_DIM))
    inp['cache_v_c'] = nrm((DEC_BATCH, N_ODD, PAST_LEN, C_KV, HEAD_DIM))
    inp['state_ret'] = nrm((DEC_BATCH, N_ODD, 2, RET_HEADS, RET_DK, RET_DV), 0.5)
    inp['c_ctx'] = nrm((D,))
    inp['ada_w'] = nrm((DEPTH, D, 6 * D), 0.5 * D ** -0.5)
    inp['ada_b'] = nrm((DEPTH, 6 * D), 0.02)
    inp['ln_g'] = 1.0 + nrm((DEPTH, 2, D), 0.02)
    inp['ln_b'] = nrm((DEPTH, 2, D), 0.02)
    inp['w_in_ab'] = nrm((N_EVEN, D, IN_AB), D ** -0.5)
    inp['hy_conv_w'] = nrm((N_EVEN, SHORT_CONV, 3 * HY_W), SHORT_CONV ** -0.5)
    inp['hy_conv_b'] = nrm((N_EVEN, 3 * HY_W), 0.02)
    inp['hf_w1'] = nrm((N_EVEN, POS_EMB, FILT_HID), POS_EMB ** -0.5)
    inp['hf_b1'] = nrm((N_EVEN, FILT_HID), 0.02)
    inp['hf_freq'] = 1.0 + nrm((N_EVEN, FILT_HID), 0.02)
    inp['hf_w2'] = nrm((N_EVEN, FILT_HID, FILT_HID), FILT_HID ** -0.5)
    inp['hf_b2'] = nrm((N_EVEN, FILT_HID), 0.02)
    inp['hf_w3'] = nrm((N_EVEN, FILT_HID, 2 * HY_W), 0.02)
    inp['hy_skip'] = nrm((N_EVEN, HY_W), 0.1)
    inp['sink_b'] = nrm((N_EVEN, B_HEADS), 0.5)
    inp['w_out_ab'] = nrm((N_EVEN, MIX_AB, D), BETA * MIX_AB ** -0.5)
    inp['ffn_w1'] = nrm((N_EVEN, D, D_FF), D ** -0.5)
    inp['ffn_w3'] = nrm((N_EVEN, D, D_FF), D ** -0.5)
    inp['ffn_w2'] = nrm((N_EVEN, D_FF, D), BETA * D_FF ** -0.5)
    inp['w_in_cd'] = nrm((N_ODD, D, IN_CD), D ** -0.5)
    inp['qn_g'] = 1.0 + nrm((N_ODD, HEAD_DIM), 0.02)
    inp['kn_g'] = 1.0 + nrm((N_ODD, HEAD_DIM), 0.02)
    base = jnp.log(2.0 ** (5.0 + jnp.arange(RET_HEADS, dtype=jnp.float32)) - 1.0)
    inp['ret_decay'] = base[None, None, :] + nrm((N_ODD, 2, RET_HEADS), 0.1)
    inp['w_out_cd'] = nrm((N_ODD, MIX_CD, D), BETA * MIX_CD ** -0.5)
    inp['moe_router'] = nrm((N_ODD, D, N_EXPERTS), D ** -0.5)
    inp['moe_w1'] = nrm((N_ODD, N_EXPERTS, D, EXPERT_FF), D ** -0.5)
    inp['moe_w3'] = nrm((N_ODD, N_EXPERTS, D, EXPERT_FF), D ** -0.5)
    inp['moe_w2'] = nrm((N_ODD, N_EXPERTS, EXPERT_FF, D), BETA * EXPERT_FF ** -0.5)
    return inp


def reference(x_prompt, x_sample, c, cache_k_b, cache_v_b, cache_k_c, cache_v_c, state_ret, c_ctx,
              ada_w, ada_b, ln_g, ln_b, w_in_ab, hy_conv_w, hy_conv_b, hf_w1, hf_b1, hf_freq, hf_w2, hf_b2,
              hf_w3, hy_skip, sink_b, w_out_ab, ffn_w1, ffn_w3, ffn_w2, w_in_cd, qn_g, kn_g, ret_decay,
              w_out_cd, moe_router, moe_w1, moe_w3, moe_w2):
    row, col = grid_positions(x_sample.shape[1])
    xp, xs = x_prompt, x_sample
    k_b, v_b, k_c, v_c, s_r = [], [], [], [], []
    for i in range(DEPTH):
        j = i // 2
        mod_p = jax.nn.silu(c_ctx) @ ada_w[i] + ada_b[i]
        mod_s = (jax.nn.silu(c) @ ada_w[i] + ada_b[i])[:, None, :]
        if i % 2 == 0:
            ab = (w_in_ab[j], hy_conv_w[j], hy_conv_b[j], hf_w1[j], hf_b1[j], hf_freq[j], hf_w2[j], hf_b2[j],
                  hf_w3[j], hy_skip[j], sink_b[j], w_out_ab[j])
            ffn = lambda h: swiglu(h, ffn_w1[j], ffn_w3[j], ffn_w2[j])
            xp, (kb, vb) = adaln_block(xp, mod_p, ln_g[i], ln_b[i], lambda h: mixer_ab(h, *ab), ffn)
            xs, _ = adaln_block(xs, mod_s, ln_g[i], ln_b[i],
                                lambda h: mixer_ab(h, *ab, ctx_k=cache_k_b[:, j], ctx_v=cache_v_b[:, j],
                                                   row=row, col=col), ffn)
            k_b.append(kb)
            v_b.append(vb)
        else:
            cd = (w_in_cd[j], qn_g[j], kn_g[j], ret_decay[j], w_out_cd[j])
            ffn = lambda h: moe_swiglu(h, moe_router[j], moe_w1[j], moe_w3[j], moe_w2[j])
            xp, (kc, vc, sr) = adaln_block(xp, mod_p, ln_g[i], ln_b[i], lambda h: mixer_cd(h, *cd), ffn)
            xs, _ = adaln_block(xs, mod_s, ln_g[i], ln_b[i],
                                lambda h: mixer_cd(h, *cd, ctx_k=cache_k_c[:, j], ctx_v=cache_v_c[:, j],
                                                   ctx_state=state_ret[:, j], row=row, col=col), ffn)
            k_c.append(kc)
            v_c.append(vc)
            s_r.append(sr)
    return (xp, xs, jnp.stack(k_b, 1), jnp.stack(v_b, 1), jnp.stack(k_c, 1), jnp.stack(v_c, 1), jnp.stack(s_r, 1))
```

```python
import functools
import math

import numpy as np
import jax
import jax.numpy as jnp
from jax import lax
from jax.experimental import pallas as pl
from jax.experimental.pallas import tpu as pltpu

F32 = jnp.float32
BF16 = jnp.bfloat16

D_MODEL = 1024
BATCH = 16
SEQ = 256
DEC_BATCH = 2
DEC_SEQ = 1024
PAST_LEN = 512
GRID_W = 64
HEAD_DIM = 64
BLOCK = 128
HY_W = 512
POS_BANDS = 16
POS_EMB = 1 + 2 * POS_BANDS
FILT_HID = 64
HY_FAST_DECAY = 0.3
HY_SLOW_DECAY = 1.5
HY_TARGET = 1e-2
N_HEADS = 8
N_KV = 2
GROUPS = N_HEADS // N_KV
RET_HEADS = 4
RET_D = 128
CHUNK = 128
ROPE_BASE = 10000.0
D_FF = 2816
N_EXPERTS = 8
EXPERT_FF = 1408
DEPTH = 2
ALPHA = (2 * DEPTH) ** 0.25
EPS = 1e-6

T_PROMPT = BATCH * SEQ
T_SAMPLE = DEC_BATCH * DEC_SEQ
T_ALL = T_PROMPT + T_SAMPLE
GROUP_ROWS = 1024
N_PROMPT_GROUPS = T_PROMPT // GROUP_ROWS
MOD_ROWS = 16
LANES = 128
VMEM_LIMIT = 58 * 1024 * 1024

AB_Q = 3 * HY_W
AB_K = AB_Q + N_HEADS * HEAD_DIM
AB_V = AB_K + N_KV * HEAD_DIM
IN_AB = AB_V + N_KV * HEAD_DIM
CD_Q = 0
CD_K = N_HEADS * HEAD_DIM
CD_V = CD_K + N_KV * HEAD_DIM
CD_RQ = CD_V + N_KV * HEAD_DIM
CD_RK = CD_RQ + RET_HEADS * RET_D
CD_RV = CD_RK + RET_HEADS * RET_D
CD_GF = CD_RV + RET_HEADS * RET_D
CD_GB = CD_GF + RET_HEADS * RET_D
IN_CD = CD_GB + RET_HEADS * RET_D


def _cparams(*sem):
    return pltpu.CompilerParams(dimension_semantics=sem, vmem_limit_bytes=VMEM_LIMIT)


def _silu(x):
    return x * jax.nn.sigmoid(x)


def _bdot(a, b):
    return jnp.dot(a.astype(BF16), b.astype(BF16), preferred_element_type=F32)


def _bdot_nt(a, b):
    return lax.dot_general(a.astype(BF16), b.astype(BF16), (((1,), (1,)), ((), ())),
                           preferred_element_type=F32)


def _bdot_tn(a, b):
    return lax.dot_general(a.astype(BF16), b.astype(BF16), (((0,), (0,)), ((), ())),
                           preferred_element_type=F32)


def _layer_norm(z, g, b):
    mu = jnp.mean(z, -1, keepdims=True)
    zc = z - mu
    var = jnp.mean(zc * zc, -1, keepdims=True)
    return zc * lax.rsqrt(var + EPS) * g + b


def _group_of_tile(i, tm):
    return jnp.maximum(i // (GROUP_ROWS // tm) - (N_PROMPT_GROUPS - 1), 0)


def _rope_tables(n_tokens, d, reps):
    nf = d // 4
    inv = ROPE_BASE ** (-np.arange(nf, dtype=np.float64) / nf)
    pos = np.arange(n_tokens)
    row, col = pos // GRID_W, pos % GRID_W
    ang_r = row[:, None] * inv[None, :]
    ang_c = col[:, None] * inv[None, :]
    zeros = np.zeros_like(ang_r)
    cos = np.concatenate([np.cos(ang_r), np.cos(ang_r), np.cos(ang_c), np.cos(ang_c)], -1)
    sin_a = np.concatenate([-np.sin(ang_r), zeros, -np.sin(ang_c), zeros], -1)
    sin_b = np.concatenate([zeros, np.sin(ang_r), zeros, np.sin(ang_c)], -1)
    tile = lambda a: jnp.asarray(np.tile(a, (1, reps)), F32)
    return tile(cos), tile(sin_a), tile(sin_b)


def _dft_mats(l):
    n = 2 * l
    k = np.arange(l, dtype=np.float64)
    ang = 2.0 * np.pi * np.outer(k, k) / n
    fc = np.cos(ang)
    fs = np.sin(ang)
    fs[0, :] = np.cos(np.pi * k)
    fwd = np.concatenate([fc, fs], 0)
    wk = np.full((l,), 2.0)
    wk[0] = 1.0
    inv = np.concatenate([fc.T * wk[None, :], fs.T * wk[None, :]], 1)
    return jnp.asarray(fwd, F32), jnp.asarray(inv, F32)


def _filter_consts(l):
    t = np.linspace(0.0, 1.0, l, dtype=np.float32).astype(np.float64)[:, None]
    w = (2.0 * math.pi * np.arange(l, dtype=np.float64) / l)[:, None]
    bands = np.linspace(1e-4, POS_BANDS - 1.0, POS_BANDS, dtype=np.float32).astype(np.float64)[None, :]
    z = np.concatenate([t, np.cos(bands * w), -np.sin(bands * w)], -1)
    z = np.pad(z, ((0, 0), (0, LANES - POS_EMB)))
    max_decay = math.log(HY_TARGET) / HY_FAST_DECAY
    min_decay = math.log(HY_TARGET) / HY_SLOW_DECAY
    deltas = np.linspace(min_decay, max_decay, HY_W, dtype=np.float32).astype(np.float64)
    window = np.exp(-t * np.abs(deltas)[None, :])
    return jnp.asarray(z, F32), jnp.asarray(window, F32)


def _mod_kernel(c_ref, w_ref, b_ref, o_ref):
    o_ref[...] = _bdot(_silu(c_ref[...]), w_ref[...]) + b_ref[...]


def _modulation(cvec, ada_w, ada_b):
    tn = 1536
    n = ada_w.shape[-1]
    return pl.pallas_call(
        _mod_kernel,
        grid=(DEPTH, n // tn),
        in_specs=[pl.BlockSpec((MOD_ROWS, D_MODEL), lambda l, j: (0, 0)),
                  pl.BlockSpec((None, D_MODEL, tn), lambda l, j: (l, 0, j)),
                  pl.BlockSpec((None, 1, tn), lambda l, j: (l, 0, j))],
        out_specs=pl.BlockSpec((None, MOD_ROWS, tn), lambda l, j: (l, 0, j)),
        out_shape=jax.ShapeDtypeStruct((DEPTH, MOD_ROWS, n), F32),
        compiler_params=_cparams("parallel", "parallel"),
        name="mod",
    )(cvec, ada_w, ada_b.reshape(DEPTH, 1, n))


def _inproj_kernel(x_ref, sh_ref, sc_ref, w_ref, o_ref, h_scr, *, tm):
    i, j = pl.program_id(0), pl.program_id(1)

    @pl.when(j == 0)
    def _():
        g = _group_of_tile(i, tm)
        sc = sc_ref[pl.ds(g, 1), :]
        sh = sh_ref[pl.ds(g, 1), :]
        h_scr[...] = (x_ref[...] * (1.0 + sc) + sh).astype(BF16)

    o_ref[...] = jnp.dot(h_scr[...], w_ref[...].astype(BF16), preferred_element_type=F32)


def _inproj(x, mod, w, tn):
    tm = 1024
    n = w.shape[1]
    return pl.pallas_call(
        functools.partial(_inproj_kernel, tm=tm),
        grid=(T_ALL // tm, n // tn),
        in_specs=[pl.BlockSpec((tm, D_MODEL), lambda i, j: (i, 0)),
                  pl.BlockSpec((MOD_ROWS, D_MODEL), lambda i, j: (0, 0)),
                  pl.BlockSpec((MOD_ROWS, D_MODEL), lambda i, j: (0, 1)),
                  pl.BlockSpec((D_MODEL, tn), lambda i, j: (0, j))],
        out_specs=pl.BlockSpec((tm, tn), lambda i, j: (i, j)),
        out_shape=jax.ShapeDtypeStruct((T_ALL, n), F32),
        scratch_shapes=[pltpu.VMEM((tm, D_MODEL), BF16)],
        compiler_params=_cparams("parallel", "arbitrary"),
        name="inproj",
    )(x, mod, mod, w)


def _hyena_filter_kernel(z_ref, w1_ref, b1_ref, fr_ref, w2_ref, b2_ref, w3_ref, win_ref, fwd32_ref, inv32_ref,
                         hc_ref, hs_ref, hc2_ref, fwd_ref, inv_ref, *, l):
    hi = lax.Precision.HIGHEST
    fwd = fwd32_ref[...].astype(BF16)
    fwd_ref[...] = fwd
    inv_ref[...] = inv32_ref[...].astype(BF16)
    fr = fr_ref[...]
    h = jnp.sin(fr * (jnp.dot(z_ref[...], w1_ref[...], precision=hi, preferred_element_type=F32) + b1_ref[...]))
    h = jnp.sin(fr * (jnp.dot(h, w2_ref[...], precision=hi, preferred_element_type=F32) + b2_ref[...]))
    h = jnp.dot(h, w3_ref[...], precision=hi, preferred_element_type=F32)
    win = win_ref[...]
    hf = h[:, :HY_W] * win
    hb = h[:, HY_W:] * win
    p = _bdot(fwd, hf + hb)
    q = _bdot(fwd, hf - hb)
    row0 = lax.broadcasted_iota(jnp.int32, (l, 1), 0) == 0
    hc = p[:l]
    hc_ref[...] = hc
    hs_ref[...] = jnp.where(row0, 0.0, q[l:])
    hc2_ref[...] = jnp.where(row0, p[l:l + 1], hc)


def _hyena_filter(l, fw1, fb1, ffreq, fw2, fb2, fw3):
    z, window = _filter_consts(l)
    fwd, inv = _dft_mats(l)
    pad_c = LANES - FILT_HID
    w1 = jnp.pad(fw1, ((0, LANES - POS_EMB), (0, pad_c)))
    w2 = jnp.pad(fw2, ((0, pad_c), (0, pad_c)))
    w3 = jnp.pad(fw3, ((0, pad_c), (0, 0)))
    row = lambda a: jnp.pad(a, (0, pad_c)).reshape(1, LANES)
    shp = jax.ShapeDtypeStruct((l, HY_W), F32)
    return pl.pallas_call(
        functools.partial(_hyena_filter_kernel, l=l),
        out_shape=(shp, shp, shp, jax.ShapeDtypeStruct(fwd.shape, BF16), jax.ShapeDtypeStruct(inv.shape, BF16)),
        compiler_params=pltpu.CompilerParams(vmem_limit_bytes=VMEM_LIMIT),
        name=f"hyena_filter_{l}",
    )(z, w1, row(fb1), row(ffreq), w2, row(fb2), w3, window, fwd, inv)


def _hyena_kernel(u_ref, cw_ref, cb_ref, skip_ref, fwd_ref, inv_ref, hc_ref, hs_ref, hc2_ref, o_ref, *, l):
    u = u_ref[...]
    rows = lax.broadcasted_iota(jnp.int32, (l, 1), 0)
    prev = jnp.where(rows == 0, 0.0, pltpu.roll(u, 1, 0))
    nxt = jnp.where(rows == l - 1, 0.0, pltpu.roll(u, l - 1, 0))
    uc = prev * cw_ref[0:1, :] + u * cw_ref[1:2, :] + nxt * cw_ref[2:3, :] + cb_ref[...]
    x0 = uc[:, :HY_W]
    x1 = uc[:, HY_W:2 * HY_W]
    v = uc[:, 2 * HY_W:] * x1
    ab = _bdot(fwd_ref[...], v)
    a, b = ab[:l], ab[l:]
    hs = hs_ref[...]
    re = a * hc_ref[...] - b * hs
    im = a * hs + b * hc2_ref[...]
    y = _bdot(inv_ref[...], jnp.concatenate([re, im], 0)) * (1.0 / (2 * l))
    o_ref[...] = ((y + skip_ref[...] * v) * x0).astype(o_ref.dtype)


def _hyena(u_all, row_block0, n_seq, l, conv_w, conv_b, skip, filt):
    hc, hs, hc2, fwd, inv = filt
    const = lambda shape: pl.BlockSpec(shape, lambda s: (0, 0))
    return pl.pallas_call(
        functools.partial(_hyena_kernel, l=l),
        grid=(n_seq,),
        in_specs=[pl.BlockSpec((l, 3 * HY_W), lambda s: (row_block0 + s, 0)),
                  const((3, 3 * HY_W)), const((1, 3 * HY_W)), const((1, HY_W)),
                  const((2 * l, l)), const((l, 2 * l)),
                  const((l, HY_W)), const((l, HY_W)), const((l, HY_W))],
        out_specs=pl.BlockSpec((l, HY_W), lambda s: (s, 0)),
        out_shape=jax.ShapeDtypeStruct((n_seq * l, HY_W), BF16),
        compiler_params=_cparams("parallel"),
        name=f"hyena_{l}",
    )(u_all, conv_w, conv_b.reshape(1, -1), skip.reshape(1, -1), fwd, inv, hc, hs, hc2)


def _seg_rms_norm(x, bd_ref, g):
    sq = x * x
    hi = sq.astype(BF16)
    lo = (sq - hi.astype(F32)).astype(BF16)
    bd = bd_ref[...]
    ss = (jnp.dot(hi, bd, preferred_element_type=F32) + jnp.dot(lo, bd, preferred_element_type=F32))
    return x * lax.rsqrt(ss * (1.0 / HEAD_DIM) + EPS) * g


def _rope(x, cos, sin_a, sin_b, quarter):
    w = x.shape[-1]
    return x * cos + pltpu.roll(x, w - quarter, 1) * sin_a + pltpu.roll(x, quarter, 1) * sin_b


def _attn_kernel(*refs, l, lc, rope, qknorm, band, has_sink, emit_k):
    it = iter(refs)
    q_ref, k_ref, v_ref = next(it), next(it), next(it)
    if lc:
        ck_ref, cv_ref = next(it), next(it)
    if rope:
        cq_ref, saq_ref, sbq_ref = next(it), next(it), next(it)
        ckk_ref, sak_ref, sbk_ref = next(it), next(it), next(it)
    if qknorm:
        qg_ref, kg_ref, bdq_ref, bdk_ref = next(it), next(it), next(it), next(it)
    if has_sink:
        sink_ref = next(it)
    o_ref = next(it)
    if emit_k:
        kout_ref = next(it)
    kp_scr = next(it)

    qi = pl.program_id(1)

    @pl.when(qi == 0)
    def _():
        k = k_ref[...]
        if qknorm:
            k = _seg_rms_norm(k, bdk_ref, kg_ref[...])
        if emit_k:
            kout_ref[...] = k
        if rope:
            k = _rope(k, ckk_ref[...], sak_ref[...], sbk_ref[...], HEAD_DIM // 4)
        for kv in range(N_KV):
            kp_scr[kv] = k[:, kv * HEAD_DIM:(kv + 1) * HEAD_DIM].astype(BF16)

    q = q_ref[...]
    if qknorm:
        q = _seg_rms_norm(q, bdq_ref, qg_ref[...])
    if rope:
        q = _rope(q, cq_ref[...], saq_ref[...], sbq_ref[...], HEAD_DIM // 4)
    q = q * HEAD_DIM ** -0.5

    rows = GROUPS * BLOCK
    if band:
        tq = qi * BLOCK + lax.broadcasted_iota(jnp.int32, (rows, 1), 0) % BLOCK
        tk = lax.broadcasted_iota(jnp.int32, (1, l), 1)
        valid = jnp.abs(tq - tk) <= BLOCK

    outs = []
    for kv in range(N_KV):
        lanes = slice(kv * HEAD_DIM, (kv + 1) * HEAD_DIM)
        qs = jnp.concatenate([q[:, (kv * GROUPS + g) * HEAD_DIM:(kv * GROUPS + g + 1) * HEAD_DIM]
                              for g in range(GROUPS)], 0)
        s = _bdot_nt(qs, kp_scr[kv])
        if band:
            s = jnp.where(valid, s, -jnp.inf)
        m = jnp.max(s, -1, keepdims=True)
        if lc:
            s_c = _bdot_nt(qs, ck_ref[:, lanes])
            m = jnp.maximum(m, jnp.max(s_c, -1, keepdims=True))
        if has_sink:
            sk = jnp.concatenate([jnp.full((BLOCK, 1), sink_ref[kv * GROUPS + g], F32)
                                  for g in range(GROUPS)], 0)
            m = jnp.maximum(m, sk)
        e = jnp.exp(s - m)
        den = jnp.sum(e, -1, keepdims=True)
        o = _bdot(e, v_ref[:, lanes])
        if lc:
            e_c = jnp.exp(s_c - m)
            den = den + jnp.sum(e_c, -1, keepdims=True)
            o = o + _bdot(e_c, cv_ref[:, lanes])
        if has_sink:
            den = den + jnp.exp(sk - m)
        o = o / den
        outs.extend(o[g * BLOCK:(g + 1) * BLOCK] for g in range(GROUPS))
    o_ref[...] = jnp.concatenate(outs, 1).astype(o_ref.dtype)


def _attention(u, *, row0, n_seq, l, q_col, k_col, v_col, ctx=None, rope=False, qk_gain=None,
               band=False, sink=None, emit_k=False):
    nq = l // BLOCK
    qw = N_HEADS * HEAD_DIM
    kw = N_KV * HEAD_DIM
    qb0, sb0 = row0 // BLOCK, row0 // l
    seq_spec = lambda col: pl.BlockSpec((l, kw), lambda b, i: (sb0 + b, col // kw))
    const = lambda shape: pl.BlockSpec(shape, lambda b, i: (0, 0))
    in_specs = [pl.BlockSpec((BLOCK, qw), lambda b, i: (qb0 + b * nq + i, q_col // qw)),
                seq_spec(k_col), seq_spec(v_col)]
    args = [u, u, u]
    lc = 0
    if ctx is not None:
        lc = ctx[0].shape[1]
        in_specs += [pl.BlockSpec((None, lc, kw), lambda b, i: (b, 0, 0))] * 2
        args += list(ctx)
    if rope:
        tabs = _rope_tables(l, HEAD_DIM, N_HEADS)
        in_specs += [pl.BlockSpec((BLOCK, qw), lambda b, i: (i, 0))] * 3 + [const((l, kw))] * 3
        args += list(tabs) + list(tabs)
    if qk_gain is not None:
        bd = np.kron(np.eye(N_HEADS), np.ones((HEAD_DIM, HEAD_DIM)))
        in_specs += [const((1, qw)), const((1, kw)), const((qw, qw)), const((kw, kw))]
        args += [jnp.tile(qk_gain[0], N_HEADS).reshape(1, qw), jnp.tile(qk_gain[1], N_KV).reshape(1, kw),
                 jnp.asarray(bd, BF16), jnp.asarray(bd[:kw, :kw], BF16)]
    if sink is not None:
        in_specs.append(pl.BlockSpec(memory_space=pltpu.SMEM))
        args.append(sink)
    out_specs = [pl.BlockSpec((BLOCK, qw), lambda b, i: (b * nq + i, 0))]
    out_shape = [jax.ShapeDtypeStruct((n_seq * l, qw), BF16)]
    if emit_k:
        out_specs.append(pl.BlockSpec((l, kw), lambda b, i: (b, 0)))
        out_shape.append(jax.ShapeDtypeStruct((n_seq * l, kw), F32))
    kern = functools.partial(_attn_kernel, l=l, lc=lc, rope=rope, qknorm=qk_gain is not None, band=band,
                             has_sink=sink is not None, emit_k=emit_k)
    return pl.pallas_call(
        kern,
        grid=(n_seq, nq),
        in_specs=in_specs,
        out_specs=out_specs,
        out_shape=out_shape,
        scratch_shapes=[pltpu.VMEM((N_KV, l, HEAD_DIM), BF16)],
        compiler_params=_cparams("parallel", "arbitrary"),
        name=f"attn_{l}_{'b' if sink is not None else 'c'}",
    )(*args)


def _ret_kernel(*refs, l, rope, has_s0, emit_state):
    it = iter(refs)
    dec_ref = next(it)
    rq_ref, rk_ref, rv_ref, gf_ref, gb_ref = next(it), next(it), next(it), next(it), next(it)
    if rope:
        cos_ref, sa_ref, sb_ref = next(it), next(it), next(it)
    if has_s0:
        s0_ref = next(it)
    o_ref = next(it)
    if emit_state:
        st_ref = next(it)

    h = pl.program_id(1)
    q = rq_ref[...] * RET_D ** -0.5
    k = rk_ref[...]
    v = rv_ref[...]
    if rope:
        q = _rope(q, cos_ref[...], sa_ref[...], sb_ref[...], RET_D // 4)
        k = _rope(k, cos_ref[...], sa_ref[...], sb_ref[...], RET_D // 4)
    n = l // CHUNK
    ii = lax.broadcasted_iota(jnp.int32, (CHUNK, 1), 0).astype(F32)
    jj = lax.broadcasted_iota(jnp.int32, (1, CHUNK), 1).astype(F32)
    diff = ii - jj
    y = None
    for d in range(2):
        log_g = jnp.log(jax.nn.sigmoid(jnp.full((1, 1), dec_ref[d, h], F32)))
        if d == 0:
            mask = jnp.exp(jnp.where(diff >= 0, diff * log_g, -jnp.inf))
            q_dec = jnp.exp((ii + 1.0) * log_g)
            k_dec = jnp.exp((CHUNK - 1.0 - ii) * log_g)
            order = range(n)
        else:
            mask = jnp.exp(jnp.where(diff <= 0, -diff * log_g, -jnp.inf))
            q_dec = jnp.exp((CHUNK - ii) * log_g)
            k_dec = jnp.exp(ii * log_g)
            order = reversed(range(n))
        c_dec = jnp.exp(CHUNK * log_g)
        state = s0_ref[d] if has_s0 else jnp.zeros((RET_D, RET_D), F32)
        o_chunks = [None] * n
        for c in order:
            sl = slice(c * CHUNK, (c + 1) * CHUNK)
            qc, kc, vc = q[sl], k[sl], v[sl]
            inner = _bdot_nt(qc, kc) * mask
            o_chunks[c] = _bdot(inner, vc) + _bdot(qc * q_dec, state)
            state = state * c_dec + _bdot_tn(kc * k_dec, vc)
        if emit_state:
            st_ref[d] = state
        o = jnp.concatenate(o_chunks, 0)
        o = o * lax.rsqrt(jnp.mean(o * o, -1, keepdims=True) + EPS)
        gate = _silu((gf_ref if d == 0 else gb_ref)[...])
        y = gate * o if y is None else y + gate * o
    o_ref[...] = y.astype(o_ref.dtype)


def _retention(u, ret_decay, *, row0, n_seq, l, rope=False, s0=None, emit_state=False):
    sb0 = row0 // l
    col = lambda c0: pl.BlockSpec((l, RET_D), lambda b, h: (sb0 + b, c0 // RET_D + h))
    in_specs = [pl.BlockSpec(memory_space=pltpu.SMEM),
                col(CD_RQ), col(CD_RK), col(CD_RV), col(CD_GF), col(CD_GB)]
    args = [ret_decay, u, u, u, u, u]
    if rope:
        in_specs += [pl.BlockSpec((l, RET_D), lambda b, h: (0, 0))] * 3
        args += list(_rope_tables(l, RET_D, 1))
    state_spec = pl.BlockSpec((None, 2, None, RET_D, RET_D), lambda b, h: (b, 0, h, 0, 0))
    if s0 is not None:
        in_specs.append(state_spec)
        args.append(s0)
    out_specs = [pl.BlockSpec((l, RET_D), lambda b, h: (b, h))]
    out_shape = [jax.ShapeDtypeStruct((n_seq * l, RET_HEADS * RET_D), BF16)]
    if emit_state:
        out_specs.append(state_spec)
        out_shape.append(jax.ShapeDtypeStruct((n_seq, 2, RET_HEADS, RET_D, RET_D), F32))
    kern = functools.partial(_ret_kernel, l=l, rope=rope, has_s0=s0 is not None, emit_state=emit_state)
    return pl.pallas_call(
        kern,
        grid=(n_seq, RET_HEADS),
        in_specs=in_specs,
        out_specs=out_specs,
        out_shape=out_shape,
        compiler_params=_cparams("parallel", "parallel"),
        name=f"retention_{l}",
    )(*args)


def _outproj_kernel(ya_ref, yb_ref, w_ref, x_ref, gate_ref, lng_ref, lnb_ref, o_ref, w_scr, *, tm):
    i = pl.program_id(0)

    @pl.when(i == 0)
    def _():
        w_scr[...] = w_ref[...].astype(BF16)

    half = ya_ref.shape[1]
    m = (jnp.dot(ya_ref[...], w_scr[:half], preferred_element_type=F32)
         + jnp.dot(yb_ref[...], w_scr[half:], preferred_element_type=F32))
    gate = gate_ref[pl.ds(_group_of_tile(i, tm), 1), :]
    o_ref[...] = _layer_norm(ALPHA * x_ref[...] + gate * m, lng_ref[...], lnb_ref[...])


def _outproj(ya, yb, w, x, mod, ln_g, ln_b):
    tm = 512
    half = ya.shape[1]
    const = lambda shape: pl.BlockSpec(shape, lambda i: (0, 0))
    return pl.pallas_call(
        functools.partial(_outproj_kernel, tm=tm),
        grid=(T_ALL // tm,),
        in_specs=[pl.BlockSpec((tm, half), lambda i: (i, 0)),
                  pl.BlockSpec((tm, half), lambda i: (i, 0)),
                  const((2 * half, D_MODEL)),
                  pl.BlockSpec((tm, D_MODEL), lambda i: (i, 0)),
                  pl.BlockSpec((MOD_ROWS, D_MODEL), lambda i: (0, 2)),
                  const((1, D_MODEL)), const((1, D_MODEL))],
        out_specs=pl.BlockSpec((tm, D_MODEL), lambda i: (i, 0)),
        out_shape=jax.ShapeDtypeStruct((T_ALL, D_MODEL), F32),
        scratch_shapes=[pltpu.VMEM((2 * half, D_MODEL), BF16)],
        compiler_params=_cparams("arbitrary"),
        name="outproj_ln",
    )(ya, yb, w, x, mod, ln_g.reshape(1, -1), ln_b.reshape(1, -1))


def _router_kernel(x_ref, sh_ref, sc_ref, r_ref, o_ref, *, tm):
    g = _group_of_tile(pl.program_id(0), tm)
    h = x_ref[...] * (1.0 + sc_ref[pl.ds(g, 1), :]) + sh_ref[pl.ds(g, 1), :]
    logits = jnp.dot(h, r_ref[...], precision=lax.Precision.HIGHEST, preferred_element_type=F32)
    lane = lax.broadcasted_iota(jnp.int32, logits.shape, 1)
    logits = jnp.where(lane < N_EXPERTS, logits, -jnp.inf)
    m1 = jnp.max(logits, -1, keepdims=True)
    i1 = jnp.min(jnp.where(logits == m1, lane, LANES), -1, keepdims=True)
    rest = jnp.where(lane == i1, -jnp.inf, logits)
    m2 = jnp.max(rest, -1, keepdims=True)
    i2 = jnp.min(jnp.where(rest == m2, lane, LANES), -1, keepdims=True)
    e2 = jnp.exp(m2 - m1)
    den = 1.0 + e2
    o_ref[...] = jnp.where(lane == i1, 1.0 / den, 0.0) + jnp.where(lane == i2, e2 / den, 0.0)


def _router(x, mod, router):
    tm = 512
    return pl.pallas_call(
        functools.partial(_router_kernel, tm=tm),
        grid=(T_ALL // tm,),
        in_specs=[pl.BlockSpec((tm, D_MODEL), lambda i: (i, 0)),
                  pl.BlockSpec((MOD_ROWS, D_MODEL), lambda i: (0, 3)),
                  pl.BlockSpec((MOD_ROWS, D_MODEL), lambda i: (0, 4)),
                  pl.BlockSpec((D_MODEL, LANES), lambda i: (0, 0))],
        out_specs=pl.BlockSpec((tm, LANES), lambda i: (i, 0)),
        out_shape=jax.ShapeDtypeStruct((T_ALL, LANES), F32),
        compiler_params=_cparams("parallel"),
        name="router",
    )(x, mod, mod, jnp.pad(router, ((0, 0), (0, LANES - N_EXPERTS))))


FF_CHUNK = 256


def _ffn_kernel(*refs, tm, ff, moe):
    it = iter(refs)
    x_ref, sh_ref, sc_ref, gate_ref, lng_ref, lnb_ref = (next(it) for _ in range(6))
    w1_ref, w3_ref, w2_ref = next(it), next(it), next(it)
    if moe:
        dg_ref = next(it)
    o_ref, h_scr, acc_scr = next(it), next(it), next(it)

    i, j = pl.program_id(0), pl.program_id(1)
    g = _group_of_tile(i, tm)

    @pl.when(j == 0)
    def _():
        h_scr[...] = (x_ref[...] * (1.0 + sc_ref[pl.ds(g, 1), :]) + sh_ref[pl.ds(g, 1), :]).astype(BF16)
        acc_scr[...] = jnp.zeros_like(acc_scr)

    h = h_scr[...]
    if moe:
        dg = dg_ref[...]
        lane = lax.broadcasted_iota(jnp.int32, dg.shape, 1)
        gcol = jnp.sum(jnp.where(lane == j, dg, 0.0), -1, keepdims=True)
    for c0 in range(0, ff, FF_CHUNK):
        cs = slice(c0, min(c0 + FF_CHUNK, ff))
        a = jnp.dot(h, w1_ref[:, cs].astype(BF16), preferred_element_type=F32)
        b = jnp.dot(h, w3_ref[:, cs].astype(BF16), preferred_element_type=F32)
        act = _silu(a) * b
        if moe:
            act = act * gcol
        acc_scr[...] += jnp.dot(act.astype(BF16), w2_ref[cs, :].astype(BF16), preferred_element_type=F32)

    @pl.when(j == pl.num_programs(1) - 1)
    def _():
        z = ALPHA * x_ref[...] + gate_ref[pl.ds(g, 1), :] * acc_scr[...]
        o_ref[...] = _layer_norm(z, lng_ref[...], lnb_ref[...])


def _ffn(x, mod, ln_g, ln_b, w1, w3, w2, dense_g=None):
    tm = 512
    moe = dense_g is not None
    ff = EXPERT_FF
    if moe:
        n_j = N_EXPERTS
        w_in_spec = pl.BlockSpec((None, D_MODEL, ff), lambda i, j: (j, 0, 0))
        w_out_spec = pl.BlockSpec((None, ff, D_MODEL), lambda i, j: (j, 0, 0))
    else:
        n_j = D_FF // ff
        w_in_spec = pl.BlockSpec((D_MODEL, ff), lambda i, j: (0, j))
        w_out_spec = pl.BlockSpec((ff, D_MODEL), lambda i, j: (j, 0))
    mod_spec = lambda col: pl.BlockSpec((MOD_ROWS, D_MODEL), lambda i, j: (0, col))
    const = lambda shape: pl.BlockSpec(shape, lambda i, j: (0, 0))
    in_specs = [pl.BlockSpec((tm, D_MODEL), lambda i, j: (i, 0)),
                mod_spec(3), mod_spec(4), mod_spec(5),
                const((1, D_MODEL)), const((1, D_MODEL)),
                w_in_spec, w_in_spec, w_out_spec]
    args = [x, mod, mod, mod, ln_g.reshape(1, -1), ln_b.reshape(1, -1), w1, w3, w2]
    if moe:
        in_specs.append(pl.BlockSpec((tm, LANES), lambda i, j: (i, 0)))
        args.append(dense_g)
    return pl.pallas_call(
        functools.partial(_ffn_kernel, tm=tm, ff=ff, moe=moe),
        grid=(T_ALL // tm, n_j),
        in_specs=in_specs,
        out_specs=pl.BlockSpec((tm, D_MODEL), lambda i, j: (i, 0)),
        out_shape=jax.ShapeDtypeStruct((T_ALL, D_MODEL), F32),
        scratch_shapes=[pltpu.VMEM((tm, D_MODEL), BF16), pltpu.VMEM((tm, D_MODEL), F32)],
        compiler_params=_cparams("parallel", "arbitrary"),
        name="moe_ffn" if moe else "ffn",
    )(*args)


def kernel(x_prompt, x_sample, c, cache_k_b, cache_v_b, cache_k_c, cache_v_c, state_ret, c_ctx, ada_w, ada_b, ln_g, ln_b, w_in_ab, hy_conv_w, hy_conv_b, hf_w1, hf_b1, hf_freq, hf_w2, hf_b2, hf_w3, hy_skip, sink_b, w_out_ab, ffn_w1, ffn_w3, ffn_w2, w_in_cd, qn_g, kn_g, ret_decay, w_out_cd, moe_router, moe_w1, moe_w3, moe_w2):
    x = jnp.concatenate([x_prompt.reshape(T_PROMPT, D_MODEL), x_sample.reshape(T_SAMPLE, D_MODEL)], 0)
    cvec = jnp.concatenate([c_ctx[None], c, jnp.zeros((MOD_ROWS - 1 - DEC_BATCH, D_MODEL), F32)], 0)
    mod = _modulation(cvec, ada_w, ada_b)
    kw = N_KV * HEAD_DIM
    prompt = dict(row0=0, n_seq=BATCH, l=SEQ)
    sample = dict(row0=T_PROMPT, n_seq=DEC_BATCH, l=DEC_SEQ)

    u = _inproj(x, mod[0], w_in_ab[0], tn=768)
    filt_args = (hf_w1[0], hf_b1[0], hf_freq[0], hf_w2[0], hf_b2[0], hf_w3[0])
    hy_args = (hy_conv_w[0], hy_conv_b[0], hy_skip[0])
    ya = jnp.concatenate([
        _hyena(u, 0, BATCH, SEQ, *hy_args, _hyena_filter(SEQ, *filt_args)),
        _hyena(u, T_PROMPT // DEC_SEQ, DEC_BATCH, DEC_SEQ, *hy_args, _hyena_filter(DEC_SEQ, *filt_args))], 0)
    cols_b = dict(q_col=AB_Q, k_col=AB_K, v_col=AB_V)
    ctx_b = (cache_k_b[:, 0].reshape(DEC_BATCH, PAST_LEN, kw), cache_v_b[:, 0].reshape(DEC_BATCH, PAST_LEN, kw))
    yb_p, = _attention(u, **prompt, **cols_b, sink=sink_b[0])
    yb_s, = _attention(u, **sample, **cols_b, sink=sink_b[0], ctx=ctx_b, rope=True, band=True)
    yb = jnp.concatenate([yb_p, yb_s], 0)
    k_b = u[:T_PROMPT, AB_K:AB_V].reshape(BATCH, 1, SEQ, N_KV, HEAD_DIM)
    v_b = u[:T_PROMPT, AB_V:IN_AB].reshape(BATCH, 1, SEQ, N_KV, HEAD_DIM)
    x = _outproj(ya, yb, w_out_ab[0], x, mod[0], ln_g[0, 0], ln_b[0, 0])
    x = _ffn(x, mod[0], ln_g[0, 1], ln_b[0, 1], ffn_w1[0], ffn_w3[0], ffn_w2[0])

    u = _inproj(x, mod[1], w_in_cd[0], tn=1664)
    cols_c = dict(q_col=CD_Q, k_col=CD_K, v_col=CD_V)
    gains = (qn_g[0], kn_g[0])
    ctx_c = (cache_k_c[:, 0].reshape(DEC_BATCH, PAST_LEN, kw), cache_v_c[:, 0].reshape(DEC_BATCH, PAST_LEN, kw))
    yc_p, k_c = _attention(u, **prompt, **cols_c, qk_gain=gains, emit_k=True)
    yc_s, = _attention(u, **sample, **cols_c, qk_gain=gains, ctx=ctx_c, rope=True)
    yd_p, s_r = _retention(u, ret_decay[0], **prompt, emit_state=True)
    yd_s, = _retention(u, ret_decay[0], **sample, rope=True, s0=state_ret[:, 0])
    k_c = k_c.reshape(BATCH, 1, SEQ, N_KV, HEAD_DIM)
    v_c = u[:T_PROMPT, CD_V:CD_RQ].reshape(BATCH, 1, SEQ, N_KV, HEAD_DIM)
    x = _outproj(jnp.concatenate([yc_p, yc_s], 0), jnp.concatenate([yd_p, yd_s], 0), w_out_cd[0], x, mod[1],
                 ln_g[1, 0], ln_b[1, 0])
    dense_g = _router(x, mod[1], moe_router[0])
    x = _ffn(x, mod[1], ln_g[1, 1], ln_b[1, 1], moe_w1[0], moe_w3[0], moe_w2[0], dense_g=dense_g)

    y_prompt = x[:T_PROMPT].reshape(BATCH, SEQ, D_MODEL)
    y_sample = x[T_PROMPT:].reshape(DEC_BATCH, DEC_SEQ, D_MODEL)
    return y_prompt, y_sample, k_b, v_b, k_c, v_c, s_r[:, None]
```

```python
import functools
import math

import numpy as np
import jax
import jax.numpy as jnp
from jax import lax
from jax.experimental import pallas as pl
from jax.experimental.pallas import tpu as pltpu

F32 = jnp.float32
BF16 = jnp.bfloat16

D_MODEL = 1024
BATCH = 16
SEQ = 256
DEC_BATCH = 2
DEC_SEQ = 1024
PAST_LEN = 512
GRID_W = 64
HEAD_DIM = 64
BLOCK = 128
HY_W = 512
POS_BANDS = 16
POS_EMB = 1 + 2 * POS_BANDS
FILT_HID = 64
HY_FAST_DECAY = 0.3
HY_SLOW_DECAY = 1.5
HY_TARGET = 1e-2
N_HEADS = 8
N_KV = 2
GROUPS = N_HEADS // N_KV
RET_HEADS = 4
RET_D = 128
CHUNK = 128
ROPE_BASE = 10000.0
D_FF = 2816
N_EXPERTS = 8
EXPERT_FF = 1408
DEPTH = 2
ALPHA = (2 * DEPTH) ** 0.25
EPS = 1e-6

T_PROMPT = BATCH * SEQ
T_SAMPLE = DEC_BATCH * DEC_SEQ
T_ALL = T_PROMPT + T_SAMPLE
GROUP_ROWS = 1024
N_PROMPT_GROUPS = T_PROMPT // GROUP_ROWS
MOD_ROWS = 16
LANES = 128
VMEM_LIMIT = 58 * 1024 * 1024

AB_Q = 3 * HY_W
AB_K = AB_Q + N_HEADS * HEAD_DIM
AB_V = AB_K + N_KV * HEAD_DIM
IN_AB = AB_V + N_KV * HEAD_DIM
CD_Q = 0
CD_K = N_HEADS * HEAD_DIM
CD_V = CD_K + N_KV * HEAD_DIM
CD_RQ = CD_V + N_KV * HEAD_DIM
CD_RK = CD_RQ + RET_HEADS * RET_D
CD_RV = CD_RK + RET_HEADS * RET_D
CD_GF = CD_RV + RET_HEADS * RET_D
CD_GB = CD_GF + RET_HEADS * RET_D
IN_CD = CD_GB + RET_HEADS * RET_D


def _cparams(*sem):
    return pltpu.CompilerParams(dimension_semantics=sem, vmem_limit_bytes=VMEM_LIMIT)


def _silu(x):
    return x * jax.nn.sigmoid(x)


def _bdot(a, b):
    return jnp.dot(a.astype(BF16), b.astype(BF16), preferred_element_type=F32)


def _bdot_nt(a, b):
    return lax.dot_general(a.astype(BF16), b.astype(BF16), (((1,), (1,)), ((), ())),
                           preferred_element_type=F32)


def _bdot_tn(a, b):
    return lax.dot_general(a.astype(BF16), b.astype(BF16), (((0,), (0,)), ((), ())),
                           preferred_element_type=F32)


def _layer_norm(z, g, b):
    mu = jnp.mean(z, -1, keepdims=True)
    zc = z - mu
    var = jnp.mean(zc * zc, -1, keepdims=True)
    return zc * lax.rsqrt(var + EPS) * g + b


def _group_of_tile(i, tm):
    return jnp.maximum(i // (GROUP_ROWS // tm) - (N_PROMPT_GROUPS - 1), 0)


def _rope_tables(n_tokens, d, reps):
    nf = d // 4
    inv = ROPE_BASE ** (-np.arange(nf, dtype=np.float64) / nf)
    pos = np.arange(n_tokens)
    row, col = pos // GRID_W, pos % GRID_W
    ang_r = row[:, None] * inv[None, :]
    ang_c = col[:, None] * inv[None, :]
    zeros = np.zeros_like(ang_r)
    cos = np.concatenate([np.cos(ang_r), np.cos(ang_r), np.cos(ang_c), np.cos(ang_c)], -1)
    sin_a = np.concatenate([-np.sin(ang_r), zeros, -np.sin(ang_c), zeros], -1)
    sin_b = np.concatenate([zeros, np.sin(ang_r), zeros, np.sin(ang_c)], -1)
    tile = lambda a: jnp.asarray(np.tile(a, (1, reps)), F32)
    return tile(cos), tile(sin_a), tile(sin_b)


def _dft_mats(l):
    n = 2 * l
    k = np.arange(l, dtype=np.float64)
    ang = 2.0 * np.pi * np.outer(k, k) / n
    fc = np.cos(ang)
    fs = np.sin(ang)
    fs[0, :] = np.cos(np.pi * k)
    fwd = np.concatenate([fc, fs], 0)
    wk = np.full((l,), 2.0)
    wk[0] = 1.0
    inv = np.concatenate([fc.T * wk[None, :], fs.T * wk[None, :]], 1)
    return jnp.asarray(fwd, F32), jnp.asarray(inv, F32)


def _filter_consts(l):
    t = np.linspace(0.0, 1.0, l, dtype=np.float32).astype(np.float64)[:, None]
    w = (2.0 * math.pi * np.arange(l, dtype=np.float64) / l)[:, None]
    bands = np.linspace(1e-4, POS_BANDS - 1.0, POS_BANDS, dtype=np.float32).astype(np.float64)[None, :]
    z = np.concatenate([t, np.cos(bands * w), -np.sin(bands * w)], -1)
    z = np.pad(z, ((0, 0), (0, LANES - POS_EMB)))
    max_decay = math.log(HY_TARGET) / HY_FAST_DECAY
    min_decay = math.log(HY_TARGET) / HY_SLOW_DECAY
    deltas = np.linspace(min_decay, max_decay, HY_W, dtype=np.float32).astype(np.float64)
    window = np.exp(-t * np.abs(deltas)[None, :])
    return jnp.asarray(z, F32), jnp.asarray(window, F32)


def _mod_kernel(c_ref, w_ref, b_ref, o_ref):
    o_ref[...] = _bdot(_silu(c_ref[...]), w_ref[...]) + b_ref[...]


def _modulation(cvec, ada_w, ada_b):
    tn = 1536
    n = ada_w.shape[-1]
    return pl.pallas_call(
        _mod_kernel,
        grid=(DEPTH, n // tn),
        in_specs=[pl.BlockSpec((MOD_ROWS, D_MODEL), lambda l, j: (0, 0)),
                  pl.BlockSpec((None, D_MODEL, tn), lambda l, j: (l, 0, j)),
                  pl.BlockSpec((None, 1, tn), lambda l, j: (l, 0, j))],
        out_specs=pl.BlockSpec((None, MOD_ROWS, tn), lambda l, j: (l, 0, j)),
        out_shape=jax.ShapeDtypeStruct((DEPTH, MOD_ROWS, n), F32),
        compiler_params=_cparams("parallel", "parallel"),
        name="mod",
    )(cvec, ada_w, ada_b.reshape(DEPTH, 1, n))


def _inproj_kernel(x_ref, sh_ref, sc_ref, w_ref, o_ref, h_scr, *, tm):
    i, j = pl.program_id(0), pl.program_id(1)

    @pl.when(j == 0)
    def _():
        g = _group_of_tile(i, tm)
        sc = sc_ref[pl.ds(g, 1), :]
        sh = sh_ref[pl.ds(g, 1), :]
        h_scr[...] = (x_ref[...] * (1.0 + sc) + sh).astype(BF16)

    o_ref[...] = jnp.dot(h_scr[...], w_ref[...].astype(BF16), preferred_element_type=F32)


def _inproj(x, mod, w, tn):
    tm = 1024
    n = w.shape[1]
    return pl.pallas_call(
        functools.partial(_inproj_kernel, tm=tm),
        grid=(T_ALL // tm, n // tn),
        in_specs=[pl.BlockSpec((tm, D_MODEL), lambda i, j: (i, 0)),
                  pl.BlockSpec((MOD_ROWS, D_MODEL), lambda i, j: (0, 0)),
                  pl.BlockSpec((MOD_ROWS, D_MODEL), lambda i, j: (0, 1)),
                  pl.BlockSpec((D_MODEL, tn), lambda i, j: (0, j))],
        out_specs=pl.BlockSpec((tm, tn), lambda i, j: (i, j)),
        out_shape=jax.ShapeDtypeStruct((T_ALL, n), F32),
        scratch_shapes=[pltpu.VMEM((tm, D_MODEL), BF16)],
        compiler_params=_cparams("parallel", "arbitrary"),
        name="inproj",
    )(x, mod, mod, w)


def _hyena_filter_kernel(z_ref, w1_ref, b1_ref, fr_ref, w2_ref, b2_ref, w3_ref, win_ref, fwd32_ref, inv32_ref,
                         hc_ref, hs_ref, hc2_ref, fwd_ref, inv_ref, *, l):
    hi = lax.Precision.HIGHEST
    fwd = fwd32_ref[...].astype(BF16)
    fwd_ref[...] = fwd
    inv_ref[...] = inv32_ref[...].astype(BF16)
    fr = fr_ref[...]
    h = jnp.sin(fr * (jnp.dot(z_ref[...], w1_ref[...], precision=hi, preferred_element_type=F32) + b1_ref[...]))
    h = jnp.sin(fr * (jnp.dot(h, w2_ref[...], precision=hi, preferred_element_type=F32) + b2_ref[...]))
    h = jnp.dot(h, w3_ref[...], precision=hi, preferred_element_type=F32)
    win = win_ref[...]
    hf = h[:, :HY_W] * win
    hb = h[:, HY_W:] * win
    p = _bdot(fwd, hf + hb)
    q = _bdot(fwd, hf - hb)
    row0 = lax.broadcasted_iota(jnp.int32, (l, 1), 0) == 0
    hc = p[:l]
    hc_ref[...] = hc
    hs_ref[...] = jnp.where(row0, 0.0, q[l:])
    hc2_ref[...] = jnp.where(row0, p[l:l + 1], hc)


def _hyena_filter(l, fw1, fb1, ffreq, fw2, fb2, fw3):
    z, window = _filter_consts(l)
    fwd, inv = _dft_mats(l)
    pad_c = LANES - FILT_HID
    w1 = jnp.pad(fw1, ((0, LANES - POS_EMB), (0, pad_c)))
    w2 = jnp.pad(fw2, ((0, pad_c), (0, pad_c)))
    w3 = jnp.pad(fw3, ((0, pad_c), (0, 0)))
    row = lambda a: jnp.pad(a, (0, pad_c)).reshape(1, LANES)
    shp = jax.ShapeDtypeStruct((l, HY_W), F32)
    return pl.pallas_call(
        functools.partial(_hyena_filter_kernel, l=l),
        out_shape=(shp, shp, shp, jax.ShapeDtypeStruct(fwd.shape, BF16), jax.ShapeDtypeStruct(inv.shape, BF16)),
        compiler_params=pltpu.CompilerParams(vmem_limit_bytes=VMEM_LIMIT),
        name=f"hyena_filter_{l}",
    )(z, w1, row(fb1), row(ffreq), w2, row(fb2), w3, window, fwd, inv)


def _hyena_kernel(u_ref, cw_ref, cb_ref, skip_ref, fwd_ref, inv_ref, hc_ref, hs_ref, hc2_ref, o_ref, *, l):
    u = u_ref[...]
    rows = lax.broadcasted_iota(jnp.int32, (l, 1), 0)
    prev = jnp.where(rows == 0, 0.0, pltpu.roll(u, 1, 0))
    nxt = jnp.where(rows == l - 1, 0.0, pltpu.roll(u, l - 1, 0))
    uc = prev * cw_ref[0:1, :] + u * cw_ref[1:2, :] + nxt * cw_ref[2:3, :] + cb_ref[...]
    x0 = uc[:, :HY_W]
    x1 = uc[:, HY_W:2 * HY_W]
    v = uc[:, 2 * HY_W:] * x1
    ab = _bdot(fwd_ref[...], v)
    a, b = ab[:l], ab[l:]
    hs = hs_ref[...]
    re = a * hc_ref[...] - b * hs
    im = a * hs + b * hc2_ref[...]
    y = _bdot(inv_ref[...], jnp.concatenate([re, im], 0)) * (1.0 / (2 * l))
    o_ref[...] = ((y + skip_ref[...] * v) * x0).astype(o_ref.dtype)


def _hyena(u_all, row_block0, n_seq, l, conv_w, conv_b, skip, filt):
    hc, hs, hc2, fwd, inv = filt
    const = lambda shape: pl.BlockSpec(shape, lambda s: (0, 0))
    return pl.pallas_call(
        functools.partial(_hyena_kernel, l=l),
        grid=(n_seq,),
        in_specs=[pl.BlockSpec((l, 3 * HY_W), lambda s: (row_block0 + s, 0)),
                  const((3, 3 * HY_W)), const((1, 3 * HY_W)), const((1, HY_W)),
                  const((2 * l, l)), const((l, 2 * l)),
                  const((l, HY_W)), const((l, HY_W)), const((l, HY_W))],
        out_specs=pl.BlockSpec((l, HY_W), lambda s: (s, 0)),
        out_shape=jax.ShapeDtypeStruct((n_seq * l, HY_W), BF16),
        compiler_params=_cparams("parallel"),
        name=f"hyena_{l}",
    )(u_all, conv_w, conv_b.reshape(1, -1), skip.reshape(1, -1), fwd, inv, hc, hs, hc2)


def _seg_rms_norm(x, bd_ref, g):
    sq = x * x
    hi = sq.astype(BF16)
    lo = (sq - hi.astype(F32)).astype(BF16)
    bd = bd_ref[...]
    ss = (jnp.dot(hi, bd, preferred_element_type=F32) + jnp.dot(lo, bd, preferred_element_type=F32))
    return x * lax.rsqrt(ss * (1.0 / HEAD_DIM) + EPS) * g


def _rope(x, cos, sin_a, sin_b, quarter):
    w = x.shape[-1]
    return x * cos + pltpu.roll(x, w - quarter, 1) * sin_a + pltpu.roll(x, quarter, 1) * sin_b


def _attn_kernel(*refs, l, lc, rope, qknorm, band, has_sink, emit_k):
    it = iter(refs)
    q_ref, k_ref, v_ref = next(it), next(it), next(it)
    if lc:
        ck_ref, cv_ref = next(it), next(it)
    if rope:
        cq_ref, saq_ref, sbq_ref = next(it), next(it), next(it)
        ckk_ref, sak_ref, sbk_ref = next(it), next(it), next(it)
    if qknorm:
        qg_ref, kg_ref, bdq_ref, bdk_ref = next(it), next(it), next(it), next(it)
    if has_sink:
        sink_ref = next(it)
    o_ref = next(it)
    if emit_k:
        kout_ref = next(it)
    kp_scr = next(it)

    qi = pl.program_id(1)

    @pl.when(qi == 0)
    def _():
        k = k_ref[...]
        if qknorm:
            k = _seg_rms_norm(k, bdk_ref, kg_ref[...])
        if emit_k:
            kout_ref[...] = k
        if rope:
            k = _rope(k, ckk_ref[...], sak_ref[...], sbk_ref[...], HEAD_DIM // 4)
        for kv in range(N_KV):
            kp_scr[kv] = k[:, kv * HEAD_DIM:(kv + 1) * HEAD_DIM].astype(BF16)

    q = q_ref[...]
    if qknorm:
        q = _seg_rms_norm(q, bdq_ref, qg_ref[...])
    if rope:
        q = _rope(q, cq_ref[...], saq_ref[...], sbq_ref[...], HEAD_DIM // 4)
    q = q * HEAD_DIM ** -0.5

    rows = GROUPS * BLOCK
    if band:
        tq = qi * BLOCK + lax.broadcasted_iota(jnp.int32, (rows, 1), 0) % BLOCK
        tk = lax.broadcasted_iota(jnp.int32, (1, l), 1)
        valid = jnp.abs(tq - tk) <= BLOCK

    outs = []
    for kv in range(N_KV):
        lanes = slice(kv * HEAD_DIM, (kv + 1) * HEAD_DIM)
        qs = jnp.concatenate([q[:, (kv * GROUPS + g) * HEAD_DIM:(kv * GROUPS + g + 1) * HEAD_DIM]
                              for g in range(GROUPS)], 0)
        s = _bdot_nt(qs, kp_scr[kv])
        if band:
            s = jnp.where(valid, s, -jnp.inf)
        m = jnp.max(s, -1, keepdims=True)
        if lc:
            s_c = _bdot_nt(qs, ck_ref[:, lanes])
            m = jnp.maximum(m, jnp.max(s_c, -1, keepdims=True))
        if has_sink:
            sk = jnp.concatenate([jnp.full((BLOCK, 1), sink_ref[kv * GROUPS + g], F32)
                                  for g in range(GROUPS)], 0)
            m = jnp.maximum(m, sk)
        e = jnp.exp(s - m)
        den = jnp.sum(e, -1, keepdims=True)
        o = _bdot(e, v_ref[:, lanes])
        if lc:
            e_c = jnp.exp(s_c - m)
            den = den + jnp.sum(e_c, -1, keepdims=True)
            o = o + _bdot(e_c, cv_ref[:, lanes])
        if has_sink:
            den = den + jnp.exp(sk - m)
        o = o / den
        outs.extend(o[g * BLOCK:(g + 1) * BLOCK] for g in range(GROUPS))
    o_ref[...] = jnp.concatenate(outs, 1).astype(o_ref.dtype)


def _attention(u, *, row0, n_seq, l, q_col, k_col, v_col, ctx=None, rope=False, qk_gain=None,
               band=False, sink=None, emit_k=False):
    nq = l // BLOCK
    qw = N_HEADS * HEAD_DIM
    kw = N_KV * HEAD_DIM
    qb0, sb0 = row0 // BLOCK, row0 // l
    seq_spec = lambda col: pl.BlockSpec((l, kw), lambda b, i: (sb0 + b, col // kw))
    const = lambda shape: pl.BlockSpec(shape, lambda b, i: (0, 0))
    in_specs = [pl.BlockSpec((BLOCK, qw), lambda b, i: (qb0 + b * nq + i, q_col // qw)),
                seq_spec(k_col), seq_spec(v_col)]
    args = [u, u, u]
    lc = 0
    if ctx is not None:
        lc = ctx[0].shape[1]
        in_specs += [pl.BlockSpec((None, lc, kw), lambda b, i: (b, 0, 0))] * 2
        args += list(ctx)
    if rope:
        tabs = _rope_tables(l, HEAD_DIM, N_HEADS)
        in_specs += [pl.BlockSpec((BLOCK, qw), lambda b, i: (i, 0))] * 3 + [const((l, kw))] * 3
        args += list(tabs) + list(tabs)
    if qk_gain is not None:
        bd = np.kron(np.eye(N_HEADS), np.ones((HEAD_DIM, HEAD_DIM)))
        in_specs += [const((1, qw)), const((1, kw)), const((qw, qw)), const((kw, kw))]
        args += [jnp.tile(qk_gain[0], N_HEADS).reshape(1, qw), jnp.tile(qk_gain[1], N_KV).reshape(1, kw),
                 jnp.asarray(bd, BF16), jnp.asarray(bd[:kw, :kw], BF16)]
    if sink is not None:
        in_specs.append(pl.BlockSpec(memory_space=pltpu.SMEM))
        args.append(sink)
    out_specs = [pl.BlockSpec((BLOCK, qw), lambda b, i: (b * nq + i, 0))]
    out_shape = [jax.ShapeDtypeStruct((n_seq * l, qw), BF16)]
    if emit_k:
        out_specs.append(pl.BlockSpec((l, kw), lambda b, i: (b, 0)))
        out_shape.append(jax.ShapeDtypeStruct((n_seq * l, kw), F32))
    kern = functools.partial(_attn_kernel, l=l, lc=lc, rope=rope, qknorm=qk_gain is not None, band=band,
                             has_sink=sink is not None, emit_k=emit_k)
    return pl.pallas_call(
        kern,
        grid=(n_seq, nq),
        in_specs=in_specs,
        out_specs=out_specs,
        out_shape=out_shape,
        scratch_shapes=[pltpu.VMEM((N_KV, l, HEAD_DIM), BF16)],
        compiler_params=_cparams("parallel", "arbitrary"),
        name=f"attn_{l}_{'b' if sink is not None else 'c'}",
    )(*args)


def _ret_kernel(*refs, l, rope, has_s0, emit_state):
    it = iter(refs)
    dec_ref = next(it)
    rq_ref, rk_ref, rv_ref, gf_ref, gb_ref = next(it), next(it), next(it), next(it), next(it)
    if rope:
        cos_ref, sa_ref, sb_ref = next(it), next(it), next(it)
    if has_s0:
        s0_ref = next(it)
    o_ref = next(it)
    if emit_state:
        st_ref = next(it)

    h = pl.program_id(1)
    q = rq_ref[...] * RET_D ** -0.5
    k = rk_ref[...]
    v = rv_ref[...]
    if rope:
        q = _rope(q, cos_ref[...], sa_ref[...], sb_ref[...], RET_D // 4)
        k = _rope(k, cos_ref[...], sa_ref[...], sb_ref[...], RET_D // 4)
    n = l // CHUNK
    ii = lax.broadcasted_iota(jnp.int32, (CHUNK, 1), 0).astype(F32)
    jj = lax.broadcasted_iota(jnp.int32, (1, CHUNK), 1).astype(F32)
    diff = ii - jj
    y = None
    for d in range(2):
        log_g = jnp.log(jax.nn.sigmoid(jnp.full((1, 1), dec_ref[d, h], F32)))
        if d == 0:
            mask = jnp.exp(jnp.where(diff >= 0, diff * log_g, -jnp.inf))
            q_dec = jnp.exp((ii + 1.0) * log_g)
            k_dec = jnp.exp((CHUNK - 1.0 - ii) * log_g)
            order = range(n)
        else:
            mask = jnp.exp(jnp.where(diff <= 0, -diff * log_g, -jnp.inf))
            q_dec = jnp.exp((CHUNK - ii) * log_g)
            k_dec = jnp.exp(ii * log_g)
            order = reversed(range(n))
        c_dec = jnp.exp(CHUNK * log_g)
        state = s0_ref[d] if has_s0 else jnp.zeros((RET_D, RET_D), F32)
        o_chunks = [None] * n
        for c in order:
            sl = slice(c * CHUNK, (c + 1) * CHUNK)
            qc, kc, vc = q[sl], k[sl], v[sl]
            inner = _bdot_nt(qc, kc) * mask
            o_chunks[c] = _bdot(inner, vc) + _bdot(qc * q_dec, state)
            state = state * c_dec + _bdot_tn(kc * k_dec, vc)
        if emit_state:
            st_ref[d] = state
        o = jnp.concatenate(o_chunks, 0)
        o = o * lax.rsqrt(jnp.mean(o * o, -1, keepdims=True) + EPS)
        gate = _silu((gf_ref if d == 0 else gb_ref)[...])
        y = gate * o if y is None else y + gate * o
    o_ref[...] = y.astype(o_ref.dtype)


def _retention(u, ret_decay, *, row0, n_seq, l, rope=False, s0=None, emit_state=False):
    sb0 = row0 // l
    col = lambda c0: pl.BlockSpec((l, RET_D), lambda b, h: (sb0 + b, c0 // RET_D + h))
    in_specs = [pl.BlockSpec(memory_space=pltpu.SMEM),
                col(CD_RQ), col(CD_RK), col(CD_RV), col(CD_GF), col(CD_GB)]
    args = [ret_decay, u, u, u, u, u]
    if rope:
        in_specs += [pl.BlockSpec((l, RET_D), lambda b, h: (0, 0))] * 3
        args += list(_rope_tables(l, RET_D, 1))
    state_spec = pl.BlockSpec((None, 2, None, RET_D, RET_D), lambda b, h: (b, 0, h, 0, 0))
    if s0 is not None:
        in_specs.append(state_spec)
        args.append(s0)
    out_specs = [pl.BlockSpec((l, RET_D), lambda b, h: (b, h))]
    out_shape = [jax.ShapeDtypeStruct((n_seq * l, RET_HEADS * RET_D), BF16)]
    if emit_state:
        out_specs.append(state_spec)
        out_shape.append(jax.ShapeDtypeStruct((n_seq, 2, RET_HEADS, RET_D, RET_D), F32))
    kern = functools.partial(_ret_kernel, l=l, rope=rope, has_s0=s0 is not None, emit_state=emit_state)
    return pl.pallas_call(
        kern,
        grid=(n_seq, RET_HEADS),
        in_specs=in_specs,
        out_specs=out_specs,
        out_shape=out_shape,
        compiler_params=_cparams("parallel", "parallel"),
        name=f"retention_{l}",
    )(*args)


def _outproj_kernel(ya_ref, yb_ref, w_ref, x_ref, gate_ref, lng_ref, lnb_ref, o_ref, w_scr, *, tm):
    i = pl.program_id(0)

    @pl.when(i == 0)
    def _():
        w_scr[...] = w_ref[...].astype(BF16)

    half = ya_ref.shape[1]
    m = (jnp.dot(ya_ref[...], w_scr[:half], preferred_element_type=F32)
         + jnp.dot(yb_ref[...], w_scr[half:], preferred_element_type=F32))
    gate = gate_ref[pl.ds(_group_of_tile(i, tm), 1), :]
    o_ref[...] = _layer_norm(ALPHA * x_ref[...] + gate * m, lng_ref[...], lnb_ref[...])


def _outproj(ya, yb, w, x, mod, ln_g, ln_b):
    tm = 512
    half = ya.shape[1]
    const = lambda shape: pl.BlockSpec(shape, lambda i: (0, 0))
    return pl.pallas_call(
        functools.partial(_outproj_kernel, tm=tm),
        grid=(T_ALL // tm,),
        in_specs=[pl.BlockSpec((tm, half), lambda i: (i, 0)),
                  pl.BlockSpec((tm, half), lambda i: (i, 0)),
                  const((2 * half, D_MODEL)),
                  pl.BlockSpec((tm, D_MODEL), lambda i: (i, 0)),
                  pl.BlockSpec((MOD_ROWS, D_MODEL), lambda i: (0, 2)),
                  const((1, D_MODEL)), const((1, D_MODEL))],
        out_specs=pl.BlockSpec((tm, D_MODEL), lambda i: (i, 0)),
        out_shape=jax.ShapeDtypeStruct((T_ALL, D_MODEL), F32),
        scratch_shapes=[pltpu.VMEM((2 * half, D_MODEL), BF16)],
        compiler_params=_cparams("arbitrary"),
        name="outproj_ln",
    )(ya, yb, w, x, mod, ln_g.reshape(1, -1), ln_b.reshape(1, -1))


TOK_TILE = 256
N_TOK_TILES = T_ALL // TOK_TILE
SORT_TILE = 256
N_SORT_TILES = (2 * T_ALL) // SORT_TILE + N_EXPERTS
CUM_ROWS = 32


def _route_kernel(x_ref, sh_ref, sc_ref, r_ref, h_ref, rank_ref, rank_t_ref, gate_t_ref, cum_ref,
                  carry_row, carry_col):
    c = pl.program_id(0)

    @pl.when(c == 0)
    def _():
        carry_row[...] = jnp.zeros_like(carry_row)
        carry_col[...] = jnp.zeros_like(carry_col)
        cum_ref[...] = jnp.zeros_like(cum_ref)

    g = _group_of_tile(c, TOK_TILE)
    h = x_ref[...] * (1.0 + sc_ref[pl.ds(g, 1), :]) + sh_ref[pl.ds(g, 1), :]
    h_ref[...] = h.astype(BF16)
    logits = jnp.dot(h, r_ref[...], precision=lax.Precision.HIGHEST, preferred_element_type=F32)
    lane = lax.broadcasted_iota(jnp.int32, logits.shape, 1)
    logits = jnp.where(lane < N_EXPERTS, logits, -jnp.inf)
    m1 = jnp.max(logits, -1, keepdims=True)
    i1 = jnp.min(jnp.where(logits == m1, lane, LANES), -1, keepdims=True)
    rest = jnp.where(lane == i1, -jnp.inf, logits)
    m2 = jnp.max(rest, -1, keepdims=True)
    i2 = jnp.min(jnp.where(rest == m2, lane, LANES), -1, keepdims=True)
    e2 = jnp.exp(m2 - m1)
    den = 1.0 + e2
    gates = jnp.where(lane == i1, 1.0 / den, 0.0) + jnp.where(lane == i2, e2 / den, 0.0)
    sel = jnp.where((lane == i1) | (lane == i2), 1.0, 0.0)
    sel_t = sel.T
    ti = lax.broadcasted_iota(jnp.int32, (TOK_TILE, TOK_TILE), 0)
    tj = lax.broadcasted_iota(jnp.int32, (TOK_TILE, TOK_TILE), 1)
    before = jnp.where(tj < ti, 1.0, 0.0).astype(BF16)
    rank = jnp.dot(before, sel.astype(BF16), preferred_element_type=F32) + carry_row[...]
    rank_t = lax.dot_general(sel_t.astype(BF16), before, (((1,), (1,)), ((), ())),
                             preferred_element_type=F32) + carry_col[...]
    rank_ref[...] = jnp.where(sel > 0.0, rank, -1.0)
    rank_t_ref[...] = jnp.where(sel_t > 0.0, rank_t, -1.0)[:N_EXPERTS]
    gate_t_ref[...] = gates.T[:N_EXPERTS]
    cum_ref[pl.ds(c, 1), :] = carry_row[...].astype(jnp.int32)
    carry_row[...] += jnp.sum(sel, 0, keepdims=True)
    carry_col[...] += jnp.sum(sel_t, 1, keepdims=True)

    @pl.when(c == N_TOK_TILES - 1)
    def _():
        cum_ref[pl.ds(N_TOK_TILES, 1), :] = carry_row[...].astype(jnp.int32)


def _route(x, mod, router):
    tile = lambda w: pl.BlockSpec((TOK_TILE, w), lambda c: (c, 0))
    tile_t = pl.BlockSpec((N_EXPERTS, TOK_TILE), lambda c: (0, c))
    return pl.pallas_call(
        _route_kernel,
        grid=(N_TOK_TILES,),
        in_specs=[tile(D_MODEL),
                  pl.BlockSpec((MOD_ROWS, D_MODEL), lambda c: (0, 3)),
                  pl.BlockSpec((MOD_ROWS, D_MODEL), lambda c: (0, 4)),
                  pl.BlockSpec((D_MODEL, LANES), lambda c: (0, 0))],
        out_specs=[tile(D_MODEL), tile(LANES), tile_t, tile_t,
                   pl.BlockSpec((CUM_ROWS, LANES), lambda c: (0, 0))],
        out_shape=[jax.ShapeDtypeStruct((T_ALL, D_MODEL), BF16),
                   jax.ShapeDtypeStruct((T_ALL, LANES), F32),
                   jax.ShapeDtypeStruct((N_EXPERTS, T_ALL), F32),
                   jax.ShapeDtypeStruct((N_EXPERTS, T_ALL), F32),
                   jax.ShapeDtypeStruct((CUM_ROWS, LANES), jnp.int32)],
        scratch_shapes=[pltpu.VMEM((1, LANES), F32), pltpu.VMEM((LANES, 1), F32)],
        compiler_params=_cparams("arbitrary"),
        name="route",
    )(x, mod, mod, jnp.pad(router, ((0, 0), (0, LANES - N_EXPERTS))))


def _gather_kernel(te_ref, off_ref, cum_ref, nt_ref, h_ref, rank_t_ref, gate_t_ref, xs_ref, gs_ref, acc_scr, g_scr):
    i = pl.program_id(0)
    e = te_ref[i]
    r0 = i * SORT_TILE - off_ref[e]
    acc_scr[...] = jnp.zeros_like(acc_scr)
    g_scr[...] = jnp.zeros_like(g_scr)
    want = (r0 + lax.broadcasted_iota(jnp.int32, (SORT_TILE, 1), 0)).astype(F32)
    for c in range(N_TOK_TILES):
        lo = cum_ref[c * N_EXPERTS + e]
        hi = cum_ref[(c + 1) * N_EXPERTS + e]

        @pl.when((i < nt_ref[0]) & (lo < r0 + SORT_TILE) & (hi > r0))
        def _():
            cols = slice(c * TOK_TILE, (c + 1) * TOK_TILE)
            pick = rank_t_ref[pl.ds(e, 1), cols] == want
            acc_scr[...] += jnp.dot(jnp.where(pick, 1.0, 0.0).astype(BF16), h_ref[cols, :],
                                    preferred_element_type=F32)
            g_scr[...] += jnp.sum(jnp.where(pick, gate_t_ref[pl.ds(e, 1), cols], 0.0), -1, keepdims=True)

    xs_ref[...] = acc_scr[...].astype(BF16)
    gs_ref[...] = jnp.broadcast_to(g_scr[...], gs_ref.shape)


def _gather(tile_expert, off, cum, n_tiles, h, rank_t, gate_t):
    const = lambda shape: pl.BlockSpec(shape, lambda i, *_: (0, 0))
    return pl.pallas_call(
        _gather_kernel,
        grid_spec=pltpu.PrefetchScalarGridSpec(
            num_scalar_prefetch=4,
            grid=(N_SORT_TILES,),
            in_specs=[const((T_ALL, D_MODEL)), const((N_EXPERTS, T_ALL)), const((N_EXPERTS, T_ALL))],
            out_specs=[pl.BlockSpec((SORT_TILE, D_MODEL), lambda i, *_: (i, 0)),
                       pl.BlockSpec((SORT_TILE, LANES), lambda i, *_: (i, 0))],
            scratch_shapes=[pltpu.VMEM((SORT_TILE, D_MODEL), F32), pltpu.VMEM((SORT_TILE, 1), F32)]),
        out_shape=[jax.ShapeDtypeStruct((N_SORT_TILES * SORT_TILE, D_MODEL), BF16),
                   jax.ShapeDtypeStruct((N_SORT_TILES * SORT_TILE, LANES), F32)],
        compiler_params=_cparams("parallel"),
        name="moe_gather",
    )(tile_expert, off, cum, n_tiles, h, rank_t, gate_t)


def _expert_kernel(te_ref, nt_ref, xs_ref, gs_ref, w1_ref, w3_ref, w2_ref, ys_ref, w1_scr, w3_scr, w2_scr, acc_scr):
    i = pl.program_id(0)

    @pl.when((i == 0) | (te_ref[i] != te_ref[jnp.maximum(i - 1, 0)]))
    def _():
        w1_scr[...] = w1_ref[...].astype(BF16)
        w3_scr[...] = w3_ref[...].astype(BF16)
        w2_scr[...] = w2_ref[...].astype(BF16)

    @pl.when(i < nt_ref[0])
    def _():
        x = xs_ref[...]
        gate = gs_ref[:, 0:1]
        acc_scr[...] = jnp.zeros_like(acc_scr)
        for c0 in range(0, EXPERT_FF, FF_CHUNK):
            cs = slice(c0, min(c0 + FF_CHUNK, EXPERT_FF))
            a = jnp.dot(x, w1_scr[:, cs], preferred_element_type=F32)
            b = jnp.dot(x, w3_scr[:, cs], preferred_element_type=F32)
            act = (_silu(a) * b * gate).astype(BF16)
            acc_scr[...] += jnp.dot(act, w2_scr[cs, :], preferred_element_type=F32)
        ys_ref[...] = acc_scr[...].astype(BF16)

    @pl.when(i >= nt_ref[0])
    def _():
        ys_ref[...] = jnp.zeros_like(ys_ref)


def _experts(tile_expert, n_tiles, xs, gs, w1, w3, w2):
    w_in = pl.BlockSpec((None, D_MODEL, EXPERT_FF), lambda i, te, nt: (te[i], 0, 0))
    w_out = pl.BlockSpec((None, EXPERT_FF, D_MODEL), lambda i, te, nt: (te[i], 0, 0))
    return pl.pallas_call(
        _expert_kernel,
        grid_spec=pltpu.PrefetchScalarGridSpec(
            num_scalar_prefetch=2,
            grid=(N_SORT_TILES,),
            in_specs=[pl.BlockSpec((SORT_TILE, D_MODEL), lambda i, te, nt: (i, 0)),
                      pl.BlockSpec((SORT_TILE, LANES), lambda i, te, nt: (i, 0)),
                      w_in, w_in, w_out],
            out_specs=pl.BlockSpec((SORT_TILE, D_MODEL), lambda i, te, nt: (i, 0)),
            scratch_shapes=[pltpu.VMEM((D_MODEL, EXPERT_FF), BF16), pltpu.VMEM((D_MODEL, EXPERT_FF), BF16),
                            pltpu.VMEM((EXPERT_FF, D_MODEL), BF16), pltpu.VMEM((SORT_TILE, D_MODEL), F32)]),
        out_shape=jax.ShapeDtypeStruct((N_SORT_TILES * SORT_TILE, D_MODEL), BF16),
        compiler_params=_cparams("arbitrary"),
        name="moe_experts",
    )(tile_expert, n_tiles, xs, gs, w1, w3, w2)


def _combine_kernel(off_ref, cum_ref, ys_ref, rank_ref, x_ref, gate_ref, lng_ref, lnb_ref, o_ref, acc_scr):
    c = pl.program_id(0)
    acc_scr[...] = jnp.zeros_like(acc_scr)
    rank = rank_ref[...]
    lane = lax.broadcasted_iota(jnp.int32, rank.shape, 1)
    cols = lax.broadcasted_iota(jnp.int32, (1, SORT_TILE), 1)
    for e in range(N_EXPERTS):
        lo = off_ref[e] + cum_ref[c * N_EXPERTS + e]
        hi = off_ref[e] + cum_ref[(c + 1) * N_EXPERTS + e]
        r = jnp.sum(jnp.where(lane == e, rank, 0.0), -1, keepdims=True)
        pos = jnp.where(r >= 0.0, r + jnp.full((1, 1), off_ref[e], jnp.int32).astype(F32), -1.0)
        first = lo // SORT_TILE
        for k in range(2):
            s = first + k

            @pl.when((hi > lo) & (s * SORT_TILE < hi))
            def _():
                pick = pos == (s * SORT_TILE + cols).astype(F32)
                rows = ys_ref[pl.ds(pl.multiple_of(s * SORT_TILE, SORT_TILE), SORT_TILE), :]
                acc_scr[...] += jnp.dot(jnp.where(pick, 1.0, 0.0).astype(BF16), rows, preferred_element_type=F32)

    g = _group_of_tile(c, TOK_TILE)
    z = ALPHA * x_ref[...] + gate_ref[pl.ds(g, 1), :] * acc_scr[...]
    o_ref[...] = _layer_norm(z, lng_ref[...], lnb_ref[...])


def _combine(off, cum, ys, rank, x, mod, ln_g, ln_b):
    const = lambda shape: pl.BlockSpec(shape, lambda c, *_: (0, 0))
    return pl.pallas_call(
        _combine_kernel,
        grid_spec=pltpu.PrefetchScalarGridSpec(
            num_scalar_prefetch=2,
            grid=(N_TOK_TILES,),
            in_specs=[pl.BlockSpec(ys.shape, lambda c, *_: (0, 0), pipeline_mode=pl.Buffered(1)),
                      pl.BlockSpec((TOK_TILE, LANES), lambda c, *_: (c, 0)),
                      pl.BlockSpec((TOK_TILE, D_MODEL), lambda c, *_: (c, 0)),
                      pl.BlockSpec((MOD_ROWS, D_MODEL), lambda c, *_: (0, 5)),
                      const((1, D_MODEL)), const((1, D_MODEL))],
            out_specs=pl.BlockSpec((TOK_TILE, D_MODEL), lambda c, *_: (c, 0)),
            scratch_shapes=[pltpu.VMEM((TOK_TILE, D_MODEL), F32)]),
        out_shape=jax.ShapeDtypeStruct((T_ALL, D_MODEL), F32),
        compiler_params=_cparams("parallel"),
        name="moe_combine_ln",
    )(off, cum, ys, rank, x, mod, ln_g.reshape(1, -1), ln_b.reshape(1, -1))


def _moe(x, mod, ln_g, ln_b, router, w1, w3, w2):
    h, rank, rank_t, gate_t, cum = _route(x, mod, router)
    counts = cum[N_TOK_TILES, :N_EXPERTS]
    tiles = (counts + SORT_TILE - 1) // SORT_TILE
    ends = jnp.cumsum(tiles)
    off = ((ends - tiles) * SORT_TILE).astype(jnp.int32)
    n_tiles = ends[-1:].astype(jnp.int32)
    tile_ids = jnp.minimum(jnp.arange(N_SORT_TILES, dtype=jnp.int32), n_tiles - 1)
    tile_expert = jnp.sum((tile_ids[:, None] >= ends[None, :]).astype(jnp.int32), -1)
    cum_flat = cum[:N_TOK_TILES + 1, :N_EXPERTS].reshape(-1)
    xs, gs = _gather(tile_expert, off, cum_flat, n_tiles, h, rank_t, gate_t)
    ys = _experts(tile_expert, n_tiles, xs, gs, w1, w3, w2)
    return _combine(off, cum_flat, ys, rank, x, mod, ln_g, ln_b)


FF_CHUNK = 256


def _ffn_kernel(x_ref, sh_ref, sc_ref, gate_ref, lng_ref, lnb_ref, w1_ref, w3_ref, w2_ref, o_ref, h_scr, acc_scr,
                *, tm, ff):
    i, j = pl.program_id(0), pl.program_id(1)
    g = _group_of_tile(i, tm)

    @pl.when(j == 0)
    def _():
        h_scr[...] = (x_ref[...] * (1.0 + sc_ref[pl.ds(g, 1), :]) + sh_ref[pl.ds(g, 1), :]).astype(BF16)
        acc_scr[...] = jnp.zeros_like(acc_scr)

    h = h_scr[...]
    for c0 in range(0, ff, FF_CHUNK):
        cs = slice(c0, min(c0 + FF_CHUNK, ff))
        a = jnp.dot(h, w1_ref[:, cs].astype(BF16), preferred_element_type=F32)
        b = jnp.dot(h, w3_ref[:, cs].astype(BF16), preferred_element_type=F32)
        act = _silu(a) * b
        acc_scr[...] += jnp.dot(act.astype(BF16), w2_ref[cs, :].astype(BF16), preferred_element_type=F32)

    @pl.when(j == pl.num_programs(1) - 1)
    def _():
        z = ALPHA * x_ref[...] + gate_ref[pl.ds(g, 1), :] * acc_scr[...]
        o_ref[...] = _layer_norm(z, lng_ref[...], lnb_ref[...])


def _ffn(x, mod, ln_g, ln_b, w1, w3, w2):
    tm = 512
    ff = D_FF // 2
    n_j = D_FF // ff
    w_in_spec = pl.BlockSpec((D_MODEL, ff), lambda i, j: (0, j))
    w_out_spec = pl.BlockSpec((ff, D_MODEL), lambda i, j: (j, 0))
    mod_spec = lambda col: pl.BlockSpec((MOD_ROWS, D_MODEL), lambda i, j: (0, col))
    const = lambda shape: pl.BlockSpec(shape, lambda i, j: (0, 0))
    in_specs = [pl.BlockSpec((tm, D_MODEL), lambda i, j: (i, 0)),
                mod_spec(3), mod_spec(4), mod_spec(5),
                const((1, D_MODEL)), const((1, D_MODEL)),
                w_in_spec, w_in_spec, w_out_spec]
    args = [x, mod, mod, mod, ln_g.reshape(1, -1), ln_b.reshape(1, -1), w1, w3, w2]
    return pl.pallas_call(
        functools.partial(_ffn_kernel, tm=tm, ff=ff),
        grid=(T_ALL // tm, n_j),
        in_specs=in_specs,
        out_specs=pl.BlockSpec((tm, D_MODEL), lambda i, j: (i, 0)),
        out_shape=jax.ShapeDtypeStruct((T_ALL, D_MODEL), F32),
        scratch_shapes=[pltpu.VMEM((tm, D_MODEL), BF16), pltpu.VMEM((tm, D_MODEL), F32)],
        compiler_params=_cparams("parallel", "arbitrary"),
        name="ffn",
    )(*args)


def kernel(x_prompt, x_sample, c, cache_k_b, cache_v_b, cache_k_c, cache_v_c, state_ret, c_ctx, ada_w, ada_b, ln_g, ln_b, w_in_ab, hy_conv_w, hy_conv_b, hf_w1, hf_b1, hf_freq, hf_w2, hf_b2, hf_w3, hy_skip, sink_b, w_out_ab, ffn_w1, ffn_w3, ffn_w2, w_in_cd, qn_g, kn_g, ret_decay, w_out_cd, moe_router, moe_w1, moe_w3, moe_w2):
    x = jnp.concatenate([x_prompt.reshape(T_PROMPT, D_MODEL), x_sample.reshape(T_SAMPLE, D_MODEL)], 0)
    cvec = jnp.concatenate([c_ctx[None], c, jnp.zeros((MOD_ROWS - 1 - DEC_BATCH, D_MODEL), F32)], 0)
    mod = _modulation(cvec, ada_w, ada_b)
    kw = N_KV * HEAD_DIM
    prompt = dict(row0=0, n_seq=BATCH, l=SEQ)
    sample = dict(row0=T_PROMPT, n_seq=DEC_BATCH, l=DEC_SEQ)

    u = _inproj(x, mod[0], w_in_ab[0], tn=768)
    filt_args = (hf_w1[0], hf_b1[0], hf_freq[0], hf_w2[0], hf_b2[0], hf_w3[0])
    hy_args = (hy_conv_w[0], hy_conv_b[0], hy_skip[0])
    ya = jnp.concatenate([
        _hyena(u, 0, BATCH, SEQ, *hy_args, _hyena_filter(SEQ, *filt_args)),
        _hyena(u, T_PROMPT // DEC_SEQ, DEC_BATCH, DEC_SEQ, *hy_args, _hyena_filter(DEC_SEQ, *filt_args))], 0)
    cols_b = dict(q_col=AB_Q, k_col=AB_K, v_col=AB_V)
    ctx_b = (cache_k_b[:, 0].reshape(DEC_BATCH, PAST_LEN, kw), cache_v_b[:, 0].reshape(DEC_BATCH, PAST_LEN, kw))
    yb_p, = _attention(u, **prompt, **cols_b, sink=sink_b[0])
    yb_s, = _attention(u, **sample, **cols_b, sink=sink_b[0], ctx=ctx_b, rope=True, band=True)
    yb = jnp.concatenate([yb_p, yb_s], 0)
    k_b = u[:T_PROMPT, AB_K:AB_V].reshape(BATCH, 1, SEQ, N_KV, HEAD_DIM)
    v_b = u[:T_PROMPT, AB_V:IN_AB].reshape(BATCH, 1, SEQ, N_KV, HEAD_DIM)
    x = _outproj(ya, yb, w_out_ab[0], x, mod[0], ln_g[0, 0], ln_b[0, 0])
    x = _ffn(x, mod[0], ln_g[0, 1], ln_b[0, 1], ffn_w1[0], ffn_w3[0], ffn_w2[0])

    u = _inproj(x, mod[1], w_in_cd[0], tn=1664)
    cols_c = dict(q_col=CD_Q, k_col=CD_K, v_col=CD_V)
    gains = (qn_g[0], kn_g[0])
    ctx_c = (cache_k_c[:, 0].reshape(DEC_BATCH, PAST_LEN, kw), cache_v_c[:, 0].reshape(DEC_BATCH, PAST_LEN, kw))
    yc_p, k_c = _attention(u, **prompt, **cols_c, qk_gain=gains, emit_k=True)
    yc_s, = _attention(u, **sample, **cols_c, qk_gain=gains, ctx=ctx_c, rope=True)
    yd_p, s_r = _retention(u, ret_decay[0], **prompt, emit_state=True)
    yd_s, = _retention(u, ret_decay[0], **sample, rope=True, s0=state_ret[:, 0])
    k_c = k_c.reshape(BATCH, 1, SEQ, N_KV, HEAD_DIM)
    v_c = u[:T_PROMPT, CD_V:CD_RQ].reshape(BATCH, 1, SEQ, N_KV, HEAD_DIM)
    x = _outproj(jnp.concatenate([yc_p, yc_s], 0), jnp.concatenate([yd_p, yd_s], 0), w_out_cd[0], x, mod[1],
                 ln_g[1, 0], ln_b[1, 0])
    x = _moe(x, mod[1], ln_g[1, 1], ln_b[1, 1], moe_router[0], moe_w1[0], moe_w3[0], moe_w2[0])

    y_prompt = x[:T_PROMPT].reshape(BATCH, SEQ, D_MODEL)
    y_sample = x[T_PROMPT:].reshape(DEC_BATCH, DEC_SEQ, D_MODEL)
    return y_prompt, y_sample, k_b, v_b, k_c, v_c, s_r[:, None]
```

```python
import functools
import math

import numpy as np
import jax
import jax.numpy as jnp
from jax import lax
from jax.experimental import pallas as pl
from jax.experimental.pallas import tpu as pltpu

F32 = jnp.float32
BF16 = jnp.bfloat16

D_MODEL = 1024
BATCH = 16
SEQ = 256
DEC_BATCH = 2
DEC_SEQ = 1024
PAST_LEN = 512
GRID_W = 64
HEAD_DIM = 64
BLOCK = 128
HY_W = 512
POS_BANDS = 16
POS_EMB = 1 + 2 * POS_BANDS
FILT_HID = 64
HY_FAST_DECAY = 0.3
HY_SLOW_DECAY = 1.5
HY_TARGET = 1e-2
N_HEADS = 8
N_KV = 2
GROUPS = N_HEADS // N_KV
RET_HEADS = 4
RET_D = 128
CHUNK = 128
ROPE_BASE = 10000.0
D_FF = 2816
N_EXPERTS = 8
EXPERT_FF = 1408
DEPTH = 2
ALPHA = (2 * DEPTH) ** 0.25
EPS = 1e-6

T_PROMPT = BATCH * SEQ
T_SAMPLE = DEC_BATCH * DEC_SEQ
T_ALL = T_PROMPT + T_SAMPLE
GROUP_ROWS = 1024
N_PROMPT_GROUPS = T_PROMPT // GROUP_ROWS
MOD_ROWS = 16
LANES = 128
VMEM_LIMIT = 58 * 1024 * 1024

AB_Q = 3 * HY_W
AB_K = AB_Q + N_HEADS * HEAD_DIM
AB_V = AB_K + N_KV * HEAD_DIM
IN_AB = AB_V + N_KV * HEAD_DIM
CD_Q = 0
CD_K = N_HEADS * HEAD_DIM
CD_V = CD_K + N_KV * HEAD_DIM
CD_RQ = CD_V + N_KV * HEAD_DIM
CD_RK = CD_RQ + RET_HEADS * RET_D
CD_RV = CD_RK + RET_HEADS * RET_D
CD_GF = CD_RV + RET_HEADS * RET_D
CD_GB = CD_GF + RET_HEADS * RET_D
IN_CD = CD_GB + RET_HEADS * RET_D


def _cparams(*sem):
    return pltpu.CompilerParams(dimension_semantics=sem, vmem_limit_bytes=VMEM_LIMIT)


def _silu(x):
    return x * jax.nn.sigmoid(x)


def _bdot(a, b):
    return jnp.dot(a.astype(BF16), b.astype(BF16), preferred_element_type=F32)


def _bdot_nt(a, b):
    return lax.dot_general(a.astype(BF16), b.astype(BF16), (((1,), (1,)), ((), ())),
                           preferred_element_type=F32)


def _bdot_tn(a, b):
    return lax.dot_general(a.astype(BF16), b.astype(BF16), (((0,), (0,)), ((), ())),
                           preferred_element_type=F32)


def _layer_norm(z, g, b):
    mu = jnp.mean(z, -1, keepdims=True)
    zc = z - mu
    var = jnp.mean(zc * zc, -1, keepdims=True)
    return zc * lax.rsqrt(var + EPS) * g + b


def _fill_rows(kern, in_specs, args, into):
    if into is None:
        return kern, {}
    n_in = len(args)
    in_specs.append(pl.BlockSpec(memory_space=pl.ANY))
    args.append(into)

    def kern_into(*refs):
        return kern(*refs[:n_in], *refs[n_in + 1:])

    return kern_into, {n_in: 0}


def _group_of_tile(i, tm):
    return jnp.maximum(i // (GROUP_ROWS // tm) - (N_PROMPT_GROUPS - 1), 0)


def _rope_tables(n_tokens, d, reps):
    nf = d // 4
    inv = ROPE_BASE ** (-np.arange(nf, dtype=np.float64) / nf)
    pos = np.arange(n_tokens)
    row, col = pos // GRID_W, pos % GRID_W
    ang_r = row[:, None] * inv[None, :]
    ang_c = col[:, None] * inv[None, :]
    zeros = np.zeros_like(ang_r)
    cos = np.concatenate([np.cos(ang_r), np.cos(ang_r), np.cos(ang_c), np.cos(ang_c)], -1)
    sin_a = np.concatenate([-np.sin(ang_r), zeros, -np.sin(ang_c), zeros], -1)
    sin_b = np.concatenate([zeros, np.sin(ang_r), zeros, np.sin(ang_c)], -1)
    tile = lambda a: jnp.asarray(np.tile(a, (1, reps)), F32)
    return tile(cos), tile(sin_a), tile(sin_b)


def _dft_mats(l):
    n = 2 * l
    k = np.arange(l, dtype=np.float64)
    ang = 2.0 * np.pi * np.outer(k, k) / n
    fc = np.cos(ang)
    fs = np.sin(ang)
    fs[0, :] = np.cos(np.pi * k)
    fwd = np.concatenate([fc, fs], 0)
    wk = np.full((l,), 2.0)
    wk[0] = 1.0
    inv = np.concatenate([fc.T * wk[None, :], fs.T * wk[None, :]], 1)
    return jnp.asarray(fwd, F32), jnp.asarray(inv, F32)


def _filter_consts(l):
    t = np.linspace(0.0, 1.0, l, dtype=np.float32).astype(np.float64)[:, None]
    w = (2.0 * math.pi * np.arange(l, dtype=np.float64) / l)[:, None]
    bands = np.linspace(1e-4, POS_BANDS - 1.0, POS_BANDS, dtype=np.float32).astype(np.float64)[None, :]
    z = np.concatenate([t, np.cos(bands * w), -np.sin(bands * w)], -1)
    z = np.pad(z, ((0, 0), (0, LANES - POS_EMB)))
    max_decay = math.log(HY_TARGET) / HY_FAST_DECAY
    min_decay = math.log(HY_TARGET) / HY_SLOW_DECAY
    deltas = np.linspace(min_decay, max_decay, HY_W, dtype=np.float32).astype(np.float64)
    window = np.exp(-t * np.abs(deltas)[None, :])
    return jnp.asarray(z, F32), jnp.asarray(window, F32)


def _mod_kernel(c_ref, w_ref, b_ref, o_ref):
    o_ref[...] = _bdot(_silu(c_ref[...]), w_ref[...]) + b_ref[...]


def _modulation(cvec, ada_w, ada_b):
    tn = 1536
    n = ada_w.shape[-1]
    return pl.pallas_call(
        _mod_kernel,
        grid=(DEPTH, n // tn),
        in_specs=[pl.BlockSpec((MOD_ROWS, D_MODEL), lambda l, j: (0, 0)),
                  pl.BlockSpec((None, D_MODEL, tn), lambda l, j: (l, 0, j)),
                  pl.BlockSpec((None, 1, tn), lambda l, j: (l, 0, j))],
        out_specs=pl.BlockSpec((None, MOD_ROWS, tn), lambda l, j: (l, 0, j)),
        out_shape=jax.ShapeDtypeStruct((DEPTH, MOD_ROWS, n), F32),
        compiler_params=_cparams("parallel", "parallel"),
        name="mod",
    )(cvec, ada_w, ada_b.reshape(DEPTH, 1, n))


def _inproj_kernel(x_ref, sh_ref, sc_ref, w_ref, o_ref, h_scr, *, tm):
    i, j = pl.program_id(0), pl.program_id(1)

    @pl.when(j == 0)
    def _():
        g = _group_of_tile(i, tm)
        sc = sc_ref[pl.ds(g, 1), :]
        sh = sh_ref[pl.ds(g, 1), :]
        h_scr[...] = (x_ref[...] * (1.0 + sc) + sh).astype(BF16)

    o_ref[...] = jnp.dot(h_scr[...], w_ref[...].astype(BF16), preferred_element_type=F32)


def _inproj(x, mod, w, tn):
    tm = 1024
    n = w.shape[1]
    return pl.pallas_call(
        functools.partial(_inproj_kernel, tm=tm),
        grid=(T_ALL // tm, n // tn),
        in_specs=[pl.BlockSpec((tm, D_MODEL), lambda i, j: (i, 0)),
                  pl.BlockSpec((MOD_ROWS, D_MODEL), lambda i, j: (0, 0)),
                  pl.BlockSpec((MOD_ROWS, D_MODEL), lambda i, j: (0, 1)),
                  pl.BlockSpec((D_MODEL, tn), lambda i, j: (0, j))],
        out_specs=pl.BlockSpec((tm, tn), lambda i, j: (i, j)),
        out_shape=jax.ShapeDtypeStruct((T_ALL, n), F32),
        scratch_shapes=[pltpu.VMEM((tm, D_MODEL), BF16)],
        compiler_params=_cparams("parallel", "arbitrary"),
        name="inproj",
    )(x, mod, mod, w)


def _hyena_filter_kernel(z_ref, w1_ref, b1_ref, fr_ref, w2_ref, b2_ref, w3_ref, win_ref, fwd32_ref, inv32_ref,
                         hc_ref, hs_ref, hc2_ref, fwd_ref, inv_ref, *, l):
    hi = lax.Precision.HIGHEST
    fwd = fwd32_ref[...].astype(BF16)
    fwd_ref[...] = fwd
    inv_ref[...] = inv32_ref[...].astype(BF16)
    fr = fr_ref[...]
    h = jnp.sin(fr * (jnp.dot(z_ref[...], w1_ref[...], precision=hi, preferred_element_type=F32) + b1_ref[...]))
    h = jnp.sin(fr * (jnp.dot(h, w2_ref[...], precision=hi, preferred_element_type=F32) + b2_ref[...]))
    h = jnp.dot(h, w3_ref[...], precision=hi, preferred_element_type=F32)
    win = win_ref[...]
    hf = h[:, :HY_W] * win
    hb = h[:, HY_W:] * win
    p = _bdot(fwd, hf + hb)
    q = _bdot(fwd, hf - hb)
    row0 = lax.broadcasted_iota(jnp.int32, (l, 1), 0) == 0
    hc = p[:l]
    hc_ref[...] = hc
    hs_ref[...] = jnp.where(row0, 0.0, q[l:])
    hc2_ref[...] = jnp.where(row0, p[l:l + 1], hc)


def _hyena_filter(l, fw1, fb1, ffreq, fw2, fb2, fw3):
    z, window = _filter_consts(l)
    fwd, inv = _dft_mats(l)
    pad_c = LANES - FILT_HID
    w1 = jnp.pad(fw1, ((0, LANES - POS_EMB), (0, pad_c)))
    w2 = jnp.pad(fw2, ((0, pad_c), (0, pad_c)))
    w3 = jnp.pad(fw3, ((0, pad_c), (0, 0)))
    row = lambda a: jnp.pad(a, (0, pad_c)).reshape(1, LANES)
    shp = jax.ShapeDtypeStruct((l, HY_W), F32)
    return pl.pallas_call(
        functools.partial(_hyena_filter_kernel, l=l),
        out_shape=(shp, shp, shp, jax.ShapeDtypeStruct(fwd.shape, BF16), jax.ShapeDtypeStruct(inv.shape, BF16)),
        compiler_params=pltpu.CompilerParams(vmem_limit_bytes=VMEM_LIMIT),
        name=f"hyena_filter_{l}",
    )(z, w1, row(fb1), row(ffreq), w2, row(fb2), w3, window, fwd, inv)


def _hyena_kernel(u_ref, cw_ref, cb_ref, skip_ref, fwd_ref, inv_ref, hc_ref, hs_ref, hc2_ref, o_ref, *, l):
    u = u_ref[...]
    rows = lax.broadcasted_iota(jnp.int32, (l, 1), 0)
    prev = jnp.where(rows == 0, 0.0, pltpu.roll(u, 1, 0))
    nxt = jnp.where(rows == l - 1, 0.0, pltpu.roll(u, l - 1, 0))
    uc = prev * cw_ref[0:1, :] + u * cw_ref[1:2, :] + nxt * cw_ref[2:3, :] + cb_ref[...]
    x0 = uc[:, :HY_W]
    x1 = uc[:, HY_W:2 * HY_W]
    v = uc[:, 2 * HY_W:] * x1
    ab = _bdot(fwd_ref[...], v)
    a, b = ab[:l], ab[l:]
    hs = hs_ref[...]
    re = a * hc_ref[...] - b * hs
    im = a * hs + b * hc2_ref[...]
    y = _bdot(inv_ref[...], jnp.concatenate([re, im], 0)) * (1.0 / (2 * l))
    o_ref[...] = ((y + skip_ref[...] * v) * x0).astype(o_ref.dtype)


def _hyena(u_all, row_block0, n_seq, l, conv_w, conv_b, skip, filt, into=None):
    hc, hs, hc2, fwd, inv = filt
    const = lambda shape: pl.BlockSpec(shape, lambda s: (0, 0))
    in_specs = [pl.BlockSpec((l, 3 * HY_W), lambda s: (row_block0 + s, 0)),
                const((3, 3 * HY_W)), const((1, 3 * HY_W)), const((1, HY_W)),
                const((2 * l, l)), const((l, 2 * l)),
                const((l, HY_W)), const((l, HY_W)), const((l, HY_W))]
    args = [u_all, conv_w, conv_b.reshape(1, -1), skip.reshape(1, -1), fwd, inv, hc, hs, hc2]
    kern, alias = _fill_rows(functools.partial(_hyena_kernel, l=l), in_specs, args, into)
    return pl.pallas_call(
        kern,
        grid=(n_seq,),
        in_specs=in_specs,
        out_specs=pl.BlockSpec((l, HY_W), lambda s: (row_block0 + s, 0)),
        out_shape=jax.ShapeDtypeStruct((T_ALL, HY_W), BF16),
        input_output_aliases=alias,
        compiler_params=_cparams("parallel"),
        name=f"hyena_{l}",
    )(*args)


def _seg_rms_norm(x, bd_ref, g):
    sq = x * x
    hi = sq.astype(BF16)
    lo = (sq - hi.astype(F32)).astype(BF16)
    bd = bd_ref[...]
    ss = (jnp.dot(hi, bd, preferred_element_type=F32) + jnp.dot(lo, bd, preferred_element_type=F32))
    return x * lax.rsqrt(ss * (1.0 / HEAD_DIM) + EPS) * g


def _rope(x, cos, sin_a, sin_b, quarter):
    w = x.shape[-1]
    return x * cos + pltpu.roll(x, w - quarter, 1) * sin_a + pltpu.roll(x, quarter, 1) * sin_b


def _attn_kernel(*refs, l, lc, rope, qknorm, band, has_sink, emit_k):
    it = iter(refs)
    q_ref, k_ref, v_ref = next(it), next(it), next(it)
    if lc:
        ck_ref, cv_ref = next(it), next(it)
    if rope:
        cq_ref, saq_ref, sbq_ref = next(it), next(it), next(it)
        ckk_ref, sak_ref, sbk_ref = next(it), next(it), next(it)
    if qknorm:
        qg_ref, kg_ref, bdq_ref, bdk_ref = next(it), next(it), next(it), next(it)
    if has_sink:
        sink_ref = next(it)
    o_ref = next(it)
    if emit_k:
        kout_ref = next(it)
    kp_scr = next(it)

    qi = pl.program_id(1)

    @pl.when(qi == 0)
    def _():
        k = k_ref[...]
        if qknorm:
            k = _seg_rms_norm(k, bdk_ref, kg_ref[...])
        if emit_k:
            kout_ref[...] = k
        if rope:
            k = _rope(k, ckk_ref[...], sak_ref[...], sbk_ref[...], HEAD_DIM // 4)
        for kv in range(N_KV):
            kp_scr[kv] = k[:, kv * HEAD_DIM:(kv + 1) * HEAD_DIM].astype(BF16)

    q = q_ref[...]
    if qknorm:
        q = _seg_rms_norm(q, bdq_ref, qg_ref[...])
    if rope:
        q = _rope(q, cq_ref[...], saq_ref[...], sbq_ref[...], HEAD_DIM // 4)
    q = q * HEAD_DIM ** -0.5

    rows = GROUPS * BLOCK
    if band:
        tq = qi * BLOCK + lax.broadcasted_iota(jnp.int32, (rows, 1), 0) % BLOCK
        tk = lax.broadcasted_iota(jnp.int32, (1, l), 1)
        valid = jnp.abs(tq - tk) <= BLOCK

    outs = []
    for kv in range(N_KV):
        lanes = slice(kv * HEAD_DIM, (kv + 1) * HEAD_DIM)
        qs = jnp.concatenate([q[:, (kv * GROUPS + g) * HEAD_DIM:(kv * GROUPS + g + 1) * HEAD_DIM]
                              for g in range(GROUPS)], 0)
        s = _bdot_nt(qs, kp_scr[kv])
        if band:
            s = jnp.where(valid, s, -jnp.inf)
        m = jnp.max(s, -1, keepdims=True)
        if lc:
            s_c = _bdot_nt(qs, ck_ref[:, lanes])
            m = jnp.maximum(m, jnp.max(s_c, -1, keepdims=True))
        if has_sink:
            sk = jnp.concatenate([jnp.full((BLOCK, 1), sink_ref[kv * GROUPS + g], F32)
                                  for g in range(GROUPS)], 0)
            m = jnp.maximum(m, sk)
        e = jnp.exp(s - m)
        den = jnp.sum(e, -1, keepdims=True)
        o = _bdot(e, v_ref[:, lanes])
        if lc:
            e_c = jnp.exp(s_c - m)
            den = den + jnp.sum(e_c, -1, keepdims=True)
            o = o + _bdot(e_c, cv_ref[:, lanes])
        if has_sink:
            den = den + jnp.exp(sk - m)
        o = o / den
        outs.extend(o[g * BLOCK:(g + 1) * BLOCK] for g in range(GROUPS))
    o_ref[...] = jnp.concatenate(outs, 1).astype(o_ref.dtype)


def _attention(u, *, row0, n_seq, l, q_col, k_col, v_col, ctx=None, rope=False, qk_gain=None,
               band=False, sink=None, emit_k=False, into=None):
    nq = l // BLOCK
    qw = N_HEADS * HEAD_DIM
    kw = N_KV * HEAD_DIM
    qb0, sb0 = row0 // BLOCK, row0 // l
    seq_spec = lambda col: pl.BlockSpec((l, kw), lambda b, i: (sb0 + b, col // kw))
    const = lambda shape: pl.BlockSpec(shape, lambda b, i: (0, 0))
    in_specs = [pl.BlockSpec((BLOCK, qw), lambda b, i: (qb0 + b * nq + i, q_col // qw)),
                seq_spec(k_col), seq_spec(v_col)]
    args = [u, u, u]
    lc = 0
    if ctx is not None:
        lc = ctx[0].shape[1]
        in_specs += [pl.BlockSpec((None, lc, kw), lambda b, i: (b, 0, 0))] * 2
        args += list(ctx)
    if rope:
        tabs = _rope_tables(l, HEAD_DIM, N_HEADS)
        in_specs += [pl.BlockSpec((BLOCK, qw), lambda b, i: (i, 0))] * 3 + [const((l, kw))] * 3
        args += list(tabs) + list(tabs)
    if qk_gain is not None:
        bd = np.kron(np.eye(N_HEADS), np.ones((HEAD_DIM, HEAD_DIM)))
        in_specs += [const((1, qw)), const((1, kw)), const((qw, qw)), const((kw, kw))]
        args += [jnp.tile(qk_gain[0], N_HEADS).reshape(1, qw), jnp.tile(qk_gain[1], N_KV).reshape(1, kw),
                 jnp.asarray(bd, BF16), jnp.asarray(bd[:kw, :kw], BF16)]
    if sink is not None:
        in_specs.append(pl.BlockSpec(memory_space=pltpu.SMEM))
        args.append(sink)
    out_specs = [pl.BlockSpec((BLOCK, qw), lambda b, i: (qb0 + b * nq + i, 0))]
    out_shape = [jax.ShapeDtypeStruct((T_ALL, qw), BF16)]
    if emit_k:
        out_specs.append(pl.BlockSpec((l, kw), lambda b, i: (b, 0)))
        out_shape.append(jax.ShapeDtypeStruct((n_seq * l, kw), F32))
    kern = functools.partial(_attn_kernel, l=l, lc=lc, rope=rope, qknorm=qk_gain is not None, band=band,
                             has_sink=sink is not None, emit_k=emit_k)
    kern, alias = _fill_rows(kern, in_specs, args, into)
    return pl.pallas_call(
        kern,
        grid=(n_seq, nq),
        in_specs=in_specs,
        out_specs=out_specs,
        out_shape=out_shape,
        input_output_aliases=alias,
        scratch_shapes=[pltpu.VMEM((N_KV, l, HEAD_DIM), BF16)],
        compiler_params=_cparams("parallel", "arbitrary"),
        name=f"attn_{l}_{'b' if sink is not None else 'c'}",
    )(*args)


def _ret_kernel(*refs, l, rope, has_s0, emit_state):
    it = iter(refs)
    dec_ref = next(it)
    rq_ref, rk_ref, rv_ref, gf_ref, gb_ref = next(it), next(it), next(it), next(it), next(it)
    if rope:
        cos_ref, sa_ref, sb_ref = next(it), next(it), next(it)
    if has_s0:
        s0_ref = next(it)
    o_ref = next(it)
    if emit_state:
        st_ref = next(it)

    h = pl.program_id(1)
    q = rq_ref[...] * RET_D ** -0.5
    k = rk_ref[...]
    v = rv_ref[...]
    if rope:
        q = _rope(q, cos_ref[...], sa_ref[...], sb_ref[...], RET_D // 4)
        k = _rope(k, cos_ref[...], sa_ref[...], sb_ref[...], RET_D // 4)
    n = l // CHUNK
    ii = lax.broadcasted_iota(jnp.int32, (CHUNK, 1), 0).astype(F32)
    jj = lax.broadcasted_iota(jnp.int32, (1, CHUNK), 1).astype(F32)
    diff = ii - jj
    y = None
    for d in range(2):
        log_g = jnp.log(jax.nn.sigmoid(jnp.full((1, 1), dec_ref[d, h], F32)))
        if d == 0:
            mask = jnp.exp(jnp.where(diff >= 0, diff * log_g, -jnp.inf))
            q_dec = jnp.exp((ii + 1.0) * log_g)
            k_dec = jnp.exp((CHUNK - 1.0 - ii) * log_g)
            order = range(n)
        else:
            mask = jnp.exp(jnp.where(diff <= 0, -diff * log_g, -jnp.inf))
            q_dec = jnp.exp((CHUNK - ii) * log_g)
            k_dec = jnp.exp(ii * log_g)
            order = reversed(range(n))
        c_dec = jnp.exp(CHUNK * log_g)
        state = s0_ref[d] if has_s0 else jnp.zeros((RET_D, RET_D), F32)
        o_chunks = [None] * n
        for c in order:
            sl = slice(c * CHUNK, (c + 1) * CHUNK)
            qc, kc, vc = q[sl], k[sl], v[sl]
            inner = _bdot_nt(qc, kc) * mask
            o_chunks[c] = _bdot(inner, vc) + _bdot(qc * q_dec, state)
            state = state * c_dec + _bdot_tn(kc * k_dec, vc)
        if emit_state:
            st_ref[d] = state
        o = jnp.concatenate(o_chunks, 0)
        o = o * lax.rsqrt(jnp.mean(o * o, -1, keepdims=True) + EPS)
        gate = _silu((gf_ref if d == 0 else gb_ref)[...])
        y = gate * o if y is None else y + gate * o
    o_ref[...] = y.astype(o_ref.dtype)


def _retention(u, ret_decay, *, row0, n_seq, l, rope=False, s0=None, emit_state=False, into=None):
    sb0 = row0 // l
    col = lambda c0: pl.BlockSpec((l, RET_D), lambda b, h: (sb0 + b, c0 // RET_D + h))
    in_specs = [pl.BlockSpec(memory_space=pltpu.SMEM),
                col(CD_RQ), col(CD_RK), col(CD_RV), col(CD_GF), col(CD_GB)]
    args = [ret_decay, u, u, u, u, u]
    if rope:
        in_specs += [pl.BlockSpec((l, RET_D), lambda b, h: (0, 0))] * 3
        args += list(_rope_tables(l, RET_D, 1))
    state_spec = pl.BlockSpec((None, 2, None, RET_D, RET_D), lambda b, h: (b, 0, h, 0, 0))
    if s0 is not None:
        in_specs.append(state_spec)
        args.append(s0)
    out_specs = [pl.BlockSpec((l, RET_D), lambda b, h: (sb0 + b, h))]
    out_shape = [jax.ShapeDtypeStruct((T_ALL, RET_HEADS * RET_D), BF16)]
    if emit_state:
        out_specs.append(state_spec)
        out_shape.append(jax.ShapeDtypeStruct((n_seq, 2, RET_HEADS, RET_D, RET_D), F32))
    kern = functools.partial(_ret_kernel, l=l, rope=rope, has_s0=s0 is not None, emit_state=emit_state)
    kern, alias = _fill_rows(kern, in_specs, args, into)
    return pl.pallas_call(
        kern,
        grid=(n_seq, RET_HEADS),
        in_specs=in_specs,
        out_specs=out_specs,
        out_shape=out_shape,
        input_output_aliases=alias,
        compiler_params=_cparams("parallel", "parallel"),
        name=f"retention_{l}",
    )(*args)


def _outproj_kernel(ya_ref, yb_ref, w_ref, x_ref, gate_ref, lng_ref, lnb_ref, o_ref, w_scr, *, tm):
    i = pl.program_id(0)

    @pl.when(i == 0)
    def _():
        w_scr[...] = w_ref[...].astype(BF16)

    half = ya_ref.shape[1]
    m = (jnp.dot(ya_ref[...], w_scr[:half], preferred_element_type=F32)
         + jnp.dot(yb_ref[...], w_scr[half:], preferred_element_type=F32))
    gate = gate_ref[pl.ds(_group_of_tile(i, tm), 1), :]
    o_ref[...] = _layer_norm(ALPHA * x_ref[...] + gate * m, lng_ref[...], lnb_ref[...])


def _outproj(ya, yb, w, x, mod, ln_g, ln_b):
    tm = 512
    half = ya.shape[1]
    const = lambda shape: pl.BlockSpec(shape, lambda i: (0, 0))
    return pl.pallas_call(
        functools.partial(_outproj_kernel, tm=tm),
        grid=(T_ALL // tm,),
        in_specs=[pl.BlockSpec((tm, half), lambda i: (i, 0)),
                  pl.BlockSpec((tm, half), lambda i: (i, 0)),
                  const((2 * half, D_MODEL)),
                  pl.BlockSpec((tm, D_MODEL), lambda i: (i, 0)),
                  pl.BlockSpec((MOD_ROWS, D_MODEL), lambda i: (0, 2)),
                  const((1, D_MODEL)), const((1, D_MODEL))],
        out_specs=pl.BlockSpec((tm, D_MODEL), lambda i: (i, 0)),
        out_shape=jax.ShapeDtypeStruct((T_ALL, D_MODEL), F32),
        scratch_shapes=[pltpu.VMEM((2 * half, D_MODEL), BF16)],
        compiler_params=_cparams("arbitrary"),
        name="outproj_ln",
    )(ya, yb, w, x, mod, ln_g.reshape(1, -1), ln_b.reshape(1, -1))


TOK_TILE = 256
N_TOK_TILES = T_ALL // TOK_TILE
SORT_TILE = 256
EXP_TILE = 512
N_EXP_TILES = (2 * T_ALL) // EXP_TILE + N_EXPERTS
N_SORT_TILES = N_EXP_TILES * (EXP_TILE // SORT_TILE)
N_PROMPT_TOK_TILES = T_PROMPT // TOK_TILE
CUM_ROWS = 32


def _route_kernel(x_ref, sh_ref, sc_ref, r_ref, h_ref, rank_ref, rank_t_ref, gate_t_ref, cum_ref,
                  carry_row, carry_col):
    c = pl.program_id(0)

    @pl.when(c == 0)
    def _():
        carry_row[...] = jnp.zeros_like(carry_row)
        carry_col[...] = jnp.zeros_like(carry_col)
        cum_ref[...] = jnp.zeros_like(cum_ref)

    g = _group_of_tile(c, TOK_TILE)
    h = x_ref[...] * (1.0 + sc_ref[pl.ds(g, 1), :]) + sh_ref[pl.ds(g, 1), :]
    h_ref[...] = h.astype(BF16)
    logits = jnp.dot(h, r_ref[...], precision=lax.Precision.HIGHEST, preferred_element_type=F32)
    lane = lax.broadcasted_iota(jnp.int32, logits.shape, 1)
    logits = jnp.where(lane < N_EXPERTS, logits, -jnp.inf)
    m1 = jnp.max(logits, -1, keepdims=True)
    i1 = jnp.min(jnp.where(logits == m1, lane, LANES), -1, keepdims=True)
    rest = jnp.where(lane == i1, -jnp.inf, logits)
    m2 = jnp.max(rest, -1, keepdims=True)
    i2 = jnp.min(jnp.where(rest == m2, lane, LANES), -1, keepdims=True)
    e2 = jnp.exp(m2 - m1)
    den = 1.0 + e2
    gates = jnp.where(lane == i1, 1.0 / den, 0.0) + jnp.where(lane == i2, e2 / den, 0.0)
    sel = jnp.where((lane == i1) | (lane == i2), 1.0, 0.0)
    sel_t = sel.T
    ti = lax.broadcasted_iota(jnp.int32, (TOK_TILE, TOK_TILE), 0)
    tj = lax.broadcasted_iota(jnp.int32, (TOK_TILE, TOK_TILE), 1)
    before = jnp.where(tj < ti, 1.0, 0.0).astype(BF16)
    rank = jnp.dot(before, sel.astype(BF16), preferred_element_type=F32) + carry_row[...]
    rank_t = lax.dot_general(sel_t.astype(BF16), before, (((1,), (1,)), ((), ())),
                             preferred_element_type=F32) + carry_col[...]
    rank_ref[...] = jnp.where(sel > 0.0, rank, -1.0)
    rank_t_ref[...] = jnp.where(sel_t > 0.0, rank_t, -1.0)[:N_EXPERTS]
    gate_t_ref[...] = gates.T[:N_EXPERTS]
    cum_ref[pl.ds(c, 1), :] = carry_row[...].astype(jnp.int32)
    carry_row[...] += jnp.sum(sel, 0, keepdims=True)
    carry_col[...] += jnp.sum(sel_t, 1, keepdims=True)

    @pl.when(c == N_TOK_TILES - 1)
    def _():
        cum_ref[pl.ds(N_TOK_TILES, 1), :] = carry_row[...].astype(jnp.int32)


def _route(x, mod, router):
    tile = lambda w: pl.BlockSpec((TOK_TILE, w), lambda c: (c, 0))
    tile_t = pl.BlockSpec((N_EXPERTS, TOK_TILE), lambda c: (0, c))
    return pl.pallas_call(
        _route_kernel,
        grid=(N_TOK_TILES,),
        in_specs=[tile(D_MODEL),
                  pl.BlockSpec((MOD_ROWS, D_MODEL), lambda c: (0, 3)),
                  pl.BlockSpec((MOD_ROWS, D_MODEL), lambda c: (0, 4)),
                  pl.BlockSpec((D_MODEL, LANES), lambda c: (0, 0))],
        out_specs=[tile(D_MODEL), tile(LANES), tile_t, tile_t,
                   pl.BlockSpec((CUM_ROWS, LANES), lambda c: (0, 0))],
        out_shape=[jax.ShapeDtypeStruct((T_ALL, D_MODEL), BF16),
                   jax.ShapeDtypeStruct((T_ALL, LANES), F32),
                   jax.ShapeDtypeStruct((N_EXPERTS, T_ALL), F32),
                   jax.ShapeDtypeStruct((N_EXPERTS, T_ALL), F32),
                   jax.ShapeDtypeStruct((CUM_ROWS, LANES), jnp.int32)],
        scratch_shapes=[pltpu.VMEM((1, LANES), F32), pltpu.VMEM((LANES, 1), F32)],
        compiler_params=_cparams("arbitrary"),
        name="route",
    )(x, mod, mod, jnp.pad(router, ((0, 0), (0, LANES - N_EXPERTS))))


def _gather_kernel(te_ref, off_ref, cum_ref, nt_ref, h_ref, rank_t_ref, gate_t_ref, xs_ref, gs_ref, acc_scr, g_scr):
    i = pl.program_id(0)
    e = te_ref[i]
    r0 = i * SORT_TILE - off_ref[e]
    acc_scr[...] = jnp.zeros_like(acc_scr)
    g_scr[...] = jnp.zeros_like(g_scr)
    want = (r0 + lax.broadcasted_iota(jnp.int32, (SORT_TILE, 1), 0)).astype(F32)
    for c in range(N_TOK_TILES):
        lo = cum_ref[c * N_EXPERTS + e]
        hi = cum_ref[(c + 1) * N_EXPERTS + e]

        @pl.when((i < nt_ref[0]) & (lo < r0 + SORT_TILE) & (hi > r0))
        def _():
            cols = slice(c * TOK_TILE, (c + 1) * TOK_TILE)
            pick = rank_t_ref[pl.ds(e, 1), cols] == want
            acc_scr[...] += jnp.dot(jnp.where(pick, 1.0, 0.0).astype(BF16), h_ref[cols, :],
                                    preferred_element_type=F32)
            g_scr[...] += jnp.sum(jnp.where(pick, gate_t_ref[pl.ds(e, 1), cols], 0.0), -1, keepdims=True)

    xs_ref[...] = acc_scr[...].astype(BF16)
    gs_ref[...] = jnp.broadcast_to(g_scr[...], gs_ref.shape)


def _gather(tile_expert, off, cum, n_tiles, h, rank_t, gate_t):
    const = lambda shape: pl.BlockSpec(shape, lambda i, *_: (0, 0))
    return pl.pallas_call(
        _gather_kernel,
        grid_spec=pltpu.PrefetchScalarGridSpec(
            num_scalar_prefetch=4,
            grid=(N_SORT_TILES,),
            in_specs=[const((T_ALL, D_MODEL)), const((N_EXPERTS, T_ALL)), const((N_EXPERTS, T_ALL))],
            out_specs=[pl.BlockSpec((SORT_TILE, D_MODEL), lambda i, *_: (i, 0)),
                       pl.BlockSpec((SORT_TILE, LANES), lambda i, *_: (i, 0))],
            scratch_shapes=[pltpu.VMEM((SORT_TILE, D_MODEL), F32), pltpu.VMEM((SORT_TILE, 1), F32)]),
        out_shape=[jax.ShapeDtypeStruct((N_SORT_TILES * SORT_TILE, D_MODEL), BF16),
                   jax.ShapeDtypeStruct((N_SORT_TILES * SORT_TILE, LANES), F32)],
        compiler_params=_cparams("parallel"),
        name="moe_gather",
    )(tile_expert, off, cum, n_tiles, h, rank_t, gate_t)


def _expert_kernel(te_ref, nt_ref, xs_ref, gs_ref, w1_ref, w3_ref, w2_ref, ys_ref, w1_scr, w3_scr, w2_scr, acc_scr):
    i = pl.program_id(0)

    @pl.when((i == 0) | (te_ref[i] != te_ref[jnp.maximum(i - 1, 0)]))
    def _():
        w1_scr[...] = w1_ref[...].astype(BF16)
        w3_scr[...] = w3_ref[...].astype(BF16)
        w2_scr[...] = w2_ref[...].astype(BF16)

    @pl.when(i < nt_ref[0])
    def _():
        x = xs_ref[...]
        gate = gs_ref[:, 0:1]
        acc_scr[...] = jnp.zeros_like(acc_scr)
        for c0 in range(0, EXPERT_FF, FF_CHUNK):
            cs = slice(c0, min(c0 + FF_CHUNK, EXPERT_FF))
            a = jnp.dot(x, w1_scr[:, cs], preferred_element_type=F32)
            b = jnp.dot(x, w3_scr[:, cs], preferred_element_type=F32)
            act = (_silu(a) * b * gate).astype(BF16)
            acc_scr[...] += jnp.dot(act, w2_scr[cs, :], preferred_element_type=F32)
        ys_ref[...] = acc_scr[...].astype(BF16)

    @pl.when(i >= nt_ref[0])
    def _():
        ys_ref[...] = jnp.zeros_like(ys_ref)


def _experts(tile_expert, n_tiles, xs, gs, w1, w3, w2):
    w_in = pl.BlockSpec((None, D_MODEL, EXPERT_FF), lambda i, te, nt: (te[i], 0, 0))
    w_out = pl.BlockSpec((None, EXPERT_FF, D_MODEL), lambda i, te, nt: (te[i], 0, 0))
    return pl.pallas_call(
        _expert_kernel,
        grid_spec=pltpu.PrefetchScalarGridSpec(
            num_scalar_prefetch=2,
            grid=(N_EXP_TILES,),
            in_specs=[pl.BlockSpec((EXP_TILE, D_MODEL), lambda i, te, nt: (i, 0)),
                      pl.BlockSpec((EXP_TILE, LANES), lambda i, te, nt: (i, 0)),
                      w_in, w_in, w_out],
            out_specs=pl.BlockSpec((EXP_TILE, D_MODEL), lambda i, te, nt: (i, 0)),
            scratch_shapes=[pltpu.VMEM((D_MODEL, EXPERT_FF), BF16), pltpu.VMEM((D_MODEL, EXPERT_FF), BF16),
                            pltpu.VMEM((EXPERT_FF, D_MODEL), BF16), pltpu.VMEM((EXP_TILE, D_MODEL), F32)]),
        out_shape=jax.ShapeDtypeStruct((N_EXP_TILES * EXP_TILE, D_MODEL), BF16),
        compiler_params=_cparams("arbitrary"),
        name="moe_experts",
    )(tile_expert, n_tiles, xs, gs, w1, w3, w2)


def _combine_kernel(off_ref, cum_ref, ys_ref, rank_ref, x_ref, gate_ref, lng_ref, lnb_ref, op_ref, os_ref, acc_scr):
    c = pl.program_id(0)
    acc_scr[...] = jnp.zeros_like(acc_scr)
    rank = rank_ref[...]
    lane = lax.broadcasted_iota(jnp.int32, rank.shape, 1)
    cols = lax.broadcasted_iota(jnp.int32, (1, SORT_TILE), 1)
    for e in range(N_EXPERTS):
        lo = off_ref[e] + cum_ref[c * N_EXPERTS + e]
        hi = off_ref[e] + cum_ref[(c + 1) * N_EXPERTS + e]
        r = jnp.sum(jnp.where(lane == e, rank, 0.0), -1, keepdims=True)
        pos = jnp.where(r >= 0.0, r + jnp.full((1, 1), off_ref[e], jnp.int32).astype(F32), -1.0)
        first = lo // SORT_TILE
        for k in range(2):
            s = first + k

            @pl.when((hi > lo) & (s * SORT_TILE < hi))
            def _():
                pick = pos == (s * SORT_TILE + cols).astype(F32)
                rows = ys_ref[pl.ds(pl.multiple_of(s * SORT_TILE, SORT_TILE), SORT_TILE), :]
                acc_scr[...] += jnp.dot(jnp.where(pick, 1.0, 0.0).astype(BF16), rows, preferred_element_type=F32)

    g = _group_of_tile(c, TOK_TILE)
    z = ALPHA * x_ref[...] + gate_ref[pl.ds(g, 1), :] * acc_scr[...]
    y = _layer_norm(z, lng_ref[...], lnb_ref[...])

    @pl.when(c < N_PROMPT_TOK_TILES)
    def _():
        op_ref[...] = y

    @pl.when(c >= N_PROMPT_TOK_TILES)
    def _():
        os_ref[...] = y


def _combine(off, cum, ys, rank, x, mod, ln_g, ln_b):
    const = lambda shape: pl.BlockSpec(shape, lambda c, *_: (0, 0))
    last_p = N_PROMPT_TOK_TILES - 1
    return pl.pallas_call(
        _combine_kernel,
        grid_spec=pltpu.PrefetchScalarGridSpec(
            num_scalar_prefetch=2,
            grid=(N_TOK_TILES,),
            in_specs=[pl.BlockSpec(ys.shape, lambda c, *_: (0, 0), pipeline_mode=pl.Buffered(1)),
                      pl.BlockSpec((TOK_TILE, LANES), lambda c, *_: (c, 0)),
                      pl.BlockSpec((TOK_TILE, D_MODEL), lambda c, *_: (c, 0)),
                      pl.BlockSpec((MOD_ROWS, D_MODEL), lambda c, *_: (0, 5)),
                      const((1, D_MODEL)), const((1, D_MODEL))],
            out_specs=[pl.BlockSpec((TOK_TILE, D_MODEL), lambda c, *_: (jnp.minimum(c, last_p), 0)),
                       pl.BlockSpec((TOK_TILE, D_MODEL), lambda c, *_: (jnp.maximum(c - last_p - 1, 0), 0))],
            scratch_shapes=[pltpu.VMEM((TOK_TILE, D_MODEL), F32)]),
        out_shape=[jax.ShapeDtypeStruct((T_PROMPT, D_MODEL), F32), jax.ShapeDtypeStruct((T_SAMPLE, D_MODEL), F32)],
        compiler_params=_cparams("arbitrary"),
        name="moe_combine_ln",
    )(off, cum, ys, rank, x, mod, ln_g.reshape(1, -1), ln_b.reshape(1, -1))


def _moe(x, mod, ln_g, ln_b, router, w1, w3, w2):
    h, rank, rank_t, gate_t, cum = _route(x, mod, router)
    counts = cum[N_TOK_TILES, :N_EXPERTS]
    tiles = (counts + EXP_TILE - 1) // EXP_TILE
    ends = jnp.cumsum(tiles)
    off = ((ends - tiles) * EXP_TILE).astype(jnp.int32)
    n_tiles = ends[-1:].astype(jnp.int32)
    tile_ids = jnp.minimum(jnp.arange(N_EXP_TILES, dtype=jnp.int32), n_tiles - 1)
    tile_expert = jnp.sum((tile_ids[:, None] >= ends[None, :]).astype(jnp.int32), -1)
    sub = EXP_TILE // SORT_TILE
    cum_flat = cum[:N_TOK_TILES + 1, :N_EXPERTS].reshape(-1)
    xs, gs = _gather(jnp.repeat(tile_expert, sub), off, cum_flat, n_tiles * sub, h, rank_t, gate_t)
    ys = _experts(tile_expert, n_tiles, xs, gs, w1, w3, w2)
    return _combine(off, cum_flat, ys, rank, x, mod, ln_g, ln_b)


FF_CHUNK = 256


def _ffn_kernel(x_ref, sh_ref, sc_ref, gate_ref, lng_ref, lnb_ref, w1_ref, w3_ref, w2_ref, o_ref, h_scr, acc_scr,
                *, tm, ff):
    i, j = pl.program_id(0), pl.program_id(1)
    g = _group_of_tile(i, tm)

    @pl.when(j == 0)
    def _():
        h_scr[...] = (x_ref[...] * (1.0 + sc_ref[pl.ds(g, 1), :]) + sh_ref[pl.ds(g, 1), :]).astype(BF16)
        acc_scr[...] = jnp.zeros_like(acc_scr)

    h = h_scr[...]
    for c0 in range(0, ff, FF_CHUNK):
        cs = slice(c0, min(c0 + FF_CHUNK, ff))
        a = jnp.dot(h, w1_ref[:, cs].astype(BF16), preferred_element_type=F32)
        b = jnp.dot(h, w3_ref[:, cs].astype(BF16), preferred_element_type=F32)
        act = _silu(a) * b
        acc_scr[...] += jnp.dot(act.astype(BF16), w2_ref[cs, :].astype(BF16), preferred_element_type=F32)

    @pl.when(j == pl.num_programs(1) - 1)
    def _():
        z = ALPHA * x_ref[...] + gate_ref[pl.ds(g, 1), :] * acc_scr[...]
        o_ref[...] = _layer_norm(z, lng_ref[...], lnb_ref[...])


def _ffn(x, mod, ln_g, ln_b, w1, w3, w2):
    tm = 1024
    ff = FF_CHUNK
    n_j = D_FF // ff
    w_in_spec = pl.BlockSpec((D_MODEL, ff), lambda i, j: (0, j))
    w_out_spec = pl.BlockSpec((ff, D_MODEL), lambda i, j: (j, 0))
    mod_spec = lambda col: pl.BlockSpec((MOD_ROWS, D_MODEL), lambda i, j: (0, col))
    const = lambda shape: pl.BlockSpec(shape, lambda i, j: (0, 0))
    in_specs = [pl.BlockSpec((tm, D_MODEL), lambda i, j: (i, 0)),
                mod_spec(3), mod_spec(4), mod_spec(5),
                const((1, D_MODEL)), const((1, D_MODEL)),
                w_in_spec, w_in_spec, w_out_spec]
    args = [x, mod, mod, mod, ln_g.reshape(1, -1), ln_b.reshape(1, -1), w1, w3, w2]
    return pl.pallas_call(
        functools.partial(_ffn_kernel, tm=tm, ff=ff),
        grid=(T_ALL // tm, n_j),
        in_specs=in_specs,
        out_specs=pl.BlockSpec((tm, D_MODEL), lambda i, j: (i, 0)),
        out_shape=jax.ShapeDtypeStruct((T_ALL, D_MODEL), F32),
        scratch_shapes=[pltpu.VMEM((tm, D_MODEL), BF16), pltpu.VMEM((tm, D_MODEL), F32)],
        compiler_params=_cparams("parallel", "arbitrary"),
        name="ffn",
    )(*args)


def kernel(x_prompt, x_sample, c, cache_k_b, cache_v_b, cache_k_c, cache_v_c, state_ret, c_ctx, ada_w, ada_b, ln_g, ln_b, w_in_ab, hy_conv_w, hy_conv_b, hf_w1, hf_b1, hf_freq, hf_w2, hf_b2, hf_w3, hy_skip, sink_b, w_out_ab, ffn_w1, ffn_w3, ffn_w2, w_in_cd, qn_g, kn_g, ret_decay, w_out_cd, moe_router, moe_w1, moe_w3, moe_w2):
    x = jnp.concatenate([x_prompt.reshape(T_PROMPT, D_MODEL), x_sample.reshape(T_SAMPLE, D_MODEL)], 0)
    cvec = jnp.concatenate([c_ctx[None], c, jnp.zeros((MOD_ROWS - 1 - DEC_BATCH, D_MODEL), F32)], 0)
    mod = _modulation(cvec, ada_w, ada_b)
    kw = N_KV * HEAD_DIM
    prompt = dict(row0=0, n_seq=BATCH, l=SEQ)
    sample = dict(row0=T_PROMPT, n_seq=DEC_BATCH, l=DEC_SEQ)

    u = _inproj(x, mod[0], w_in_ab[0], tn=768)
    filt_args = (hf_w1[0], hf_b1[0], hf_freq[0], hf_w2[0], hf_b2[0], hf_w3[0])
    hy_args = (hy_conv_w[0], hy_conv_b[0], hy_skip[0])
    ya = _hyena(u, 0, BATCH, SEQ, *hy_args, _hyena_filter(SEQ, *filt_args))
    ya = _hyena(u, T_PROMPT // DEC_SEQ, DEC_BATCH, DEC_SEQ, *hy_args, _hyena_filter(DEC_SEQ, *filt_args), into=ya)
    cols_b = dict(q_col=AB_Q, k_col=AB_K, v_col=AB_V)
    ctx_b = (cache_k_b[:, 0].reshape(DEC_BATCH, PAST_LEN, kw), cache_v_b[:, 0].reshape(DEC_BATCH, PAST_LEN, kw))
    yb, = _attention(u, **prompt, **cols_b, sink=sink_b[0])
    yb, = _attention(u, **sample, **cols_b, sink=sink_b[0], ctx=ctx_b, rope=True, band=True, into=yb)
    k_b = u[:T_PROMPT, AB_K:AB_V].reshape(BATCH, 1, SEQ, N_KV, HEAD_DIM)
    v_b = u[:T_PROMPT, AB_V:IN_AB].reshape(BATCH, 1, SEQ, N_KV, HEAD_DIM)
    x = _outproj(ya, yb, w_out_ab[0], x, mod[0], ln_g[0, 0], ln_b[0, 0])
    x = _ffn(x, mod[0], ln_g[0, 1], ln_b[0, 1], ffn_w1[0], ffn_w3[0], ffn_w2[0])

    u = _inproj(x, mod[1], w_in_cd[0], tn=1664)
    cols_c = dict(q_col=CD_Q, k_col=CD_K, v_col=CD_V)
    gains = (qn_g[0], kn_g[0])
    ctx_c = (cache_k_c[:, 0].reshape(DEC_BATCH, PAST_LEN, kw), cache_v_c[:, 0].reshape(DEC_BATCH, PAST_LEN, kw))
    yc, k_c = _attention(u, **prompt, **cols_c, qk_gain=gains, emit_k=True)
    yc, = _attention(u, **sample, **cols_c, qk_gain=gains, ctx=ctx_c, rope=True, into=yc)
    yd, s_r = _retention(u, ret_decay[0], **prompt, emit_state=True)
    yd, = _retention(u, ret_decay[0], **sample, rope=True, s0=state_ret[:, 0], into=yd)
    k_c = k_c.reshape(BATCH, 1, SEQ, N_KV, HEAD_DIM)
    v_c = u[:T_PROMPT, CD_V:CD_RQ].reshape(BATCH, 1, SEQ, N_KV, HEAD_DIM)
    x = _outproj(yc, yd, w_out_cd[0], x, mod[1], ln_g[1, 0], ln_b[1, 0])
    y_prompt, y_sample = _moe(x, mod[1], ln_g[1, 1], ln_b[1, 1], moe_router[0], moe_w1[0], moe_w3[0], moe_w2[0])

    return (y_prompt.reshape(BATCH, SEQ, D_MODEL), y_sample.reshape(DEC_BATCH, DEC_SEQ, D_MODEL),
            k_b, v_b, k_c, v_c, s_r[:, None])
```

```python
import functools
import math

import numpy as np
import jax
import jax.numpy as jnp
from jax import lax
from jax.experimental import pallas as pl
from jax.experimental.pallas import tpu as pltpu

F32 = jnp.float32
BF16 = jnp.bfloat16

D_MODEL = 1024
BATCH = 16
SEQ = 256
DEC_BATCH = 2
DEC_SEQ = 1024
PAST_LEN = 512
GRID_W = 64
HEAD_DIM = 64
BLOCK = 128
HY_W = 512
POS_BANDS = 16
POS_EMB = 1 + 2 * POS_BANDS
FILT_HID = 64
HY_FAST_DECAY = 0.3
HY_SLOW_DECAY = 1.5
HY_TARGET = 1e-2
N_HEADS = 8
N_KV = 2
GROUPS = N_HEADS // N_KV
RET_HEADS = 4
RET_D = 128
CHUNK = 128
ROPE_BASE = 10000.0
D_FF = 2816
N_EXPERTS = 8
EXPERT_FF = 1408
DEPTH = 2
ALPHA = (2 * DEPTH) ** 0.25
EPS = 1e-6

T_PROMPT = BATCH * SEQ
T_SAMPLE = DEC_BATCH * DEC_SEQ
T_ALL = T_PROMPT + T_SAMPLE
GROUP_ROWS = 1024
N_PROMPT_GROUPS = T_PROMPT // GROUP_ROWS
MOD_ROWS = 16
LANES = 128
VMEM_LIMIT = 58 * 1024 * 1024

AB_Q = 3 * HY_W
AB_K = AB_Q + N_HEADS * HEAD_DIM
AB_V = AB_K + N_KV * HEAD_DIM
IN_AB = AB_V + N_KV * HEAD_DIM
CD_Q = 0
CD_K = N_HEADS * HEAD_DIM
CD_V = CD_K + N_KV * HEAD_DIM
CD_RQ = CD_V + N_KV * HEAD_DIM
CD_RK = CD_RQ + RET_HEADS * RET_D
CD_RV = CD_RK + RET_HEADS * RET_D
CD_GF = CD_RV + RET_HEADS * RET_D
CD_GB = CD_GF + RET_HEADS * RET_D
IN_CD = CD_GB + RET_HEADS * RET_D


def _cparams(*sem):
    return pltpu.CompilerParams(dimension_semantics=sem, vmem_limit_bytes=VMEM_LIMIT)


def _silu(x):
    return x * jax.nn.sigmoid(x)


def _bdot(a, b):
    return jnp.dot(a.astype(BF16), b.astype(BF16), preferred_element_type=F32)


def _bdot_nt(a, b):
    return lax.dot_general(a.astype(BF16), b.astype(BF16), (((1,), (1,)), ((), ())),
                           preferred_element_type=F32)


def _bdot_tn(a, b):
    return lax.dot_general(a.astype(BF16), b.astype(BF16), (((0,), (0,)), ((), ())),
                           preferred_element_type=F32)


def _layer_norm(z, g, b):
    mu = jnp.mean(z, -1, keepdims=True)
    zc = z - mu
    var = jnp.mean(zc * zc, -1, keepdims=True)
    return zc * lax.rsqrt(var + EPS) * g + b


def _fill_rows(kern, in_specs, args, into):
    n_in = len(args)
    in_specs.append(pl.BlockSpec(memory_space=pl.ANY))
    args.append(into)

    def kern_into(*refs):
        return kern(*refs[:n_in], *refs[n_in + 1:])

    return kern_into, {n_in: 0}


def _group_of_tile(i, tm):
    return jnp.maximum(i // (GROUP_ROWS // tm) - (N_PROMPT_GROUPS - 1), 0)


def _rope_tables(n_tokens, d, reps):
    nf = d // 4
    inv = ROPE_BASE ** (-np.arange(nf, dtype=np.float64) / nf)
    pos = np.arange(n_tokens)
    row, col = pos // GRID_W, pos % GRID_W
    ang_r = row[:, None] * inv[None, :]
    ang_c = col[:, None] * inv[None, :]
    zeros = np.zeros_like(ang_r)
    cos = np.concatenate([np.cos(ang_r), np.cos(ang_r), np.cos(ang_c), np.cos(ang_c)], -1)
    sin_a = np.concatenate([-np.sin(ang_r), zeros, -np.sin(ang_c), zeros], -1)
    sin_b = np.concatenate([zeros, np.sin(ang_r), zeros, np.sin(ang_c)], -1)
    tile = lambda a: jnp.asarray(np.tile(a, (1, reps)), F32)
    return tile(cos), tile(sin_a), tile(sin_b)


def _dft_mats(l):
    n = 2 * l
    k = np.arange(l, dtype=np.float64)
    ang = 2.0 * np.pi * np.outer(k, k) / n
    fc = np.cos(ang)
    fs = np.sin(ang)
    fs[0, :] = np.cos(np.pi * k)
    fwd = np.concatenate([fc, fs], 0)
    wk = np.full((l,), 2.0)
    wk[0] = 1.0
    inv = np.concatenate([fc.T * wk[None, :], fs.T * wk[None, :]], 1)
    return jnp.asarray(fwd, F32), jnp.asarray(inv, F32)


def _filter_consts(l):
    t = np.linspace(0.0, 1.0, l, dtype=np.float32).astype(np.float64)[:, None]
    w = (2.0 * math.pi * np.arange(l, dtype=np.float64) / l)[:, None]
    bands = np.linspace(1e-4, POS_BANDS - 1.0, POS_BANDS, dtype=np.float32).astype(np.float64)[None, :]
    z = np.concatenate([t, np.cos(bands * w), -np.sin(bands * w)], -1)
    z = np.pad(z, ((0, 0), (0, LANES - POS_EMB)))
    max_decay = math.log(HY_TARGET) / HY_FAST_DECAY
    min_decay = math.log(HY_TARGET) / HY_SLOW_DECAY
    deltas = np.linspace(min_decay, max_decay, HY_W, dtype=np.float32).astype(np.float64)
    window = np.exp(-t * np.abs(deltas)[None, :])
    return jnp.asarray(z, F32), jnp.asarray(window, F32)


def _mod_kernel(c_ref, w_ref, b_ref, o_ref):
    o_ref[...] = _bdot(_silu(c_ref[...]), w_ref[...]) + b_ref[...]


def _modulation(cvec, ada_w, ada_b):
    tn = 1536
    n = ada_w.shape[-1]
    return pl.pallas_call(
        _mod_kernel,
        grid=(DEPTH, n // tn),
        in_specs=[pl.BlockSpec((MOD_ROWS, D_MODEL), lambda l, j: (0, 0)),
                  pl.BlockSpec((None, D_MODEL, tn), lambda l, j: (l, 0, j)),
                  pl.BlockSpec((None, 1, tn), lambda l, j: (l, 0, j))],
        out_specs=pl.BlockSpec((None, MOD_ROWS, tn), lambda l, j: (l, 0, j)),
        out_shape=jax.ShapeDtypeStruct((DEPTH, MOD_ROWS, n), F32),
        compiler_params=_cparams("parallel", "parallel"),
        name="mod",
    )(cvec, ada_w, ada_b.reshape(DEPTH, 1, n))


def _inproj_kernel(x_ref, sh_ref, sc_ref, w_ref, o_ref, h_scr, w_scr, *, tm):
    j, i = pl.program_id(0), pl.program_id(1)
    rows = pl.ds(pl.multiple_of(i * tm, tm), tm)

    @pl.when(j == 0)
    def _():
        g = _group_of_tile(i, tm)
        sc = sc_ref[pl.ds(g, 1), :]
        sh = sh_ref[pl.ds(g, 1), :]
        h_scr[rows, :] = (x_ref[...] * (1.0 + sc) + sh).astype(BF16)

    @pl.when(i == 0)
    def _():
        w_scr[...] = w_ref[...].astype(BF16)

    o_ref[...] = jnp.dot(h_scr[rows, :], w_scr[...], preferred_element_type=F32)


def _inproj(x, mod, w, tn):
    tm = 1024
    n = w.shape[1]
    n_i = T_ALL // tm
    return pl.pallas_call(
        functools.partial(_inproj_kernel, tm=tm),
        grid=(n // tn, n_i),
        in_specs=[pl.BlockSpec((tm, D_MODEL), lambda j, i: (jnp.where(j == 0, i, n_i - 1), 0)),
                  pl.BlockSpec((MOD_ROWS, D_MODEL), lambda j, i: (0, 0)),
                  pl.BlockSpec((MOD_ROWS, D_MODEL), lambda j, i: (0, 1)),
                  pl.BlockSpec((D_MODEL, tn), lambda j, i: (0, j))],
        out_specs=pl.BlockSpec((tm, tn), lambda j, i: (i, j)),
        out_shape=jax.ShapeDtypeStruct((T_ALL, n), F32),
        scratch_shapes=[pltpu.VMEM((T_ALL, D_MODEL), BF16), pltpu.VMEM((D_MODEL, tn), BF16)],
        compiler_params=_cparams("arbitrary", "arbitrary"),
        name="inproj",
    )(x, mod, mod, w)


def _hyena_filter_kernel(z_ref, w1_ref, b1_ref, fr_ref, w2_ref, b2_ref, w3_ref, win_ref, fwd32_ref, inv32_ref,
                         hc_ref, hs_ref, hc2_ref, fwd_ref, inv_ref, *, l):
    hi = lax.Precision.HIGHEST
    fwd = fwd32_ref[...].astype(BF16)
    fwd_ref[...] = fwd
    inv_ref[...] = inv32_ref[...].astype(BF16)
    fr = fr_ref[...]
    h = jnp.sin(fr * (jnp.dot(z_ref[...], w1_ref[...], precision=hi, preferred_element_type=F32) + b1_ref[...]))
    h = jnp.sin(fr * (jnp.dot(h, w2_ref[...], precision=hi, preferred_element_type=F32) + b2_ref[...]))
    h = jnp.dot(h, w3_ref[...], precision=hi, preferred_element_type=F32)
    win = win_ref[...]
    hf = h[:, :HY_W] * win
    hb = h[:, HY_W:] * win
    p = _bdot(fwd, hf + hb)
    q = _bdot(fwd, hf - hb)
    row0 = lax.broadcasted_iota(jnp.int32, (l, 1), 0) == 0
    hc = p[:l]
    hc_ref[...] = hc
    hs_ref[...] = jnp.where(row0, 0.0, q[l:])
    hc2_ref[...] = jnp.where(row0, p[l:l + 1], hc)


def _hyena_filter(l, fw1, fb1, ffreq, fw2, fb2, fw3):
    z, window = _filter_consts(l)
    fwd, inv = _dft_mats(l)
    pad_c = LANES - FILT_HID
    w1 = jnp.pad(fw1, ((0, LANES - POS_EMB), (0, pad_c)))
    w2 = jnp.pad(fw2, ((0, pad_c), (0, pad_c)))
    w3 = jnp.pad(fw3, ((0, pad_c), (0, 0)))
    row = lambda a: jnp.pad(a, (0, pad_c)).reshape(1, LANES)
    shp = jax.ShapeDtypeStruct((l, HY_W), F32)
    return pl.pallas_call(
        functools.partial(_hyena_filter_kernel, l=l),
        out_shape=(shp, shp, shp, jax.ShapeDtypeStruct(fwd.shape, BF16), jax.ShapeDtypeStruct(inv.shape, BF16)),
        compiler_params=pltpu.CompilerParams(vmem_limit_bytes=VMEM_LIMIT),
        name=f"hyena_filter_{l}",
    )(z, w1, row(fb1), row(ffreq), w2, row(fb2), w3, window, fwd, inv)


def _hyena_kernel(u_ref, cw_ref, cb_ref, skip_ref, fwd_ref, inv_ref, hc_ref, hs_ref, hc2_ref, o_ref, *, l):
    u = u_ref[...]
    rows = lax.broadcasted_iota(jnp.int32, (l, 1), 0)
    prev = jnp.where(rows == 0, 0.0, pltpu.roll(u, 1, 0))
    nxt = jnp.where(rows == l - 1, 0.0, pltpu.roll(u, l - 1, 0))
    uc = prev * cw_ref[0:1, :] + u * cw_ref[1:2, :] + nxt * cw_ref[2:3, :] + cb_ref[...]
    x0 = uc[:, :HY_W]
    x1 = uc[:, HY_W:2 * HY_W]
    v = uc[:, 2 * HY_W:] * x1
    ab = _bdot(fwd_ref[...], v)
    a, b = ab[:l], ab[l:]
    hs = hs_ref[...]
    re = a * hc_ref[...] - b * hs
    im = a * hs + b * hc2_ref[...]
    y = _bdot(inv_ref[...], jnp.concatenate([re, im], 0)) * (1.0 / (2 * l))
    o_ref[...] = ((y + skip_ref[...] * v) * x0).astype(o_ref.dtype)


def _hyena(u_all, row_block0, n_seq, l, conv_w, conv_b, skip, filt, into):
    hc, hs, hc2, fwd, inv = filt
    const = lambda shape: pl.BlockSpec(shape, lambda s: (0, 0))
    in_specs = [pl.BlockSpec((l, 3 * HY_W), lambda s: (row_block0 + s, 0)),
                const((3, 3 * HY_W)), const((1, 3 * HY_W)), const((1, HY_W)),
                const((2 * l, l)), const((l, 2 * l)),
                const((l, HY_W)), const((l, HY_W)), const((l, HY_W))]
    args = [u_all, conv_w, conv_b.reshape(1, -1), skip.reshape(1, -1), fwd, inv, hc, hs, hc2]
    kern, alias = _fill_rows(functools.partial(_hyena_kernel, l=l), in_specs, args, into)
    return pl.pallas_call(
        kern,
        grid=(n_seq,),
        in_specs=in_specs,
        out_specs=pl.BlockSpec((l, HY_W), lambda s: (row_block0 + s, 0)),
        out_shape=jax.ShapeDtypeStruct((T_ALL, HY_W), BF16),
        input_output_aliases=alias,
        compiler_params=_cparams("parallel"),
        name=f"hyena_{l}",
    )(*args)


def _seg_rms_norm(x, bd_ref, g):
    sq = x * x
    hi = sq.astype(BF16)
    lo = (sq - hi.astype(F32)).astype(BF16)
    bd = bd_ref[...]
    ss = (jnp.dot(hi, bd, preferred_element_type=F32) + jnp.dot(lo, bd, preferred_element_type=F32))
    return x * lax.rsqrt(ss * (1.0 / HEAD_DIM) + EPS) * g


def _rope(x, cos, sin_a, sin_b, quarter):
    w = x.shape[-1]
    return x * cos + pltpu.roll(x, w - quarter, 1) * sin_a + pltpu.roll(x, quarter, 1) * sin_b


def _attn_kernel(*refs, l, qb, lc, rope, qknorm, band, has_sink, emit_k):
    it = iter(refs)
    q_ref, k_ref, v_ref = next(it), next(it), next(it)
    if lc:
        ck_ref, cv_ref = next(it), next(it)
    if rope:
        cq_ref, saq_ref, sbq_ref = next(it), next(it), next(it)
        ckk_ref, sak_ref, sbk_ref = next(it), next(it), next(it)
    if qknorm:
        qg_ref, kg_ref, bdq_ref, bdk_ref = next(it), next(it), next(it), next(it)
    if has_sink:
        sink_ref = next(it)
    o_ref = next(it)
    if emit_k:
        kout_ref = next(it)
    kp_scr = next(it)

    qi = pl.program_id(1)

    @pl.when(qi == 0)
    def _():
        k = k_ref[...]
        if qknorm:
            k = _seg_rms_norm(k, bdk_ref, kg_ref[...])
        if emit_k:
            kout_ref[...] = k
        if rope:
            k = _rope(k, ckk_ref[...], sak_ref[...], sbk_ref[...], HEAD_DIM // 4)
        for kv in range(N_KV):
            kp_scr[kv] = k[:, kv * HEAD_DIM:(kv + 1) * HEAD_DIM].astype(BF16)

    q = q_ref[...]
    if qknorm:
        q = _seg_rms_norm(q, bdq_ref, qg_ref[...])
    if rope:
        q = _rope(q, cq_ref[...], saq_ref[...], sbq_ref[...], HEAD_DIM // 4)
    q = q * HEAD_DIM ** -0.5

    rows = GROUPS * qb
    if band:
        n_loc = 3 * BLOCK
        start = pl.multiple_of(jnp.clip((qi - 1) * BLOCK, 0, l - n_loc), BLOCK)
        keys = pl.ds(start, n_loc)
        tq = qi * qb + lax.broadcasted_iota(jnp.int32, (rows, 1), 0) % qb
        tk = start + lax.broadcasted_iota(jnp.int32, (1, n_loc), 1)
        valid = jnp.abs(tq - tk) <= BLOCK
    else:
        keys = slice(None)

    outs = []
    for kv in range(N_KV):
        lanes = slice(kv * HEAD_DIM, (kv + 1) * HEAD_DIM)
        qs = jnp.concatenate([q[:, (kv * GROUPS + g) * HEAD_DIM:(kv * GROUPS + g + 1) * HEAD_DIM]
                              for g in range(GROUPS)], 0)
        s = _bdot_nt(qs, kp_scr[kv, keys, :])
        if band:
            s = jnp.where(valid, s, -jnp.inf)
        m = jnp.max(s, -1, keepdims=True)
        if lc:
            s_c = _bdot_nt(qs, ck_ref[:, lanes])
            m = jnp.maximum(m, jnp.max(s_c, -1, keepdims=True))
        if has_sink:
            sk = jnp.concatenate([jnp.full((qb, 1), sink_ref[kv * GROUPS + g], F32)
                                  for g in range(GROUPS)], 0)
            m = jnp.maximum(m, sk)
        e = jnp.exp(s - m)
        den = jnp.sum(e, -1, keepdims=True)
        o = _bdot(e, v_ref[keys, lanes])
        if lc:
            e_c = jnp.exp(s_c - m)
            den = den + jnp.sum(e_c, -1, keepdims=True)
            o = o + _bdot(e_c, cv_ref[:, lanes])
        if has_sink:
            den = den + jnp.exp(sk - m)
        o = o / den
        outs.extend(o[g * qb:(g + 1) * qb] for g in range(GROUPS))
    o_ref[...] = jnp.concatenate(outs, 1).astype(o_ref.dtype)


def _attention(u, *, row0, n_seq, l, q_col, k_col, v_col, into, qb=BLOCK, ctx=None, rope=False, qk_gain=None,
               band=False, sink=None, emit_k=False):
    nq = l // qb
    qw = N_HEADS * HEAD_DIM
    kw = N_KV * HEAD_DIM
    qb0, sb0 = row0 // qb, row0 // l
    seq_spec = lambda col: pl.BlockSpec((l, kw), lambda b, i: (sb0 + b, col // kw))
    const = lambda shape: pl.BlockSpec(shape, lambda b, i: (0, 0))
    in_specs = [pl.BlockSpec((qb, qw), lambda b, i: (qb0 + b * nq + i, q_col // qw)),
                seq_spec(k_col), seq_spec(v_col)]
    args = [u, u, u]
    lc = 0
    if ctx is not None:
        lc = ctx[0].shape[1]
        in_specs += [pl.BlockSpec((None, lc, kw), lambda b, i: (b, 0, 0))] * 2
        args += list(ctx)
    if rope:
        tabs = _rope_tables(l, HEAD_DIM, N_HEADS)
        in_specs += [pl.BlockSpec((qb, qw), lambda b, i: (i, 0))] * 3 + [const((l, kw))] * 3
        args += list(tabs) + list(tabs)
    if qk_gain is not None:
        bd = np.kron(np.eye(N_HEADS), np.ones((HEAD_DIM, HEAD_DIM)))
        in_specs += [const((1, qw)), const((1, kw)), const((qw, qw)), const((kw, kw))]
        args += [jnp.tile(qk_gain[0], N_HEADS).reshape(1, qw), jnp.tile(qk_gain[1], N_KV).reshape(1, kw),
                 jnp.asarray(bd, BF16), jnp.asarray(bd[:kw, :kw], BF16)]
    if sink is not None:
        in_specs.append(pl.BlockSpec(memory_space=pltpu.SMEM))
        args.append(sink)
    out_specs = [pl.BlockSpec((qb, qw), lambda b, i: (qb0 + b * nq + i, 0))]
    out_shape = [jax.ShapeDtypeStruct((T_ALL, qw), BF16)]
    if emit_k:
        out_specs.append(pl.BlockSpec((l, kw), lambda b, i: (b, 0)))
        out_shape.append(jax.ShapeDtypeStruct((n_seq * l, kw), F32))
    kern = functools.partial(_attn_kernel, l=l, qb=qb, lc=lc, rope=rope, qknorm=qk_gain is not None, band=band,
                             has_sink=sink is not None, emit_k=emit_k)
    kern, alias = _fill_rows(kern, in_specs, args, into)
    return pl.pallas_call(
        kern,
        grid=(n_seq, nq),
        in_specs=in_specs,
        out_specs=out_specs,
        out_shape=out_shape,
        input_output_aliases=alias,
        scratch_shapes=[pltpu.VMEM((N_KV, l, HEAD_DIM), BF16)],
        compiler_params=_cparams("parallel", "arbitrary"),
        name=f"attn_{l}_{'b' if sink is not None else 'c'}",
    )(*args)


def _ret_kernel(*refs, l, rope, has_s0, emit_state):
    it = iter(refs)
    dec_ref = next(it)
    rq_ref, rk_ref, rv_ref, gf_ref, gb_ref = next(it), next(it), next(it), next(it), next(it)
    if rope:
        cos_ref, sa_ref, sb_ref = next(it), next(it), next(it)
    if has_s0:
        s0_ref = next(it)
    o_ref = next(it)
    if emit_state:
        st_ref = next(it)

    h = pl.program_id(1)
    q = rq_ref[...] * RET_D ** -0.5
    k = rk_ref[...]
    v = rv_ref[...]
    if rope:
        q = _rope(q, cos_ref[...], sa_ref[...], sb_ref[...], RET_D // 4)
        k = _rope(k, cos_ref[...], sa_ref[...], sb_ref[...], RET_D // 4)
    n = l // CHUNK
    ii = lax.broadcasted_iota(jnp.int32, (CHUNK, 1), 0).astype(F32)
    jj = lax.broadcasted_iota(jnp.int32, (1, CHUNK), 1).astype(F32)
    diff = ii - jj
    y = None
    for d in range(2):
        log_g = jnp.log(jax.nn.sigmoid(jnp.full((1, 1), dec_ref[d, h], F32)))
        if d == 0:
            mask = jnp.exp(jnp.where(diff >= 0, diff * log_g, -jnp.inf))
            q_dec = jnp.exp((ii + 1.0) * log_g)
            k_dec = jnp.exp((CHUNK - 1.0 - ii) * log_g)
            order = range(n)
        else:
            mask = jnp.exp(jnp.where(diff <= 0, -diff * log_g, -jnp.inf))
            q_dec = jnp.exp((CHUNK - ii) * log_g)
            k_dec = jnp.exp(ii * log_g)
            order = reversed(range(n))
        c_dec = jnp.exp(CHUNK * log_g)
        state = s0_ref[d] if has_s0 else jnp.zeros((RET_D, RET_D), F32)
        o_chunks = [None] * n
        for c in order:
            sl = slice(c * CHUNK, (c + 1) * CHUNK)
            qc, kc, vc = q[sl], k[sl], v[sl]
            inner = _bdot_nt(qc, kc) * mask
            o_chunks[c] = _bdot(inner, vc) + _bdot(qc * q_dec, state)
            state = state * c_dec + _bdot_tn(kc * k_dec, vc)
        if emit_state:
            st_ref[d] = state
        o = jnp.concatenate(o_chunks, 0)
        o = o * lax.rsqrt(jnp.mean(o * o, -1, keepdims=True) + EPS)
        gate = _silu((gf_ref if d == 0 else gb_ref)[...])
        y = gate * o if y is None else y + gate * o
    o_ref[...] = y.astype(o_ref.dtype)


def _retention(u, ret_decay, *, row0, n_seq, l, into, rope=False, s0=None, emit_state=False):
    sb0 = row0 // l
    col = lambda c0: pl.BlockSpec((l, RET_D), lambda b, h: (sb0 + b, c0 // RET_D + h))
    in_specs = [pl.BlockSpec(memory_space=pltpu.SMEM),
                col(CD_RQ), col(CD_RK), col(CD_RV), col(CD_GF), col(CD_GB)]
    args = [ret_decay, u, u, u, u, u]
    if rope:
        in_specs += [pl.BlockSpec((l, RET_D), lambda b, h: (0, 0))] * 3
        args += list(_rope_tables(l, RET_D, 1))
    state_spec = pl.BlockSpec((None, 2, None, RET_D, RET_D), lambda b, h: (b, 0, h, 0, 0))
    if s0 is not None:
        in_specs.append(state_spec)
        args.append(s0)
    out_specs = [pl.BlockSpec((l, RET_D), lambda b, h: (sb0 + b, h))]
    out_shape = [jax.ShapeDtypeStruct((T_ALL, RET_HEADS * RET_D), BF16)]
    if emit_state:
        out_specs.append(state_spec)
        out_shape.append(jax.ShapeDtypeStruct((n_seq, 2, RET_HEADS, RET_D, RET_D), F32))
    kern = functools.partial(_ret_kernel, l=l, rope=rope, has_s0=s0 is not None, emit_state=emit_state)
    kern, alias = _fill_rows(kern, in_specs, args, into)
    return pl.pallas_call(
        kern,
        grid=(n_seq, RET_HEADS),
        in_specs=in_specs,
        out_specs=out_specs,
        out_shape=out_shape,
        input_output_aliases=alias,
        compiler_params=_cparams("parallel", "parallel"),
        name=f"retention_{l}",
    )(*args)


def _outproj_kernel(ya_ref, yb_ref, w_ref, x_ref, gate_ref, lng_ref, lnb_ref, o_ref, w_scr, *, tm):
    i = pl.program_id(0)

    @pl.when(i == 0)
    def _():
        w_scr[...] = w_ref[...].astype(BF16)

    half = ya_ref.shape[1]
    m = (jnp.dot(ya_ref[...], w_scr[:half], preferred_element_type=F32)
         + jnp.dot(yb_ref[...], w_scr[half:], preferred_element_type=F32))
    gate = gate_ref[pl.ds(_group_of_tile(i, tm), 1), :]
    o_ref[...] = _layer_norm(ALPHA * x_ref[...] + gate * m, lng_ref[...], lnb_ref[...])


def _outproj(ya, yb, w, x, mod, ln_g, ln_b):
    tm = 512
    half = ya.shape[1]
    const = lambda shape: pl.BlockSpec(shape, lambda i: (0, 0))
    return pl.pallas_call(
        functools.partial(_outproj_kernel, tm=tm),
        grid=(T_ALL // tm,),
        in_specs=[pl.BlockSpec((tm, half), lambda i: (i, 0)),
                  pl.BlockSpec((tm, half), lambda i: (i, 0)),
                  const((2 * half, D_MODEL)),
                  pl.BlockSpec((tm, D_MODEL), lambda i: (i, 0)),
                  pl.BlockSpec((MOD_ROWS, D_MODEL), lambda i: (0, 2)),
                  const((1, D_MODEL)), const((1, D_MODEL))],
        out_specs=pl.BlockSpec((tm, D_MODEL), lambda i: (i, 0)),
        out_shape=jax.ShapeDtypeStruct((T_ALL, D_MODEL), F32),
        scratch_shapes=[pltpu.VMEM((2 * half, D_MODEL), BF16)],
        compiler_params=_cparams("arbitrary"),
        name="outproj_ln",
    )(ya, yb, w, x, mod, ln_g.reshape(1, -1), ln_b.reshape(1, -1))


TOK_TILE = 256
N_TOK_TILES = T_ALL // TOK_TILE
SORT_TILE = 256
EXP_TILE = 512
N_EXP_TILES = (2 * T_ALL) // EXP_TILE + N_EXPERTS
N_SORT_TILES = N_EXP_TILES * (EXP_TILE // SORT_TILE)
N_PROMPT_TOK_TILES = T_PROMPT // TOK_TILE
CUM_ROWS = 32


def _route_kernel(x_ref, sh_ref, sc_ref, r_ref, h_ref, rank_ref, rank_t_ref, gate_t_ref, cum_ref,
                  carry_row, carry_col):
    c = pl.program_id(0)

    @pl.when(c == 0)
    def _():
        carry_row[...] = jnp.zeros_like(carry_row)
        carry_col[...] = jnp.zeros_like(carry_col)
        cum_ref[...] = jnp.zeros_like(cum_ref)

    g = _group_of_tile(c, TOK_TILE)
    h = x_ref[...] * (1.0 + sc_ref[pl.ds(g, 1), :]) + sh_ref[pl.ds(g, 1), :]
    h_ref[...] = h.astype(BF16)
    logits = jnp.dot(h, r_ref[...], precision=lax.Precision.HIGHEST, preferred_element_type=F32)
    lane = lax.broadcasted_iota(jnp.int32, logits.shape, 1)
    logits = jnp.where(lane < N_EXPERTS, logits, -jnp.inf)
    m1 = jnp.max(logits, -1, keepdims=True)
    i1 = jnp.min(jnp.where(logits == m1, lane, LANES), -1, keepdims=True)
    rest = jnp.where(lane == i1, -jnp.inf, logits)
    m2 = jnp.max(rest, -1, keepdims=True)
    i2 = jnp.min(jnp.where(rest == m2, lane, LANES), -1, keepdims=True)
    e2 = jnp.exp(m2 - m1)
    den = 1.0 + e2
    gates = jnp.where(lane == i1, 1.0 / den, 0.0) + jnp.where(lane == i2, e2 / den, 0.0)
    sel = jnp.where((lane == i1) | (lane == i2), 1.0, 0.0)
    sel_t = sel.T
    ti = lax.broadcasted_iota(jnp.int32, (TOK_TILE, TOK_TILE), 0)
    tj = lax.broadcasted_iota(jnp.int32, (TOK_TILE, TOK_TILE), 1)
    before = jnp.where(tj < ti, 1.0, 0.0).astype(BF16)
    rank = jnp.dot(before, sel.astype(BF16), preferred_element_type=F32) + carry_row[...]
    rank_t = lax.dot_general(sel_t.astype(BF16), before, (((1,), (1,)), ((), ())),
                             preferred_element_type=F32) + carry_col[...]
    rank_ref[...] = jnp.where(sel > 0.0, rank, -1.0)
    rank_t_ref[...] = jnp.where(sel_t > 0.0, rank_t, -1.0)[:N_EXPERTS]
    gate_t_ref[...] = gates.T[:N_EXPERTS]
    cum_ref[pl.ds(c, 1), :] = carry_row[...].astype(jnp.int32)
    carry_row[...] += jnp.sum(sel, 0, keepdims=True)
    carry_col[...] += jnp.sum(sel_t, 1, keepdims=True)

    @pl.when(c == N_TOK_TILES - 1)
    def _():
        cum_ref[pl.ds(N_TOK_TILES, 1), :] = carry_row[...].astype(jnp.int32)


def _route(x, mod, router):
    tile = lambda w: pl.BlockSpec((TOK_TILE, w), lambda c: (c, 0))
    tile_t = pl.BlockSpec((N_EXPERTS, TOK_TILE), lambda c: (0, c))
    return pl.pallas_call(
        _route_kernel,
        grid=(N_TOK_TILES,),
        in_specs=[tile(D_MODEL),
                  pl.BlockSpec((MOD_ROWS, D_MODEL), lambda c: (0, 3)),
                  pl.BlockSpec((MOD_ROWS, D_MODEL), lambda c: (0, 4)),
                  pl.BlockSpec((D_MODEL, LANES), lambda c: (0, 0))],
        out_specs=[tile(D_MODEL), tile(LANES), tile_t, tile_t,
                   pl.BlockSpec((CUM_ROWS, LANES), lambda c: (0, 0))],
        out_shape=[jax.ShapeDtypeStruct((T_ALL, D_MODEL), BF16),
                   jax.ShapeDtypeStruct((T_ALL, LANES), F32),
                   jax.ShapeDtypeStruct((N_EXPERTS, T_ALL), F32),
                   jax.ShapeDtypeStruct((N_EXPERTS, T_ALL), F32),
                   jax.ShapeDtypeStruct((CUM_ROWS, LANES), jnp.int32)],
        scratch_shapes=[pltpu.VMEM((1, LANES), F32), pltpu.VMEM((LANES, 1), F32)],
        compiler_params=_cparams("arbitrary"),
        name="route",
    )(x, mod, mod, jnp.pad(router, ((0, 0), (0, LANES - N_EXPERTS))))


def _gather_kernel(te_ref, off_ref, cum_ref, nt_ref, h_ref, rank_t_ref, gate_t_ref, xs_ref, gs_ref, acc_scr, g_scr):
    i = pl.program_id(0)
    e = te_ref[i]
    r0 = i * SORT_TILE - off_ref[e]
    acc_scr[...] = jnp.zeros_like(acc_scr)
    g_scr[...] = jnp.zeros_like(g_scr)
    want = (r0 + lax.broadcasted_iota(jnp.int32, (SORT_TILE, 1), 0)).astype(F32)
    for c in range(N_TOK_TILES):
        lo = cum_ref[c * N_EXPERTS + e]
        hi = cum_ref[(c + 1) * N_EXPERTS + e]

        @pl.when((i < nt_ref[0]) & (lo < r0 + SORT_TILE) & (hi > r0))
        def _():
            cols = slice(c * TOK_TILE, (c + 1) * TOK_TILE)
            pick = rank_t_ref[pl.ds(e, 1), cols] == want
            acc_scr[...] += jnp.dot(jnp.where(pick, 1.0, 0.0).astype(BF16), h_ref[cols, :],
                                    preferred_element_type=F32)
            g_scr[...] += jnp.sum(jnp.where(pick, gate_t_ref[pl.ds(e, 1), cols], 0.0), -1, keepdims=True)

    xs_ref[...] = acc_scr[...].astype(BF16)
    gs_ref[...] = jnp.broadcast_to(g_scr[...], gs_ref.shape)


def _gather(tile_expert, off, cum, n_tiles, h, rank_t, gate_t):
    const = lambda shape: pl.BlockSpec(shape, lambda i, *_: (0, 0))
    return pl.pallas_call(
        _gather_kernel,
        grid_spec=pltpu.PrefetchScalarGridSpec(
            num_scalar_prefetch=4,
            grid=(N_SORT_TILES,),
            in_specs=[const((T_ALL, D_MODEL)), const((N_EXPERTS, T_ALL)), const((N_EXPERTS, T_ALL))],
            out_specs=[pl.BlockSpec((SORT_TILE, D_MODEL), lambda i, *_: (i, 0)),
                       pl.BlockSpec((SORT_TILE, LANES), lambda i, *_: (i, 0))],
            scratch_shapes=[pltpu.VMEM((SORT_TILE, D_MODEL), F32), pltpu.VMEM((SORT_TILE, 1), F32)]),
        out_shape=[jax.ShapeDtypeStruct((N_SORT_TILES * SORT_TILE, D_MODEL), BF16),
                   jax.ShapeDtypeStruct((N_SORT_TILES * SORT_TILE, LANES), F32)],
        compiler_params=_cparams("parallel"),
        name="moe_gather",
    )(tile_expert, off, cum, n_tiles, h, rank_t, gate_t)


def _expert_kernel(te_ref, nt_ref, xs_ref, gs_ref, w1_ref, w3_ref, w2_ref, ys_ref, w1_scr, w3_scr, w2_scr, acc_scr):
    i = pl.program_id(0)

    @pl.when((i == 0) | (te_ref[i] != te_ref[jnp.maximum(i - 1, 0)]))
    def _():
        w1_scr[...] = w1_ref[...].astype(BF16)
        w3_scr[...] = w3_ref[...].astype(BF16)
        w2_scr[...] = w2_ref[...].astype(BF16)

    @pl.when(i < nt_ref[0])
    def _():
        x = xs_ref[...]
        gate = gs_ref[:, 0:1]
        acc_scr[...] = jnp.zeros_like(acc_scr)
        for c0 in range(0, EXPERT_FF, FF_CHUNK):
            cs = slice(c0, min(c0 + FF_CHUNK, EXPERT_FF))
            a = jnp.dot(x, w1_scr[:, cs], preferred_element_type=F32)
            b = jnp.dot(x, w3_scr[:, cs], preferred_element_type=F32)
            act = (_silu(a) * b * gate).astype(BF16)
            acc_scr[...] += jnp.dot(act, w2_scr[cs, :], preferred_element_type=F32)
        ys_ref[...] = acc_scr[...].astype(BF16)

    @pl.when(i >= nt_ref[0])
    def _():
        ys_ref[...] = jnp.zeros_like(ys_ref)


def _experts(tile_expert, n_tiles, xs, gs, w1, w3, w2):
    w_in = pl.BlockSpec((None, D_MODEL, EXPERT_FF), lambda i, te, nt: (te[i], 0, 0))
    w_out = pl.BlockSpec((None, EXPERT_FF, D_MODEL), lambda i, te, nt: (te[i], 0, 0))
    return pl.pallas_call(
        _expert_kernel,
        grid_spec=pltpu.PrefetchScalarGridSpec(
            num_scalar_prefetch=2,
            grid=(N_EXP_TILES,),
            in_specs=[pl.BlockSpec((EXP_TILE, D_MODEL), lambda i, te, nt: (i, 0)),
                      pl.BlockSpec((EXP_TILE, LANES), lambda i, te, nt: (i, 0)),
                      w_in, w_in, w_out],
            out_specs=pl.BlockSpec((EXP_TILE, D_MODEL), lambda i, te, nt: (i, 0)),
            scratch_shapes=[pltpu.VMEM((D_MODEL, EXPERT_FF), BF16), pltpu.VMEM((D_MODEL, EXPERT_FF), BF16),
                            pltpu.VMEM((EXPERT_FF, D_MODEL), BF16), pltpu.VMEM((EXP_TILE, D_MODEL), F32)]),
        out_shape=jax.ShapeDtypeStruct((N_EXP_TILES * EXP_TILE, D_MODEL), BF16),
        compiler_params=_cparams("arbitrary"),
        name="moe_experts",
    )(tile_expert, n_tiles, xs, gs, w1, w3, w2)


def _combine_kernel(off_ref, cum_ref, ys_ref, rank_ref, x_ref, gate_ref, lng_ref, lnb_ref, op_ref, os_ref, acc_scr):
    c = pl.program_id(0)
    acc_scr[...] = jnp.zeros_like(acc_scr)
    rank = rank_ref[...]
    lane = lax.broadcasted_iota(jnp.int32, rank.shape, 1)
    cols = lax.broadcasted_iota(jnp.int32, (1, SORT_TILE), 1)
    for e in range(N_EXPERTS):
        lo = off_ref[e] + cum_ref[c * N_EXPERTS + e]
        hi = off_ref[e] + cum_ref[(c + 1) * N_EXPERTS + e]
        r = jnp.sum(jnp.where(lane == e, rank, 0.0), -1, keepdims=True)
        pos = jnp.where(r >= 0.0, r + jnp.full((1, 1), off_ref[e], jnp.int32).astype(F32), -1.0)
        first = lo // SORT_TILE
        for k in range(2):
            s = first + k

            @pl.when((hi > lo) & (s * SORT_TILE < hi))
            def _():
                pick = pos == (s * SORT_TILE + cols).astype(F32)
                rows = ys_ref[pl.ds(pl.multiple_of(s * SORT_TILE, SORT_TILE), SORT_TILE), :]
                acc_scr[...] += jnp.dot(jnp.where(pick, 1.0, 0.0).astype(BF16), rows, preferred_element_type=F32)

    g = _group_of_tile(c, TOK_TILE)
    z = ALPHA * x_ref[...] + gate_ref[pl.ds(g, 1), :] * acc_scr[...]
    y = _layer_norm(z, lng_ref[...], lnb_ref[...])

    @pl.when(c < N_PROMPT_TOK_TILES)
    def _():
        op_ref[...] = y

    @pl.when(c >= N_PROMPT_TOK_TILES)
    def _():
        os_ref[...] = y


def _combine(off, cum, ys, rank, x, mod, ln_g, ln_b):
    const = lambda shape: pl.BlockSpec(shape, lambda c, *_: (0, 0))
    last_p = N_PROMPT_TOK_TILES - 1
    return pl.pallas_call(
        _combine_kernel,
        grid_spec=pltpu.PrefetchScalarGridSpec(
            num_scalar_prefetch=2,
            grid=(N_TOK_TILES,),
            in_specs=[pl.BlockSpec(ys.shape, lambda c, *_: (0, 0), pipeline_mode=pl.Buffered(1)),
                      pl.BlockSpec((TOK_TILE, LANES), lambda c, *_: (c, 0)),
                      pl.BlockSpec((TOK_TILE, D_MODEL), lambda c, *_: (c, 0)),
                      pl.BlockSpec((MOD_ROWS, D_MODEL), lambda c, *_: (0, 5)),
                      const((1, D_MODEL)), const((1, D_MODEL))],
            out_specs=[pl.BlockSpec((TOK_TILE, D_MODEL), lambda c, *_: (jnp.minimum(c, last_p), 0)),
                       pl.BlockSpec((TOK_TILE, D_MODEL), lambda c, *_: (jnp.maximum(c - last_p - 1, 0), 0))],
            scratch_shapes=[pltpu.VMEM((TOK_TILE, D_MODEL), F32)]),
        out_shape=[jax.ShapeDtypeStruct((T_PROMPT, D_MODEL), F32), jax.ShapeDtypeStruct((T_SAMPLE, D_MODEL), F32)],
        compiler_params=_cparams("arbitrary"),
        name="moe_combine_ln",
    )(off, cum, ys, rank, x, mod, ln_g.reshape(1, -1), ln_b.reshape(1, -1))


def _moe(x, mod, ln_g, ln_b, router, w1, w3, w2):
    h, rank, rank_t, gate_t, cum = _route(x, mod, router)
    counts = cum[N_TOK_TILES, :N_EXPERTS]
    tiles = (counts + EXP_TILE - 1) // EXP_TILE
    ends = jnp.cumsum(tiles)
    off = ((ends - tiles) * EXP_TILE).astype(jnp.int32)
    n_tiles = ends[-1:].astype(jnp.int32)
    tile_ids = jnp.minimum(jnp.arange(N_EXP_TILES, dtype=jnp.int32), n_tiles - 1)
    tile_expert = jnp.sum((tile_ids[:, None] >= ends[None, :]).astype(jnp.int32), -1)
    sub = EXP_TILE // SORT_TILE
    cum_flat = cum[:N_TOK_TILES + 1, :N_EXPERTS].reshape(-1)
    xs, gs = _gather(jnp.repeat(tile_expert, sub), off, cum_flat, n_tiles * sub, h, rank_t, gate_t)
    ys = _experts(tile_expert, n_tiles, xs, gs, w1, w3, w2)
    return _combine(off, cum_flat, ys, rank, x, mod, ln_g, ln_b)


FF_CHUNK = 256


def _ffn_kernel(x_ref, sh_ref, sc_ref, gate_ref, lng_ref, lnb_ref, w1_ref, w3_ref, w2_ref, o_ref, h_scr, acc_scr,
                *, tm, ff):
    i, j = pl.program_id(0), pl.program_id(1)
    g = _group_of_tile(i, tm)

    @pl.when(j == 0)
    def _():
        h_scr[...] = (x_ref[...] * (1.0 + sc_ref[pl.ds(g, 1), :]) + sh_ref[pl.ds(g, 1), :]).astype(BF16)
        acc_scr[...] = jnp.zeros_like(acc_scr)

    h = h_scr[...]
    for c0 in range(0, ff, FF_CHUNK):
        cs = slice(c0, min(c0 + FF_CHUNK, ff))
        a = jnp.dot(h, w1_ref[:, cs].astype(BF16), preferred_element_type=F32)
        b = jnp.dot(h, w3_ref[:, cs].astype(BF16), preferred_element_type=F32)
        act = _silu(a) * b
        acc_scr[...] += jnp.dot(act.astype(BF16), w2_ref[cs, :].astype(BF16), preferred_element_type=F32)

    @pl.when(j == pl.num_programs(1) - 1)
    def _():
        z = ALPHA * x_ref[...] + gate_ref[pl.ds(g, 1), :] * acc_scr[...]
        o_ref[...] = _layer_norm(z, lng_ref[...], lnb_ref[...])


def _ffn(x, mod, ln_g, ln_b, w1, w3, w2):
    tm = 1024
    ff = FF_CHUNK
    n_j = D_FF // ff
    w_in_spec = pl.BlockSpec((D_MODEL, ff), lambda i, j: (0, j))
    w_out_spec = pl.BlockSpec((ff, D_MODEL), lambda i, j: (j, 0))
    mod_spec = lambda col: pl.BlockSpec((MOD_ROWS, D_MODEL), lambda i, j: (0, col))
    const = lambda shape: pl.BlockSpec(shape, lambda i, j: (0, 0))
    in_specs = [pl.BlockSpec((tm, D_MODEL), lambda i, j: (i, 0)),
                mod_spec(3), mod_spec(4), mod_spec(5),
                const((1, D_MODEL)), const((1, D_MODEL)),
                w_in_spec, w_in_spec, w_out_spec]
    args = [x, mod, mod, mod, ln_g.reshape(1, -1), ln_b.reshape(1, -1), w1, w3, w2]
    return pl.pallas_call(
        functools.partial(_ffn_kernel, tm=tm, ff=ff),
        grid=(T_ALL // tm, n_j),
        in_specs=in_specs,
        out_specs=pl.BlockSpec((tm, D_MODEL), lambda i, j: (i, 0)),
        out_shape=jax.ShapeDtypeStruct((T_ALL, D_MODEL), F32),
        scratch_shapes=[pltpu.VMEM((tm, D_MODEL), BF16), pltpu.VMEM((tm, D_MODEL), F32)],
        compiler_params=_cparams("parallel", "arbitrary"),
        name="ffn",
    )(*args)


def kernel(x_prompt, x_sample, c, cache_k_b, cache_v_b, cache_k_c, cache_v_c, state_ret, c_ctx, ada_w, ada_b, ln_g, ln_b, w_in_ab, hy_conv_w, hy_conv_b, hf_w1, hf_b1, hf_freq, hf_w2, hf_b2, hf_w3, hy_skip, sink_b, w_out_ab, ffn_w1, ffn_w3, ffn_w2, w_in_cd, qn_g, kn_g, ret_decay, w_out_cd, moe_router, moe_w1, moe_w3, moe_w2):
    x = jnp.concatenate([x_prompt.reshape(T_PROMPT, D_MODEL), x_sample.reshape(T_SAMPLE, D_MODEL)], 0)
    cvec = jnp.concatenate([c_ctx[None], c, jnp.zeros((MOD_ROWS - 1 - DEC_BATCH, D_MODEL), F32)], 0)
    mod = _modulation(cvec, ada_w, ada_b)
    kw = N_KV * HEAD_DIM
    prompt = dict(row0=0, n_seq=BATCH, l=SEQ)
    sample = dict(row0=T_PROMPT, n_seq=DEC_BATCH, l=DEC_SEQ)

    u = _inproj(x, mod[0], w_in_ab[0], tn=1152)
    filt_args = (hf_w1[0], hf_b1[0], hf_freq[0], hf_w2[0], hf_b2[0], hf_w3[0])
    hy_args = (hy_conv_w[0], hy_conv_b[0], hy_skip[0])
    blank = lambda w: jnp.zeros((T_ALL, w), BF16)
    ya = _hyena(u, 0, BATCH, SEQ, *hy_args, _hyena_filter(SEQ, *filt_args), into=blank(HY_W))
    ya = _hyena(u, T_PROMPT // DEC_SEQ, DEC_BATCH, DEC_SEQ, *hy_args, _hyena_filter(DEC_SEQ, *filt_args), into=ya)
    cols_b = dict(q_col=AB_Q, k_col=AB_K, v_col=AB_V)
    ctx_b = (cache_k_b[:, 0].reshape(DEC_BATCH, PAST_LEN, kw), cache_v_b[:, 0].reshape(DEC_BATCH, PAST_LEN, kw))
    yb, = _attention(u, **prompt, **cols_b, qb=SEQ, sink=sink_b[0], into=blank(N_HEADS * HEAD_DIM))
    yb, = _attention(u, **sample, **cols_b, sink=sink_b[0], ctx=ctx_b, rope=True, band=True, into=yb)
    k_b = u[:T_PROMPT, AB_K:AB_V].reshape(BATCH, 1, SEQ, N_KV, HEAD_DIM)
    v_b = u[:T_PROMPT, AB_V:IN_AB].reshape(BATCH, 1, SEQ, N_KV, HEAD_DIM)
    x = _outproj(ya, yb, w_out_ab[0], x, mod[0], ln_g[0, 0], ln_b[0, 0])
    x = _ffn(x, mod[0], ln_g[0, 1], ln_b[0, 1], ffn_w1[0], ffn_w3[0], ffn_w2[0])

    u = _inproj(x, mod[1], w_in_cd[0], tn=1664)
    cols_c = dict(q_col=CD_Q, k_col=CD_K, v_col=CD_V)
    gains = (qn_g[0], kn_g[0])
    ctx_c = (cache_k_c[:, 0].reshape(DEC_BATCH, PAST_LEN, kw), cache_v_c[:, 0].reshape(DEC_BATCH, PAST_LEN, kw))
    yc, k_c = _attention(u, **prompt, **cols_c, qb=SEQ, qk_gain=gains, emit_k=True, into=blank(N_HEADS * HEAD_DIM))
    yc, = _attention(u, **sample, **cols_c, qk_gain=gains, ctx=ctx_c, rope=True, into=yc)
    yd, s_r = _retention(u, ret_decay[0], **prompt, emit_state=True, into=blank(RET_HEADS * RET_D))
    yd, = _retention(u, ret_decay[0], **sample, rope=True, s0=state_ret[:, 0], into=yd)
    k_c = k_c.reshape(BATCH, 1, SEQ, N_KV, HEAD_DIM)
    v_c = u[:T_PROMPT, CD_V:CD_RQ].reshape(BATCH, 1, SEQ, N_KV, HEAD_DIM)
    x = _outproj(yc, yd, w_out_cd[0], x, mod[1], ln_g[1, 0], ln_b[1, 0])
    y_prompt, y_sample = _moe(x, mod[1], ln_g[1, 1], ln_b[1, 1], moe_router[0], moe_w1[0], moe_w3[0], moe_w2[0])

    return (y_prompt.reshape(BATCH, SEQ, D_MODEL), y_sample.reshape(DEC_BATCH, DEC_SEQ, D_MODEL),
            k_b, v_b, k_c, v_c, s_r[:, None])
```

```python
import functools
import math

import numpy as np
import jax
import jax.numpy as jnp
from jax import lax
from jax.experimental import pallas as pl
from jax.experimental.pallas import tpu as pltpu

F32 = jnp.float32
BF16 = jnp.bfloat16

D_MODEL = 1024
BATCH = 16
SEQ = 256
DEC_BATCH = 2
DEC_SEQ = 1024
PAST_LEN = 512
GRID_W = 64
HEAD_DIM = 64
BLOCK = 128
HY_W = 512
POS_BANDS = 16
POS_EMB = 1 + 2 * POS_BANDS
FILT_HID = 64
HY_FAST_DECAY = 0.3
HY_SLOW_DECAY = 1.5
HY_TARGET = 1e-2
N_HEADS = 8
N_KV = 2
GROUPS = N_HEADS // N_KV
RET_HEADS = 4
RET_D = 128
CHUNK = 128
ROPE_BASE = 10000.0
D_FF = 2816
N_EXPERTS = 8
EXPERT_FF = 1408
DEPTH = 2
ALPHA = (2 * DEPTH) ** 0.25
EPS = 1e-6

T_PROMPT = BATCH * SEQ
T_SAMPLE = DEC_BATCH * DEC_SEQ
T_ALL = T_PROMPT + T_SAMPLE
GROUP_ROWS = 1024
N_PROMPT_GROUPS = T_PROMPT // GROUP_ROWS
MOD_ROWS = 16
LANES = 128
VMEM_LIMIT = 58 * 1024 * 1024

AB_Q = 3 * HY_W
AB_K = AB_Q + N_HEADS * HEAD_DIM
AB_V = AB_K + N_KV * HEAD_DIM
IN_AB = AB_V + N_KV * HEAD_DIM
CD_Q = 0
CD_K = N_HEADS * HEAD_DIM
CD_V = CD_K + N_KV * HEAD_DIM
CD_RQ = CD_V + N_KV * HEAD_DIM
CD_RK = CD_RQ + RET_HEADS * RET_D
CD_RV = CD_RK + RET_HEADS * RET_D
CD_GF = CD_RV + RET_HEADS * RET_D
CD_GB = CD_GF + RET_HEADS * RET_D
IN_CD = CD_GB + RET_HEADS * RET_D


def _cparams(*sem):
    return pltpu.CompilerParams(dimension_semantics=sem, vmem_limit_bytes=VMEM_LIMIT)


def _silu(x):
    return x * jax.nn.sigmoid(x)


def _bdot(a, b):
    return jnp.dot(a.astype(BF16), b.astype(BF16), preferred_element_type=F32)


def _bdot_nt(a, b):
    return lax.dot_general(a.astype(BF16), b.astype(BF16), (((1,), (1,)), ((), ())),
                           preferred_element_type=F32)


def _bdot_tn(a, b):
    return lax.dot_general(a.astype(BF16), b.astype(BF16), (((0,), (0,)), ((), ())),
                           preferred_element_type=F32)


def _layer_norm(z, g, b):
    mu = jnp.mean(z, -1, keepdims=True)
    zc = z - mu
    var = jnp.mean(zc * zc, -1, keepdims=True)
    return zc * lax.rsqrt(var + EPS) * g + b


def _fill_rows(kern, in_specs, args, into):
    n_in = len(args)
    in_specs.append(pl.BlockSpec(memory_space=pl.ANY))
    args.append(into)

    def kern_into(*refs):
        return kern(*refs[:n_in], *refs[n_in + 1:])

    return kern_into, {n_in: 0}


def _group_of_tile(i, tm):
    return jnp.maximum(i // (GROUP_ROWS // tm) - (N_PROMPT_GROUPS - 1), 0)


def _rope_tables(n_tokens, d, reps):
    nf = d // 4
    inv = ROPE_BASE ** (-np.arange(nf, dtype=np.float64) / nf)
    pos = np.arange(n_tokens)
    row, col = pos // GRID_W, pos % GRID_W
    ang_r = row[:, None] * inv[None, :]
    ang_c = col[:, None] * inv[None, :]
    zeros = np.zeros_like(ang_r)
    cos = np.concatenate([np.cos(ang_r), np.cos(ang_r), np.cos(ang_c), np.cos(ang_c)], -1)
    sin_a = np.concatenate([-np.sin(ang_r), zeros, -np.sin(ang_c), zeros], -1)
    sin_b = np.concatenate([zeros, np.sin(ang_r), zeros, np.sin(ang_c)], -1)
    tile = lambda a: jnp.asarray(np.tile(a, (1, reps)), F32)
    return tile(cos), tile(sin_a), tile(sin_b)


def _dft_mats(l):
    n = 2 * l
    k = np.arange(l, dtype=np.float64)
    ang = 2.0 * np.pi * np.outer(k, k) / n
    fc = np.cos(ang)
    fs = np.sin(ang)
    fs[0, :] = np.cos(np.pi * k)
    fwd = np.concatenate([fc, fs], 0)
    wk = np.full((l,), 2.0)
    wk[0] = 1.0
    inv = np.concatenate([fc.T * wk[None, :], fs.T * wk[None, :]], 1)
    return jnp.asarray(fwd, F32), jnp.asarray(inv, F32)


def _filter_consts(l):
    t = np.linspace(0.0, 1.0, l, dtype=np.float32).astype(np.float64)[:, None]
    w = (2.0 * math.pi * np.arange(l, dtype=np.float64) / l)[:, None]
    bands = np.linspace(1e-4, POS_BANDS - 1.0, POS_BANDS, dtype=np.float32).astype(np.float64)[None, :]
    z = np.concatenate([t, np.cos(bands * w), -np.sin(bands * w)], -1)
    z = np.pad(z, ((0, 0), (0, LANES - POS_EMB)))
    max_decay = math.log(HY_TARGET) / HY_FAST_DECAY
    min_decay = math.log(HY_TARGET) / HY_SLOW_DECAY
    deltas = np.linspace(min_decay, max_decay, HY_W, dtype=np.float32).astype(np.float64)
    window = np.exp(-t * np.abs(deltas)[None, :])
    return jnp.asarray(z, F32), jnp.asarray(window, F32)


def _mod_kernel(c_ref, w_ref, b_ref, o_ref):
    o_ref[...] = _bdot(_silu(c_ref[...]), w_ref[...]) + b_ref[...]


def _modulation(cvec, ada_w, ada_b):
    tn = 1536
    n = ada_w.shape[-1]
    return pl.pallas_call(
        _mod_kernel,
        grid=(DEPTH, n // tn),
        in_specs=[pl.BlockSpec((MOD_ROWS, D_MODEL), lambda l, j: (0, 0)),
                  pl.BlockSpec((None, D_MODEL, tn), lambda l, j: (l, 0, j)),
                  pl.BlockSpec((None, 1, tn), lambda l, j: (l, 0, j))],
        out_specs=pl.BlockSpec((None, MOD_ROWS, tn), lambda l, j: (l, 0, j)),
        out_shape=jax.ShapeDtypeStruct((DEPTH, MOD_ROWS, n), F32),
        compiler_params=_cparams("parallel", "parallel"),
        name="mod",
    )(cvec, ada_w, ada_b.reshape(DEPTH, 1, n))


def _slab_tiles(xs, tm):
    ends = np.cumsum([x.shape[0] // tm for x in xs])
    return [(int(e - x.shape[0] // tm), int(e)) for x, e in zip(xs, ends)]


def _inproj_kernel(*refs, tm, slabs):
    x_refs = refs[:len(slabs)]
    sh_ref, sc_ref, w_ref, o_ref, h_scr, w_scr = refs[len(slabs):]
    j, i = pl.program_id(0), pl.program_id(1)
    rows = pl.ds(pl.multiple_of(i * tm, tm), tm)
    g = _group_of_tile(i, tm)

    for x_ref, (first, end) in zip(x_refs, slabs):
        @pl.when((j == 0) & (i >= first) & (i < end))
        def _():
            sc = sc_ref[pl.ds(g, 1), :]
            sh = sh_ref[pl.ds(g, 1), :]
            h_scr[rows, :] = (x_ref[...] * (1.0 + sc) + sh).astype(BF16)

    @pl.when(i == 0)
    def _():
        w_scr[...] = w_ref[...].astype(BF16)

    o_ref[...] = jnp.dot(h_scr[rows, :], w_scr[...], preferred_element_type=F32)


def _inproj(xs, mod, w, tn):
    tm = 1024
    n = w.shape[1]
    slabs = _slab_tiles(xs, tm)

    def slab_spec(first, end):
        last = end - first - 1
        return pl.BlockSpec((tm, D_MODEL), lambda j, i: (jnp.where(j == 0, jnp.clip(i - first, 0, last), last), 0))

    return pl.pallas_call(
        functools.partial(_inproj_kernel, tm=tm, slabs=slabs),
        grid=(n // tn, T_ALL // tm),
        in_specs=[slab_spec(*s) for s in slabs]
        + [pl.BlockSpec((MOD_ROWS, D_MODEL), lambda j, i: (0, 0)),
           pl.BlockSpec((MOD_ROWS, D_MODEL), lambda j, i: (0, 1)),
           pl.BlockSpec((D_MODEL, tn), lambda j, i: (0, j))],
        out_specs=pl.BlockSpec((tm, tn), lambda j, i: (i, j)),
        out_shape=jax.ShapeDtypeStruct((T_ALL, n), F32),
        scratch_shapes=[pltpu.VMEM((T_ALL, D_MODEL), BF16), pltpu.VMEM((D_MODEL, tn), BF16)],
        compiler_params=_cparams("arbitrary", "arbitrary"),
        name="inproj",
    )(*xs, mod, mod, w)


def _hyena_filter_kernel(z_ref, w1_ref, b1_ref, fr_ref, w2_ref, b2_ref, w3_ref, win_ref, fwd32_ref, inv32_ref,
                         hc_ref, hs_ref, hc2_ref, fwd_ref, inv_ref, *, l):
    hi = lax.Precision.HIGHEST
    fwd = fwd32_ref[...].astype(BF16)
    fwd_ref[...] = fwd
    inv_ref[...] = inv32_ref[...].astype(BF16)
    fr = fr_ref[...]
    h = jnp.sin(fr * (jnp.dot(z_ref[...], w1_ref[...], precision=hi, preferred_element_type=F32) + b1_ref[...]))
    h = jnp.sin(fr * (jnp.dot(h, w2_ref[...], precision=hi, preferred_element_type=F32) + b2_ref[...]))
    h = jnp.dot(h, w3_ref[...], precision=hi, preferred_element_type=F32)
    win = win_ref[...]
    hf = h[:, :HY_W] * win
    hb = h[:, HY_W:] * win
    p = _bdot(fwd, hf + hb)
    q = _bdot(fwd, hf - hb)
    row0 = lax.broadcasted_iota(jnp.int32, (l, 1), 0) == 0
    hc = p[:l]
    hc_ref[...] = hc
    hs_ref[...] = jnp.where(row0, 0.0, q[l:])
    hc2_ref[...] = jnp.where(row0, p[l:l + 1], hc)


def _hyena_filter(l, fw1, fb1, ffreq, fw2, fb2, fw3):
    z, window = _filter_consts(l)
    fwd, inv = _dft_mats(l)
    pad_c = LANES - FILT_HID
    w1 = jnp.pad(fw1, ((0, LANES - POS_EMB), (0, pad_c)))
    w2 = jnp.pad(fw2, ((0, pad_c), (0, pad_c)))
    w3 = jnp.pad(fw3, ((0, pad_c), (0, 0)))
    row = lambda a: jnp.pad(a, (0, pad_c)).reshape(1, LANES)
    shp = jax.ShapeDtypeStruct((l, HY_W), F32)
    return pl.pallas_call(
        functools.partial(_hyena_filter_kernel, l=l),
        out_shape=(shp, shp, shp, jax.ShapeDtypeStruct(fwd.shape, BF16), jax.ShapeDtypeStruct(inv.shape, BF16)),
        compiler_params=pltpu.CompilerParams(vmem_limit_bytes=VMEM_LIMIT),
        name=f"hyena_filter_{l}",
    )(z, w1, row(fb1), row(ffreq), w2, row(fb2), w3, window, fwd, inv)


def _hyena_kernel(u_ref, cw_ref, cb_ref, skip_ref, fwd_ref, inv_ref, hc_ref, hs_ref, hc2_ref, o_ref, *, l):
    u = u_ref[...]
    rows = lax.broadcasted_iota(jnp.int32, (l, 1), 0)
    prev = jnp.where(rows == 0, 0.0, pltpu.roll(u, 1, 0))
    nxt = jnp.where(rows == l - 1, 0.0, pltpu.roll(u, l - 1, 0))
    uc = prev * cw_ref[0:1, :] + u * cw_ref[1:2, :] + nxt * cw_ref[2:3, :] + cb_ref[...]
    x0 = uc[:, :HY_W]
    x1 = uc[:, HY_W:2 * HY_W]
    v = uc[:, 2 * HY_W:] * x1
    ab = _bdot(fwd_ref[...], v)
    a, b = ab[:l], ab[l:]
    hs = hs_ref[...]
    re = a * hc_ref[...] - b * hs
    im = a * hs + b * hc2_ref[...]
    y = _bdot(inv_ref[...], jnp.concatenate([re, im], 0)) * (1.0 / (2 * l))
    o_ref[...] = ((y + skip_ref[...] * v) * x0).astype(o_ref.dtype)


def _hyena(u_all, row_block0, n_seq, l, conv_w, conv_b, skip, filt, into):
    hc, hs, hc2, fwd, inv = filt
    const = lambda shape: pl.BlockSpec(shape, lambda s: (0, 0))
    in_specs = [pl.BlockSpec((l, 3 * HY_W), lambda s: (row_block0 + s, 0)),
                const((3, 3 * HY_W)), const((1, 3 * HY_W)), const((1, HY_W)),
                const((2 * l, l)), const((l, 2 * l)),
                const((l, HY_W)), const((l, HY_W)), const((l, HY_W))]
    args = [u_all, conv_w, conv_b.reshape(1, -1), skip.reshape(1, -1), fwd, inv, hc, hs, hc2]
    kern, alias = _fill_rows(functools.partial(_hyena_kernel, l=l), in_specs, args, into)
    return pl.pallas_call(
        kern,
        grid=(n_seq,),
        in_specs=in_specs,
        out_specs=pl.BlockSpec((l, HY_W), lambda s: (row_block0 + s, 0)),
        out_shape=jax.ShapeDtypeStruct((T_ALL, HY_W), BF16),
        input_output_aliases=alias,
        compiler_params=_cparams("parallel"),
        name=f"hyena_{l}",
    )(*args)


def _seg_rms_norm(x, bd_ref, g):
    sq = x * x
    hi = sq.astype(BF16)
    lo = (sq - hi.astype(F32)).astype(BF16)
    bd = bd_ref[...]
    ss = (jnp.dot(hi, bd, preferred_element_type=F32) + jnp.dot(lo, bd, preferred_element_type=F32))
    return x * lax.rsqrt(ss * (1.0 / HEAD_DIM) + EPS) * g


def _rope(x, cos, sin_a, sin_b, quarter):
    w = x.shape[-1]
    return x * cos + pltpu.roll(x, w - quarter, 1) * sin_a + pltpu.roll(x, quarter, 1) * sin_b


def _attn_kernel(*refs, l, qb, lc, rope, qknorm, band, has_sink, emit_k):
    it = iter(refs)
    q_ref, k_ref, v_ref = next(it), next(it), next(it)
    if lc:
        ck_ref, cv_ref = next(it), next(it)
    if rope:
        cq_ref, saq_ref, sbq_ref = next(it), next(it), next(it)
        ckk_ref, sak_ref, sbk_ref = next(it), next(it), next(it)
    if qknorm:
        qg_ref, kg_ref, bdq_ref, bdk_ref = next(it), next(it), next(it), next(it)
    if has_sink:
        sink_ref = next(it)
    o_ref = next(it)
    if emit_k:
        kout_ref = next(it)
    kp_scr = next(it)

    qi = pl.program_id(1)

    @pl.when(qi == 0)
    def _():
        k = k_ref[...]
        if qknorm:
            k = _seg_rms_norm(k, bdk_ref, kg_ref[...])
        if emit_k:
            kout_ref[...] = k
        if rope:
            k = _rope(k, ckk_ref[...], sak_ref[...], sbk_ref[...], HEAD_DIM // 4)
        for kv in range(N_KV):
            kp_scr[kv] = k[:, kv * HEAD_DIM:(kv + 1) * HEAD_DIM].astype(BF16)

    q = q_ref[...]
    if qknorm:
        q = _seg_rms_norm(q, bdq_ref, qg_ref[...])
    if rope:
        q = _rope(q, cq_ref[...], saq_ref[...], sbq_ref[...], HEAD_DIM // 4)
    q = q * HEAD_DIM ** -0.5

    rows = GROUPS * qb
    if band:
        n_loc = 3 * BLOCK
        start = pl.multiple_of(jnp.clip((qi - 1) * BLOCK, 0, l - n_loc), BLOCK)
        keys = pl.ds(start, n_loc)
        tq = qi * qb + lax.broadcasted_iota(jnp.int32, (rows, 1), 0) % qb
        tk = start + lax.broadcasted_iota(jnp.int32, (1, n_loc), 1)
        valid = jnp.abs(tq - tk) <= BLOCK
    else:
        keys = slice(None)

    outs = []
    for kv in range(N_KV):
        lanes = slice(kv * HEAD_DIM, (kv + 1) * HEAD_DIM)
        qs = jnp.concatenate([q[:, (kv * GROUPS + g) * HEAD_DIM:(kv * GROUPS + g + 1) * HEAD_DIM]
                              for g in range(GROUPS)], 0)
        s = _bdot_nt(qs, kp_scr[kv, keys, :])
        if band:
            s = jnp.where(valid, s, -jnp.inf)
        if lc:
            s_c = _bdot_nt(qs, ck_ref[:, lanes])
        e_parts, ec_parts, dens = [], [], []
        for g in range(GROUPS):
            r = slice(g * qb, (g + 1) * qb)
            m = jnp.max(s[r], -1, keepdims=True)
            if lc:
                m = jnp.maximum(m, jnp.max(s_c[r], -1, keepdims=True))
            if has_sink:
                sink = sink_ref[kv * GROUPS + g]
                m = jnp.maximum(m, sink)
            e = jnp.exp(s[r] - m)
            den = jnp.sum(e, -1, keepdims=True)
            e_parts.append(e)
            if lc:
                e_c = jnp.exp(s_c[r] - m)
                den = den + jnp.sum(e_c, -1, keepdims=True)
                ec_parts.append(e_c)
            if has_sink:
                den = den + jnp.exp(sink - m)
            dens.append(den)
        o = _bdot(jnp.concatenate(e_parts, 0), v_ref[keys, lanes])
        if lc:
            o = o + _bdot(jnp.concatenate(ec_parts, 0), cv_ref[:, lanes])
        outs.extend(o[g * qb:(g + 1) * qb] / dens[g] for g in range(GROUPS))
    o_ref[...] = jnp.concatenate(outs, 1).astype(o_ref.dtype)


def _attention(u, *, row0, n_seq, l, q_col, k_col, v_col, into, qb=BLOCK, ctx=None, rope=False, qk_gain=None,
               band=False, sink=None, emit_k=False):
    nq = l // qb
    qw = N_HEADS * HEAD_DIM
    kw = N_KV * HEAD_DIM
    qb0, sb0 = row0 // qb, row0 // l
    seq_spec = lambda col: pl.BlockSpec((l, kw), lambda b, i: (sb0 + b, col // kw))
    const = lambda shape: pl.BlockSpec(shape, lambda b, i: (0, 0))
    in_specs = [pl.BlockSpec((qb, qw), lambda b, i: (qb0 + b * nq + i, q_col // qw)),
                seq_spec(k_col), seq_spec(v_col)]
    args = [u, u, u]
    lc = 0
    if ctx is not None:
        lc = ctx[0].shape[1]
        in_specs += [pl.BlockSpec((None, lc, kw), lambda b, i: (b, 0, 0))] * 2
        args += list(ctx)
    if rope:
        tabs = _rope_tables(l, HEAD_DIM, N_HEADS)
        in_specs += [pl.BlockSpec((qb, qw), lambda b, i: (i, 0))] * 3 + [const((l, kw))] * 3
        args += list(tabs) + list(tabs)
    if qk_gain is not None:
        bd = np.kron(np.eye(N_HEADS), np.ones((HEAD_DIM, HEAD_DIM)))
        in_specs += [const((1, qw)), const((1, kw)), const((qw, qw)), const((kw, kw))]
        args += [jnp.tile(qk_gain[0], N_HEADS).reshape(1, qw), jnp.tile(qk_gain[1], N_KV).reshape(1, kw),
                 jnp.asarray(bd, BF16), jnp.asarray(bd[:kw, :kw], BF16)]
    if sink is not None:
        in_specs.append(pl.BlockSpec(memory_space=pltpu.SMEM))
        args.append(sink)
    out_specs = [pl.BlockSpec((qb, qw), lambda b, i: (qb0 + b * nq + i, 0))]
    out_shape = [jax.ShapeDtypeStruct((T_ALL, qw), BF16)]
    if emit_k:
        out_specs.append(pl.BlockSpec((l, kw), lambda b, i: (b, 0)))
        out_shape.append(jax.ShapeDtypeStruct((n_seq * l, kw), F32))
    kern = functools.partial(_attn_kernel, l=l, qb=qb, lc=lc, rope=rope, qknorm=qk_gain is not None, band=band,
                             has_sink=sink is not None, emit_k=emit_k)
    kern, alias = _fill_rows(kern, in_specs, args, into)
    return pl.pallas_call(
        kern,
        grid=(n_seq, nq),
        in_specs=in_specs,
        out_specs=out_specs,
        out_shape=out_shape,
        input_output_aliases=alias,
        scratch_shapes=[pltpu.VMEM((N_KV, l, HEAD_DIM), BF16)],
        compiler_params=_cparams("parallel", "arbitrary"),
        name=f"attn_{l}_{'b' if sink is not None else 'c'}",
    )(*args)


def _ret_kernel(*refs, l, nb, rope, has_s0, emit_state):
    it = iter(refs)
    dec_ref = next(it)
    rq_ref, rk_ref, rv_ref, gf_ref, gb_ref = next(it), next(it), next(it), next(it), next(it)
    if rope:
        cos_ref, sa_ref, sb_ref = next(it), next(it), next(it)
    if has_s0:
        s0_ref = next(it)
    o_ref = next(it)
    if emit_state:
        st_ref = next(it)

    h = pl.program_id(1)
    qs, ks, vs = [], [], []
    for s in range(nb):
        seq = slice(s * l, (s + 1) * l)
        q = rq_ref[seq, :] * RET_D ** -0.5
        k = rk_ref[seq, :]
        if rope:
            q = _rope(q, cos_ref[...], sa_ref[...], sb_ref[...], RET_D // 4)
            k = _rope(k, cos_ref[...], sa_ref[...], sb_ref[...], RET_D // 4)
        qs.append(q)
        ks.append(k)
        vs.append(rv_ref[seq, :])
    n = l // CHUNK
    ii = lax.broadcasted_iota(jnp.int32, (CHUNK, 1), 0).astype(F32)
    jj = lax.broadcasted_iota(jnp.int32, (1, CHUNK), 1).astype(F32)
    diff = ii - jj
    ys = [None] * nb
    for d in range(2):
        log_g = jnp.log(jax.nn.sigmoid(jnp.full((1, 1), dec_ref[d, h], F32)))
        if d == 0:
            mask = jnp.exp(jnp.where(diff >= 0, diff * log_g, -jnp.inf))
            q_dec = jnp.exp((ii + 1.0) * log_g)
            k_dec = jnp.exp((CHUNK - 1.0 - ii) * log_g)
            order = range(n)
        else:
            mask = jnp.exp(jnp.where(diff <= 0, -diff * log_g, -jnp.inf))
            q_dec = jnp.exp((CHUNK - ii) * log_g)
            k_dec = jnp.exp(ii * log_g)
            order = reversed(range(n))
        c_dec = jnp.exp(CHUNK * log_g)
        order = list(order)
        for s in range(nb):
            seq = slice(s * l, (s + 1) * l)
            q, k, v = qs[s], ks[s], vs[s]
            state = s0_ref[s, d] if has_s0 else jnp.zeros((RET_D, RET_D), F32)
            o_chunks = [None] * n
            for c in order:
                sl = slice(c * CHUNK, (c + 1) * CHUNK)
                qc, kc, vc = q[sl], k[sl], v[sl]
                inner = _bdot_nt(qc, kc) * mask
                o_chunks[c] = _bdot(inner, vc) + _bdot(qc * q_dec, state)
                state = state * c_dec + _bdot_tn(kc * k_dec, vc)
            if emit_state:
                st_ref[s, d] = state
            o = jnp.concatenate(o_chunks, 0)
            o = o * lax.rsqrt(jnp.mean(o * o, -1, keepdims=True) + EPS)
            gate = _silu((gf_ref if d == 0 else gb_ref)[seq, :])
            ys[s] = gate * o if ys[s] is None else ys[s] + gate * o
    for s in range(nb):
        o_ref[s * l:(s + 1) * l, :] = ys[s].astype(o_ref.dtype)


def _retention(u, ret_decay, *, row0, n_seq, l, into, nb=1, rope=False, s0=None, emit_state=False):
    sb0 = row0 // (nb * l)
    col = lambda c0: pl.BlockSpec((nb * l, RET_D), lambda b, h: (sb0 + b, c0 // RET_D + h))
    in_specs = [pl.BlockSpec(memory_space=pltpu.SMEM),
                col(CD_RQ), col(CD_RK), col(CD_RV), col(CD_GF), col(CD_GB)]
    args = [ret_decay, u, u, u, u, u]
    if rope:
        in_specs += [pl.BlockSpec((l, RET_D), lambda b, h: (0, 0))] * 3
        args += list(_rope_tables(l, RET_D, 1))
    state_spec = pl.BlockSpec((nb, 2, None, RET_D, RET_D), lambda b, h: (b, 0, h, 0, 0))
    if s0 is not None:
        in_specs.append(state_spec)
        args.append(s0)
    out_specs = [pl.BlockSpec((nb * l, RET_D), lambda b, h: (sb0 + b, h))]
    out_shape = [jax.ShapeDtypeStruct((T_ALL, RET_HEADS * RET_D), BF16)]
    if emit_state:
        out_specs.append(state_spec)
        out_shape.append(jax.ShapeDtypeStruct((n_seq, 2, RET_HEADS, RET_D, RET_D), F32))
    kern = functools.partial(_ret_kernel, l=l, nb=nb, rope=rope, has_s0=s0 is not None, emit_state=emit_state)
    kern, alias = _fill_rows(kern, in_specs, args, into)
    return pl.pallas_call(
        kern,
        grid=(n_seq // nb, RET_HEADS),
        in_specs=in_specs,
        out_specs=out_specs,
        out_shape=out_shape,
        input_output_aliases=alias,
        compiler_params=_cparams("parallel", "parallel"),
        name=f"retention_{l}",
    )(*args)


def _outproj_kernel(*refs, tm, slabs):
    ya_ref, yb_ref, w_ref = refs[:3]
    x_refs = refs[3:3 + len(slabs)]
    gate_ref, lng_ref, lnb_ref, o_ref, w_scr = refs[3 + len(slabs):]
    i = pl.program_id(0)

    @pl.when(i == 0)
    def _():
        w_scr[...] = w_ref[...].astype(BF16)

    half = ya_ref.shape[1]
    m = (jnp.dot(ya_ref[...], w_scr[:half], preferred_element_type=F32)
         + jnp.dot(yb_ref[...], w_scr[half:], preferred_element_type=F32))
    gm = gate_ref[pl.ds(_group_of_tile(i, tm), 1), :] * m

    for x_ref, (first, end) in zip(x_refs, slabs):
        @pl.when((i >= first) & (i < end))
        def _():
            o_ref[...] = _layer_norm(ALPHA * x_ref[...] + gm, lng_ref[...], lnb_ref[...])


def _outproj(ya, yb, w, xs, mod, ln_g, ln_b):
    tm = 512
    half = ya.shape[1]
    slabs = _slab_tiles(xs, tm)
    const = lambda shape: pl.BlockSpec(shape, lambda i: (0, 0))

    def slab_spec(first, end):
        return pl.BlockSpec((tm, D_MODEL), lambda i: (jnp.clip(i - first, 0, end - first - 1), 0))

    return pl.pallas_call(
        functools.partial(_outproj_kernel, tm=tm, slabs=slabs),
        grid=(T_ALL // tm,),
        in_specs=[pl.BlockSpec((tm, half), lambda i: (i, 0)),
                  pl.BlockSpec((tm, half), lambda i: (i, 0)),
                  const((2 * half, D_MODEL))]
        + [slab_spec(*s) for s in slabs]
        + [pl.BlockSpec((MOD_ROWS, D_MODEL), lambda i: (0, 2)),
           const((1, D_MODEL)), const((1, D_MODEL))],
        out_specs=pl.BlockSpec((tm, D_MODEL), lambda i: (i, 0)),
        out_shape=jax.ShapeDtypeStruct((T_ALL, D_MODEL), F32),
        scratch_shapes=[pltpu.VMEM((2 * half, D_MODEL), BF16)],
        compiler_params=_cparams("arbitrary"),
        name="outproj_ln",
    )(ya, yb, w, *xs, mod, ln_g.reshape(1, -1), ln_b.reshape(1, -1))


TOK_TILE = 256
N_TOK_TILES = T_ALL // TOK_TILE
SORT_TILE = 256
EXP_TILE = 512
N_EXP_TILES = (2 * T_ALL) // EXP_TILE + N_EXPERTS
N_SORT_TILES = N_EXP_TILES * (EXP_TILE // SORT_TILE)
N_PROMPT_TOK_TILES = T_PROMPT // TOK_TILE
CUM_ROWS = 32


def _route_kernel(x_ref, sh_ref, sc_ref, r_ref, h_ref, rank_ref, rank_t_ref, gate_t_ref, cum_ref,
                  carry_row, carry_col):
    c = pl.program_id(0)

    @pl.when(c == 0)
    def _():
        carry_row[...] = jnp.zeros_like(carry_row)
        carry_col[...] = jnp.zeros_like(carry_col)
        cum_ref[...] = jnp.zeros_like(cum_ref)

    g = _group_of_tile(c, TOK_TILE)
    h = x_ref[...] * (1.0 + sc_ref[pl.ds(g, 1), :]) + sh_ref[pl.ds(g, 1), :]
    h_ref[...] = h.astype(BF16)
    logits = jnp.dot(h, r_ref[...], precision=lax.Precision.HIGHEST, preferred_element_type=F32)
    lane = lax.broadcasted_iota(jnp.int32, logits.shape, 1)
    logits = jnp.where(lane < N_EXPERTS, logits, -jnp.inf)
    m1 = jnp.max(logits, -1, keepdims=True)
    i1 = jnp.min(jnp.where(logits == m1, lane, LANES), -1, keepdims=True)
    rest = jnp.where(lane == i1, -jnp.inf, logits)
    m2 = jnp.max(rest, -1, keepdims=True)
    i2 = jnp.min(jnp.where(rest == m2, lane, LANES), -1, keepdims=True)
    e2 = jnp.exp(m2 - m1)
    den = 1.0 + e2
    gates = jnp.where(lane == i1, 1.0 / den, 0.0) + jnp.where(lane == i2, e2 / den, 0.0)
    sel = jnp.where((lane == i1) | (lane == i2), 1.0, 0.0)
    sel_t = sel.T
    ti = lax.broadcasted_iota(jnp.int32, (TOK_TILE, TOK_TILE), 0)
    tj = lax.broadcasted_iota(jnp.int32, (TOK_TILE, TOK_TILE), 1)
    before = jnp.where(tj < ti, 1.0, 0.0).astype(BF16)
    rank = jnp.dot(before, sel.astype(BF16), preferred_element_type=F32) + carry_row[...]
    rank_t = lax.dot_general(sel_t.astype(BF16), before, (((1,), (1,)), ((), ())),
                             preferred_element_type=F32) + carry_col[...]
    rank_ref[...] = jnp.where(sel > 0.0, rank, -1.0)
    rank_t_ref[...] = jnp.where(sel_t > 0.0, rank_t, -1.0)[:N_EXPERTS]
    gate_t_ref[...] = gates.T[:N_EXPERTS]
    cum_ref[pl.ds(c, 1), :] = carry_row[...].astype(jnp.int32)
    carry_row[...] += jnp.sum(sel, 0, keepdims=True)
    carry_col[...] += jnp.sum(sel_t, 1, keepdims=True)

    @pl.when(c == N_TOK_TILES - 1)
    def _():
        cum_ref[pl.ds(N_TOK_TILES, 1), :] = carry_row[...].astype(jnp.int32)


def _route(x, mod, router):
    tile = lambda w: pl.BlockSpec((TOK_TILE, w), lambda c: (c, 0))
    tile_t = pl.BlockSpec((N_EXPERTS, TOK_TILE), lambda c: (0, c))
    return pl.pallas_call(
        _route_kernel,
        grid=(N_TOK_TILES,),
        in_specs=[tile(D_MODEL),
                  pl.BlockSpec((MOD_ROWS, D_MODEL), lambda c: (0, 3)),
                  pl.BlockSpec((MOD_ROWS, D_MODEL), lambda c: (0, 4)),
                  pl.BlockSpec((D_MODEL, LANES), lambda c: (0, 0))],
        out_specs=[tile(D_MODEL), tile(LANES), tile_t, tile_t,
                   pl.BlockSpec((CUM_ROWS, LANES), lambda c: (0, 0))],
        out_shape=[jax.ShapeDtypeStruct((T_ALL, D_MODEL), BF16),
                   jax.ShapeDtypeStruct((T_ALL, LANES), F32),
                   jax.ShapeDtypeStruct((N_EXPERTS, T_ALL), F32),
                   jax.ShapeDtypeStruct((N_EXPERTS, T_ALL), F32),
                   jax.ShapeDtypeStruct((CUM_ROWS, LANES), jnp.int32)],
        scratch_shapes=[pltpu.VMEM((1, LANES), F32), pltpu.VMEM((LANES, 1), F32)],
        compiler_params=_cparams("arbitrary"),
        name="route",
    )(x, mod, mod, jnp.pad(router, ((0, 0), (0, LANES - N_EXPERTS))))


def _gather_kernel(te_ref, off_ref, cum_ref, nt_ref, h_ref, rank_t_ref, gate_t_ref, xs_ref, gs_ref, acc_scr, g_scr):
    i = pl.program_id(0)
    e = te_ref[i]
    r0 = i * SORT_TILE - off_ref[e]
    acc_scr[...] = jnp.zeros_like(acc_scr)
    g_scr[...] = jnp.zeros_like(g_scr)
    want = (r0 + lax.broadcasted_iota(jnp.int32, (SORT_TILE, 1), 0)).astype(F32)
    for c in range(N_TOK_TILES):
        lo = cum_ref[c * N_EXPERTS + e]
        hi = cum_ref[(c + 1) * N_EXPERTS + e]

        @pl.when((i < nt_ref[0]) & (lo < r0 + SORT_TILE) & (hi > r0))
        def _():
            cols = slice(c * TOK_TILE, (c + 1) * TOK_TILE)
            pick = rank_t_ref[pl.ds(e, 1), cols] == want
            acc_scr[...] += jnp.dot(jnp.where(pick, 1.0, 0.0).astype(BF16), h_ref[cols, :],
                                    preferred_element_type=F32)
            g_scr[...] += jnp.sum(jnp.where(pick, gate_t_ref[pl.ds(e, 1), cols], 0.0), -1, keepdims=True)

    xs_ref[...] = acc_scr[...].astype(BF16)
    gs_ref[...] = jnp.broadcast_to(g_scr[...], gs_ref.shape)


def _gather(tile_expert, off, cum, n_tiles, h, rank_t, gate_t):
    const = lambda shape: pl.BlockSpec(shape, lambda i, *_: (0, 0))
    return pl.pallas_call(
        _gather_kernel,
        grid_spec=pltpu.PrefetchScalarGridSpec(
            num_scalar_prefetch=4,
            grid=(N_SORT_TILES,),
            in_specs=[const((T_ALL, D_MODEL)), const((N_EXPERTS, T_ALL)), const((N_EXPERTS, T_ALL))],
            out_specs=[pl.BlockSpec((SORT_TILE, D_MODEL), lambda i, *_: (i, 0)),
                       pl.BlockSpec((SORT_TILE, LANES), lambda i, *_: (i, 0))],
            scratch_shapes=[pltpu.VMEM((SORT_TILE, D_MODEL), F32), pltpu.VMEM((SORT_TILE, 1), F32)]),
        out_shape=[jax.ShapeDtypeStruct((N_SORT_TILES * SORT_TILE, D_MODEL), BF16),
                   jax.ShapeDtypeStruct((N_SORT_TILES * SORT_TILE, LANES), F32)],
        compiler_params=_cparams("parallel"),
        name="moe_gather",
    )(tile_expert, off, cum, n_tiles, h, rank_t, gate_t)


def _expert_kernel(te_ref, nt_ref, xs_ref, gs_ref, w1_ref, w3_ref, w2_ref, ys_ref, w1_scr, w3_scr, w2_scr, acc_scr):
    i = pl.program_id(0)

    @pl.when((i == 0) | (te_ref[i] != te_ref[jnp.maximum(i - 1, 0)]))
    def _():
        w1_scr[...] = w1_ref[...].astype(BF16)
        w3_scr[...] = w3_ref[...].astype(BF16)
        w2_scr[...] = w2_ref[...].astype(BF16)

    @pl.when(i < nt_ref[0])
    def _():
        x = xs_ref[...]
        gate = gs_ref[:, 0:1]
        acc_scr[...] = jnp.zeros_like(acc_scr)
        for c0 in range(0, EXPERT_FF, FF_CHUNK):
            cs = slice(c0, min(c0 + FF_CHUNK, EXPERT_FF))
            a = jnp.dot(x, w1_scr[:, cs], preferred_element_type=F32)
            b = jnp.dot(x, w3_scr[:, cs], preferred_element_type=F32)
            act = (_silu(a) * b * gate).astype(BF16)
            acc_scr[...] += jnp.dot(act, w2_scr[cs, :], preferred_element_type=F32)
        ys_ref[...] = acc_scr[...].astype(BF16)

    @pl.when(i >= nt_ref[0])
    def _():
        ys_ref[...] = jnp.zeros_like(ys_ref)


def _experts(tile_expert, n_tiles, xs, gs, w1, w3, w2):
    w_in = pl.BlockSpec((None, D_MODEL, EXPERT_FF), lambda i, te, nt: (te[i], 0, 0))
    w_out = pl.BlockSpec((None, EXPERT_FF, D_MODEL), lambda i, te, nt: (te[i], 0, 0))
    return pl.pallas_call(
        _expert_kernel,
        grid_spec=pltpu.PrefetchScalarGridSpec(
            num_scalar_prefetch=2,
            grid=(N_EXP_TILES,),
            in_specs=[pl.BlockSpec((EXP_TILE, D_MODEL), lambda i, te, nt: (i, 0)),
                      pl.BlockSpec((EXP_TILE, LANES), lambda i, te, nt: (i, 0)),
                      w_in, w_in, w_out],
            out_specs=pl.BlockSpec((EXP_TILE, D_MODEL), lambda i, te, nt: (i, 0)),
            scratch_shapes=[pltpu.VMEM((D_MODEL, EXPERT_FF), BF16), pltpu.VMEM((D_MODEL, EXPERT_FF), BF16),
                            pltpu.VMEM((EXPERT_FF, D_MODEL), BF16), pltpu.VMEM((EXP_TILE, D_MODEL), F32)]),
        out_shape=jax.ShapeDtypeStruct((N_EXP_TILES * EXP_TILE, D_MODEL), BF16),
        compiler_params=_cparams("arbitrary"),
        name="moe_experts",
    )(tile_expert, n_tiles, xs, gs, w1, w3, w2)


def _combine_kernel(off_ref, cum_ref, ys_ref, rank_ref, x_ref, gate_ref, lng_ref, lnb_ref, op_ref, os_ref, acc_scr):
    c = pl.program_id(0)
    acc_scr[...] = jnp.zeros_like(acc_scr)
    rank = rank_ref[...]
    lane = lax.broadcasted_iota(jnp.int32, rank.shape, 1)
    cols = lax.broadcasted_iota(jnp.int32, (1, SORT_TILE), 1)
    for e in range(N_EXPERTS):
        lo = off_ref[e] + cum_ref[c * N_EXPERTS + e]
        hi = off_ref[e] + cum_ref[(c + 1) * N_EXPERTS + e]
        r = jnp.sum(jnp.where(lane == e, rank, 0.0), -1, keepdims=True)
        pos = jnp.where(r >= 0.0, r + jnp.full((1, 1), off_ref[e], jnp.int32).astype(F32), -1.0)
        first = lo // SORT_TILE
        for k in range(2):
            s = first + k

            @pl.when((hi > lo) & (s * SORT_TILE < hi))
            def _():
                pick = pos == (s * SORT_TILE + cols).astype(F32)
                rows = ys_ref[pl.ds(pl.multiple_of(s * SORT_TILE, SORT_TILE), SORT_TILE), :]
                acc_scr[...] += jnp.dot(jnp.where(pick, 1.0, 0.0).astype(BF16), rows, preferred_element_type=F32)

    g = _group_of_tile(c, TOK_TILE)
    z = ALPHA * x_ref[...] + gate_ref[pl.ds(g, 1), :] * acc_scr[...]
    y = _layer_norm(z, lng_ref[...], lnb_ref[...])

    @pl.when(c < N_PROMPT_TOK_TILES)
    def _():
        op_ref[...] = y

    @pl.when(c >= N_PROMPT_TOK_TILES)
    def _():
        os_ref[...] = y


def _combine(off, cum, ys, rank, x, mod, ln_g, ln_b):
    const = lambda shape: pl.BlockSpec(shape, lambda c, *_: (0, 0))
    last_p = N_PROMPT_TOK_TILES - 1
    return pl.pallas_call(
        _combine_kernel,
        grid_spec=pltpu.PrefetchScalarGridSpec(
            num_scalar_prefetch=2,
            grid=(N_TOK_TILES,),
            in_specs=[pl.BlockSpec(ys.shape, lambda c, *_: (0, 0), pipeline_mode=pl.Buffered(1)),
                      pl.BlockSpec((TOK_TILE, LANES), lambda c, *_: (c, 0)),
                      pl.BlockSpec((TOK_TILE, D_MODEL), lambda c, *_: (c, 0)),
                      pl.BlockSpec((MOD_ROWS, D_MODEL), lambda c, *_: (0, 5)),
                      const((1, D_MODEL)), const((1, D_MODEL))],
            out_specs=[pl.BlockSpec((TOK_TILE, D_MODEL), lambda c, *_: (jnp.minimum(c, last_p), 0)),
                       pl.BlockSpec((TOK_TILE, D_MODEL), lambda c, *_: (jnp.maximum(c - last_p - 1, 0), 0))],
            scratch_shapes=[pltpu.VMEM((TOK_TILE, D_MODEL), F32)]),
        out_shape=[jax.ShapeDtypeStruct((T_PROMPT, D_MODEL), F32), jax.ShapeDtypeStruct((T_SAMPLE, D_MODEL), F32)],
        compiler_params=_cparams("arbitrary"),
        name="moe_combine_ln",
    )(off, cum, ys, rank, x, mod, ln_g.reshape(1, -1), ln_b.reshape(1, -1))


def _moe(x, mod, ln_g, ln_b, router, w1, w3, w2):
    h, rank, rank_t, gate_t, cum = _route(x, mod, router)
    counts = cum[N_TOK_TILES, :N_EXPERTS]
    tiles = (counts + EXP_TILE - 1) // EXP_TILE
    ends = jnp.cumsum(tiles)
    off = ((ends - tiles) * EXP_TILE).astype(jnp.int32)
    n_tiles = ends[-1:].astype(jnp.int32)
    tile_ids = jnp.minimum(jnp.arange(N_EXP_TILES, dtype=jnp.int32), n_tiles - 1)
    tile_expert = jnp.sum((tile_ids[:, None] >= ends[None, :]).astype(jnp.int32), -1)
    sub = EXP_TILE // SORT_TILE
    cum_flat = cum[:N_TOK_TILES + 1, :N_EXPERTS].reshape(-1)
    xs, gs = _gather(jnp.repeat(tile_expert, sub), off, cum_flat, n_tiles * sub, h, rank_t, gate_t)
    ys = _experts(tile_expert, n_tiles, xs, gs, w1, w3, w2)
    return _combine(off, cum_flat, ys, rank, x, mod, ln_g, ln_b)


FF_CHUNK = 256


def _ffn_kernel(x_ref, sh_ref, sc_ref, gate_ref, lng_ref, lnb_ref, w1_ref, w3_ref, w2_ref, o_ref, h_scr, acc_scr,
                *, tm, ff):
    i, j = pl.program_id(0), pl.program_id(1)
    g = _group_of_tile(i, tm)

    @pl.when(j == 0)
    def _():
        h_scr[...] = (x_ref[...] * (1.0 + sc_ref[pl.ds(g, 1), :]) + sh_ref[pl.ds(g, 1), :]).astype(BF16)
        acc_scr[...] = jnp.zeros_like(acc_scr)

    h = h_scr[...]
    for c0 in range(0, ff, FF_CHUNK):
        cs = slice(c0, min(c0 + FF_CHUNK, ff))
        a = jnp.dot(h, w1_ref[:, cs].astype(BF16), preferred_element_type=F32)
        b = jnp.dot(h, w3_ref[:, cs].astype(BF16), preferred_element_type=F32)
        act = _silu(a) * b
        acc_scr[...] += jnp.dot(act.astype(BF16), w2_ref[cs, :].astype(BF16), preferred_element_type=F32)

    @pl.when(j == pl.num_programs(1) - 1)
    def _():
        z = ALPHA * x_ref[...] + gate_ref[pl.ds(g, 1), :] * acc_scr[...]
        o_ref[...] = _layer_norm(z, lng_ref[...], lnb_ref[...])


def _ffn(x, mod, ln_g, ln_b, w1, w3, w2):
    tm = 1024
    ff = FF_CHUNK
    n_j = D_FF // ff
    w_in_spec = pl.BlockSpec((D_MODEL, ff), lambda i, j: (0, j))
    w_out_spec = pl.BlockSpec((ff, D_MODEL), lambda i, j: (j, 0))
    mod_spec = lambda col: pl.BlockSpec((MOD_ROWS, D_MODEL), lambda i, j: (0, col))
    const = lambda shape: pl.BlockSpec(shape, lambda i, j: (0, 0))
    in_specs = [pl.BlockSpec((tm, D_MODEL), lambda i, j: (i, 0)),
                mod_spec(3), mod_spec(4), mod_spec(5),
                const((1, D_MODEL)), const((1, D_MODEL)),
                w_in_spec, w_in_spec, w_out_spec]
    args = [x, mod, mod, mod, ln_g.reshape(1, -1), ln_b.reshape(1, -1), w1, w3, w2]
    return pl.pallas_call(
        functools.partial(_ffn_kernel, tm=tm, ff=ff),
        grid=(T_ALL // tm, n_j),
        in_specs=in_specs,
        out_specs=pl.BlockSpec((tm, D_MODEL), lambda i, j: (i, 0)),
        out_shape=jax.ShapeDtypeStruct((T_ALL, D_MODEL), F32),
        scratch_shapes=[pltpu.VMEM((tm, D_MODEL), BF16), pltpu.VMEM((tm, D_MODEL), F32)],
        compiler_params=_cparams("parallel", "arbitrary"),
        name="ffn",
    )(*args)


def kernel(x_prompt, x_sample, c, cache_k_b, cache_v_b, cache_k_c, cache_v_c, state_ret, c_ctx, ada_w, ada_b, ln_g, ln_b, w_in_ab, hy_conv_w, hy_conv_b, hf_w1, hf_b1, hf_freq, hf_w2, hf_b2, hf_w3, hy_skip, sink_b, w_out_ab, ffn_w1, ffn_w3, ffn_w2, w_in_cd, qn_g, kn_g, ret_decay, w_out_cd, moe_router, moe_w1, moe_w3, moe_w2):
    x_in = (x_prompt.reshape(T_PROMPT, D_MODEL), x_sample.reshape(T_SAMPLE, D_MODEL))
    cvec = jnp.concatenate([c_ctx[None], c, jnp.zeros((MOD_ROWS - 1 - DEC_BATCH, D_MODEL), F32)], 0)
    mod = _modulation(cvec, ada_w, ada_b)
    kw = N_KV * HEAD_DIM
    prompt = dict(row0=0, n_seq=BATCH, l=SEQ)
    sample = dict(row0=T_PROMPT, n_seq=DEC_BATCH, l=DEC_SEQ)

    u = _inproj(x_in, mod[0], w_in_ab[0], tn=1152)
    filt_args = (hf_w1[0], hf_b1[0], hf_freq[0], hf_w2[0], hf_b2[0], hf_w3[0])
    hy_args = (hy_conv_w[0], hy_conv_b[0], hy_skip[0])
    blank = lambda w: jnp.zeros((T_ALL, w), BF16)
    ya = _hyena(u, 0, BATCH, SEQ, *hy_args, _hyena_filter(SEQ, *filt_args), into=blank(HY_W))
    ya = _hyena(u, T_PROMPT // DEC_SEQ, DEC_BATCH, DEC_SEQ, *hy_args, _hyena_filter(DEC_SEQ, *filt_args), into=ya)
    cols_b = dict(q_col=AB_Q, k_col=AB_K, v_col=AB_V)
    ctx_b = (cache_k_b[:, 0].reshape(DEC_BATCH, PAST_LEN, kw), cache_v_b[:, 0].reshape(DEC_BATCH, PAST_LEN, kw))
    yb, = _attention(u, **prompt, **cols_b, qb=SEQ, sink=sink_b[0], into=blank(N_HEADS * HEAD_DIM))
    yb, = _attention(u, **sample, **cols_b, sink=sink_b[0], ctx=ctx_b, rope=True, band=True, into=yb)
    k_b = u[:T_PROMPT, AB_K:AB_V].reshape(BATCH, 1, SEQ, N_KV, HEAD_DIM)
    v_b = u[:T_PROMPT, AB_V:IN_AB].reshape(BATCH, 1, SEQ, N_KV, HEAD_DIM)
    x = _outproj(ya, yb, w_out_ab[0], x_in, mod[0], ln_g[0, 0], ln_b[0, 0])
    x = _ffn(x, mod[0], ln_g[0, 1], ln_b[0, 1], ffn_w1[0], ffn_w3[0], ffn_w2[0])

    u = _inproj((x,), mod[1], w_in_cd[0], tn=1664)
    cols_c = dict(q_col=CD_Q, k_col=CD_K, v_col=CD_V)
    gains = (qn_g[0], kn_g[0])
    ctx_c = (cache_k_c[:, 0].reshape(DEC_BATCH, PAST_LEN, kw), cache_v_c[:, 0].reshape(DEC_BATCH, PAST_LEN, kw))
    yc, k_c = _attention(u, **prompt, **cols_c, qb=SEQ, qk_gain=gains, emit_k=True, into=blank(N_HEADS * HEAD_DIM))
    yc, = _attention(u, **sample, **cols_c, qk_gain=gains, ctx=ctx_c, rope=True, into=yc)
    yd, s_r = _retention(u, ret_decay[0], **prompt, nb=4, emit_state=True, into=blank(RET_HEADS * RET_D))
    yd, = _retention(u, ret_decay[0], **sample, rope=True, s0=state_ret[:, 0], into=yd)
    k_c = k_c.reshape(BATCH, 1, SEQ, N_KV, HEAD_DIM)
    v_c = u[:T_PROMPT, CD_V:CD_RQ].reshape(BATCH, 1, SEQ, N_KV, HEAD_DIM)
    x = _outproj(yc, yd, w_out_cd[0], (x,), mod[1], ln_g[1, 0], ln_b[1, 0])
    y_prompt, y_sample = _moe(x, mod[1], ln_g[1, 1], ln_b[1, 1], moe_router[0], moe_w1[0], moe_w3[0], moe_w2[0])

    return (y_prompt.reshape(BATCH, SEQ, D_MODEL), y_sample.reshape(DEC_BATCH, DEC_SEQ, D_MODEL),
            k_b, v_b, k_c, v_c, s_r[:, None])
```

```python
import functools
import math

import numpy as np
import jax
import jax.numpy as jnp
from jax import lax
from jax.experimental import pallas as pl
from jax.experimental.pallas import tpu as pltpu

F32 = jnp.float32
BF16 = jnp.bfloat16

D_MODEL = 1024
BATCH = 16
SEQ = 256
DEC_BATCH = 2
DEC_SEQ = 1024
PAST_LEN = 512
GRID_W = 64
HEAD_DIM = 64
BLOCK = 128
HY_W = 512
POS_BANDS = 16
POS_EMB = 1 + 2 * POS_BANDS
FILT_HID = 64
HY_FAST_DECAY = 0.3
HY_SLOW_DECAY = 1.5
HY_TARGET = 1e-2
N_HEADS = 8
N_KV = 2
GROUPS = N_HEADS // N_KV
RET_HEADS = 4
RET_D = 128
CHUNK = 128
ROPE_BASE = 10000.0
D_FF = 2816
N_EXPERTS = 8
EXPERT_FF = 1408
DEPTH = 2
ALPHA = (2 * DEPTH) ** 0.25
EPS = 1e-6

T_PROMPT = BATCH * SEQ
T_SAMPLE = DEC_BATCH * DEC_SEQ
T_ALL = T_PROMPT + T_SAMPLE
GROUP_ROWS = 1024
N_PROMPT_GROUPS = T_PROMPT // GROUP_ROWS
MOD_ROWS = 16
LANES = 128
VMEM_LIMIT = 58 * 1024 * 1024

AB_Q = 3 * HY_W
AB_K = AB_Q + N_HEADS * HEAD_DIM
AB_V = AB_K + N_KV * HEAD_DIM
IN_AB = AB_V + N_KV * HEAD_DIM
CD_Q = 0
CD_K = N_HEADS * HEAD_DIM
CD_V = CD_K + N_KV * HEAD_DIM
CD_RQ = CD_V + N_KV * HEAD_DIM
CD_RK = CD_RQ + RET_HEADS * RET_D
CD_RV = CD_RK + RET_HEADS * RET_D
CD_GF = CD_RV + RET_HEADS * RET_D
CD_GB = CD_GF + RET_HEADS * RET_D
IN_CD = CD_GB + RET_HEADS * RET_D


def _cparams(*sem):
    return pltpu.CompilerParams(dimension_semantics=sem, vmem_limit_bytes=VMEM_LIMIT)


def _silu(x):
    return x * jax.nn.sigmoid(x)


def _bdot(a, b):
    return jnp.dot(a.astype(BF16), b.astype(BF16), preferred_element_type=F32)


def _bdot_nt(a, b):
    return lax.dot_general(a.astype(BF16), b.astype(BF16), (((1,), (1,)), ((), ())),
                           preferred_element_type=F32)


def _bdot_tn(a, b):
    return lax.dot_general(a.astype(BF16), b.astype(BF16), (((0,), (0,)), ((), ())),
                           preferred_element_type=F32)


def _layer_norm(z, g, b):
    mu = jnp.mean(z, -1, keepdims=True)
    zc = z - mu
    var = jnp.mean(zc * zc, -1, keepdims=True)
    return zc * lax.rsqrt(var + EPS) * g + b


def _fill_rows(kern, in_specs, args, into):
    n_in = len(args)
    in_specs.append(pl.BlockSpec(memory_space=pl.ANY))
    args.append(into)

    def kern_into(*refs):
        return kern(*refs[:n_in], *refs[n_in + 1:])

    return kern_into, {n_in: 0}


def _group_of_tile(i, tm):
    return jnp.maximum(i // (GROUP_ROWS // tm) - (N_PROMPT_GROUPS - 1), 0)


def _rope_tables(n_tokens, d, reps):
    nf = d // 4
    inv = ROPE_BASE ** (-np.arange(nf, dtype=np.float64) / nf)
    pos = np.arange(n_tokens)
    row, col = pos // GRID_W, pos % GRID_W
    ang_r = row[:, None] * inv[None, :]
    ang_c = col[:, None] * inv[None, :]
    zeros = np.zeros_like(ang_r)
    cos = np.concatenate([np.cos(ang_r), np.cos(ang_r), np.cos(ang_c), np.cos(ang_c)], -1)
    sin_a = np.concatenate([-np.sin(ang_r), zeros, -np.sin(ang_c), zeros], -1)
    sin_b = np.concatenate([zeros, np.sin(ang_r), zeros, np.sin(ang_c)], -1)
    tile = lambda a: jnp.asarray(np.tile(a, (1, reps)), F32)
    return tile(cos), tile(sin_a), tile(sin_b)


def _dft_mats(l):
    n = 2 * l
    k = np.arange(l, dtype=np.float64)
    ang = 2.0 * np.pi * np.outer(k, k) / n
    fc = np.cos(ang)
    fs = np.sin(ang)
    fs[0, :] = np.cos(np.pi * k)
    fwd = np.concatenate([fc, fs], 0)
    wk = np.full((l,), 2.0)
    wk[0] = 1.0
    inv = np.concatenate([fc.T * wk[None, :], fs.T * wk[None, :]], 1)
    return jnp.asarray(fwd, F32), jnp.asarray(inv, F32)


def _filter_consts(l):
    t = np.linspace(0.0, 1.0, l, dtype=np.float32).astype(np.float64)[:, None]
    w = (2.0 * math.pi * np.arange(l, dtype=np.float64) / l)[:, None]
    bands = np.linspace(1e-4, POS_BANDS - 1.0, POS_BANDS, dtype=np.float32).astype(np.float64)[None, :]
    z = np.concatenate([t, np.cos(bands * w), -np.sin(bands * w)], -1)
    z = np.pad(z, ((0, 0), (0, LANES - POS_EMB)))
    max_decay = math.log(HY_TARGET) / HY_FAST_DECAY
    min_decay = math.log(HY_TARGET) / HY_SLOW_DECAY
    deltas = np.linspace(min_decay, max_decay, HY_W, dtype=np.float32).astype(np.float64)
    window = np.exp(-t * np.abs(deltas)[None, :])
    return jnp.asarray(z, F32), jnp.asarray(window, F32)


def _mod_kernel(c_ref, w_ref, b_ref, o_ref):
    o_ref[...] = _bdot(_silu(c_ref[...]), w_ref[...]) + b_ref[...]


def _modulation(cvec, ada_w, ada_b):
    tn = 1536
    n = ada_w.shape[-1]
    return pl.pallas_call(
        _mod_kernel,
        grid=(DEPTH, n // tn),
        in_specs=[pl.BlockSpec((MOD_ROWS, D_MODEL), lambda l, j: (0, 0)),
                  pl.BlockSpec((None, D_MODEL, tn), lambda l, j: (l, 0, j)),
                  pl.BlockSpec((None, 1, tn), lambda l, j: (l, 0, j))],
        out_specs=pl.BlockSpec((None, MOD_ROWS, tn), lambda l, j: (l, 0, j)),
        out_shape=jax.ShapeDtypeStruct((DEPTH, MOD_ROWS, n), F32),
        compiler_params=_cparams("parallel", "parallel"),
        name="mod",
    )(cvec, ada_w, ada_b.reshape(DEPTH, 1, n))


def _slab_tiles(xs, tm):
    ends = np.cumsum([x.shape[0] // tm for x in xs])
    return [(int(e - x.shape[0] // tm), int(e)) for x, e in zip(xs, ends)]


def _inproj_kernel(*refs, tm, tn, slabs, taps):
    x_refs = refs[:len(slabs)]
    sh_ref, sc_ref, w_ref, o_ref = refs[len(slabs):len(slabs) + 4]
    tap_refs = refs[len(slabs) + 4:len(slabs) + 4 + len(taps)]
    h_scr, w_scr = refs[len(slabs) + 4 + len(taps):]
    j, i = pl.program_id(0), pl.program_id(1)
    rows = pl.ds(pl.multiple_of(i * tm, tm), tm)
    g = _group_of_tile(i, tm)

    for x_ref, (first, end) in zip(x_refs, slabs):
        @pl.when((j == 0) & (i >= first) & (i < end))
        def _():
            sc = sc_ref[pl.ds(g, 1), :]
            sh = sh_ref[pl.ds(g, 1), :]
            h_scr[rows, :] = (x_ref[...] * (1.0 + sc) + sh).astype(BF16)

    @pl.when(i == 0)
    def _():
        w_scr[...] = w_ref[...].astype(BF16)

    y = jnp.dot(h_scr[rows, :], w_scr[...], preferred_element_type=F32)
    o_ref[...] = y

    for tap_ref, col in zip(tap_refs, taps):
        @pl.when((j == col // tn) & (i < T_PROMPT // tm))
        def _():
            tap_ref[...] = y[:, col % tn:col % tn + tap_ref.shape[1]]


def _inproj(xs, mod, w, tn, taps):
    tm = 1024
    n = w.shape[1]
    slabs = _slab_tiles(xs, tm)
    tap_w = N_KV * HEAD_DIM
    last_p = T_PROMPT // tm - 1

    def tap_spec(col):
        jt = col // tn
        return pl.BlockSpec((tm, tap_w), lambda j, i: (
            jnp.where(j < jt, 0, jnp.where(j == jt, jnp.minimum(i, last_p), last_p)), 0))

    def slab_spec(first, end):
        last = end - first - 1
        return pl.BlockSpec((tm, D_MODEL), lambda j, i: (jnp.where(j == 0, jnp.clip(i - first, 0, last), last), 0))

    return pl.pallas_call(
        functools.partial(_inproj_kernel, tm=tm, tn=tn, slabs=slabs, taps=taps),
        grid=(n // tn, T_ALL // tm),
        in_specs=[slab_spec(*s) for s in slabs]
        + [pl.BlockSpec((MOD_ROWS, D_MODEL), lambda j, i: (0, 0)),
           pl.BlockSpec((MOD_ROWS, D_MODEL), lambda j, i: (0, 1)),
           pl.BlockSpec((D_MODEL, tn), lambda j, i: (0, j))],
        out_specs=[pl.BlockSpec((tm, tn), lambda j, i: (i, j))] + [tap_spec(c) for c in taps],
        out_shape=[jax.ShapeDtypeStruct((T_ALL, n), F32)]
        + [jax.ShapeDtypeStruct((T_PROMPT, tap_w), F32) for _ in taps],
        scratch_shapes=[pltpu.VMEM((T_ALL, D_MODEL), BF16), pltpu.VMEM((D_MODEL, tn), BF16)],
        compiler_params=_cparams("arbitrary", "arbitrary"),
        name="inproj",
    )(*xs, mod, mod, w)


def _hyena_filter_kernel(z_ref, w1_ref, b1_ref, fr_ref, w2_ref, b2_ref, w3_ref, win_ref, fwd32_ref, inv32_ref,
                         hc_ref, hs_ref, hc2_ref, fwd_ref, inv_ref, *, l):
    hi = lax.Precision.HIGHEST
    fwd = fwd32_ref[...].astype(BF16)
    fwd_ref[...] = fwd
    inv_ref[...] = inv32_ref[...].astype(BF16)
    fr = fr_ref[...]
    h = jnp.sin(fr * (jnp.dot(z_ref[...], w1_ref[...], precision=hi, preferred_element_type=F32) + b1_ref[...]))
    h = jnp.sin(fr * (jnp.dot(h, w2_ref[...], precision=hi, preferred_element_type=F32) + b2_ref[...]))
    h = jnp.dot(h, w3_ref[...], precision=hi, preferred_element_type=F32)
    win = win_ref[...]
    hf = h[:, :HY_W] * win
    hb = h[:, HY_W:] * win
    p = _bdot(fwd, hf + hb)
    q = _bdot(fwd, hf - hb)
    row0 = lax.broadcasted_iota(jnp.int32, (l, 1), 0) == 0
    hc = p[:l]
    hc_ref[...] = hc
    hs_ref[...] = jnp.where(row0, 0.0, q[l:])
    hc2_ref[...] = jnp.where(row0, p[l:l + 1], hc)


def _hyena_filter(l, fw1, fb1, ffreq, fw2, fb2, fw3):
    z, window = _filter_consts(l)
    fwd, inv = _dft_mats(l)
    pad_c = LANES - FILT_HID
    w1 = jnp.pad(fw1, ((0, LANES - POS_EMB), (0, pad_c)))
    w2 = jnp.pad(fw2, ((0, pad_c), (0, pad_c)))
    w3 = jnp.pad(fw3, ((0, pad_c), (0, 0)))
    row = lambda a: jnp.pad(a, (0, pad_c)).reshape(1, LANES)
    shp = jax.ShapeDtypeStruct((l, HY_W), F32)
    return pl.pallas_call(
        functools.partial(_hyena_filter_kernel, l=l),
        out_shape=(shp, shp, shp, jax.ShapeDtypeStruct(fwd.shape, BF16), jax.ShapeDtypeStruct(inv.shape, BF16)),
        compiler_params=pltpu.CompilerParams(vmem_limit_bytes=VMEM_LIMIT),
        name=f"hyena_filter_{l}",
    )(z, w1, row(fb1), row(ffreq), w2, row(fb2), w3, window, fwd, inv)


def _hyena_kernel(u_ref, cw_ref, cb_ref, skip_ref, fwd_ref, inv_ref, hc_ref, hs_ref, hc2_ref, o_ref, *, l):
    u = u_ref[...]
    rows = lax.broadcasted_iota(jnp.int32, (l, 1), 0)
    prev = jnp.where(rows == 0, 0.0, pltpu.roll(u, 1, 0))
    nxt = jnp.where(rows == l - 1, 0.0, pltpu.roll(u, l - 1, 0))
    uc = prev * cw_ref[0:1, :] + u * cw_ref[1:2, :] + nxt * cw_ref[2:3, :] + cb_ref[...]
    x0 = uc[:, :HY_W]
    x1 = uc[:, HY_W:2 * HY_W]
    v = uc[:, 2 * HY_W:] * x1
    ab = _bdot(fwd_ref[...], v)
    a, b = ab[:l], ab[l:]
    hs = hs_ref[...]
    re = a * hc_ref[...] - b * hs
    im = a * hs + b * hc2_ref[...]
    y = _bdot(inv_ref[...], jnp.concatenate([re, im], 0)) * (1.0 / (2 * l))
    o_ref[...] = ((y + skip_ref[...] * v) * x0).astype(o_ref.dtype)


def _hyena(u_all, row_block0, n_seq, l, conv_w, conv_b, skip, filt, into):
    hc, hs, hc2, fwd, inv = filt
    const = lambda shape: pl.BlockSpec(shape, lambda s: (0, 0))
    in_specs = [pl.BlockSpec((l, 3 * HY_W), lambda s: (row_block0 + s, 0)),
                const((3, 3 * HY_W)), const((1, 3 * HY_W)), const((1, HY_W)),
                const((2 * l, l)), const((l, 2 * l)),
                const((l, HY_W)), const((l, HY_W)), const((l, HY_W))]
    args = [u_all, conv_w, conv_b.reshape(1, -1), skip.reshape(1, -1), fwd, inv, hc, hs, hc2]
    kern, alias = _fill_rows(functools.partial(_hyena_kernel, l=l), in_specs, args, into)
    return pl.pallas_call(
        kern,
        grid=(n_seq,),
        in_specs=in_specs,
        out_specs=pl.BlockSpec((l, HY_W), lambda s: (row_block0 + s, 0)),
        out_shape=jax.ShapeDtypeStruct((T_ALL, HY_W), BF16),
        input_output_aliases=alias,
        compiler_params=_cparams("parallel"),
        name=f"hyena_{l}",
    )(*args)


def _seg_rms_norm(x, bd_ref, g):
    sq = x * x
    hi = sq.astype(BF16)
    lo = (sq - hi.astype(F32)).astype(BF16)
    bd = bd_ref[...]
    ss = (jnp.dot(hi, bd, preferred_element_type=F32) + jnp.dot(lo, bd, preferred_element_type=F32))
    return x * lax.rsqrt(ss * (1.0 / HEAD_DIM) + EPS) * g


def _rope(x, cos, sin_a, sin_b, quarter):
    w = x.shape[-1]
    return x * cos + pltpu.roll(x, w - quarter, 1) * sin_a + pltpu.roll(x, quarter, 1) * sin_b


def _attn_kernel(*refs, l, qb, lc, rope, qknorm, band, has_sink, emit_k):
    it = iter(refs)
    q_ref, k_ref, v_ref = next(it), next(it), next(it)
    if lc:
        ck_ref, cv_ref = next(it), next(it)
    if rope:
        cq_ref, saq_ref, sbq_ref = next(it), next(it), next(it)
        ckk_ref, sak_ref, sbk_ref = next(it), next(it), next(it)
    if qknorm:
        qg_ref, kg_ref, bdq_ref, bdk_ref = next(it), next(it), next(it), next(it)
    if has_sink:
        sink_ref = next(it)
    o_ref = next(it)
    if emit_k:
        kout_ref = next(it)
    kp_scr = next(it)

    qi = pl.program_id(1)

    @pl.when(qi == 0)
    def _():
        k = k_ref[...]
        if qknorm:
            k = _seg_rms_norm(k, bdk_ref, kg_ref[...])
        if emit_k:
            kout_ref[...] = k
        if rope:
            k = _rope(k, ckk_ref[...], sak_ref[...], sbk_ref[...], HEAD_DIM // 4)
        for kv in range(N_KV):
            kp_scr[kv] = k[:, kv * HEAD_DIM:(kv + 1) * HEAD_DIM].astype(BF16)

    q = q_ref[...]
    if qknorm:
        q = _seg_rms_norm(q, bdq_ref, qg_ref[...])
    if rope:
        q = _rope(q, cq_ref[...], saq_ref[...], sbq_ref[...], HEAD_DIM // 4)
    q = q * HEAD_DIM ** -0.5

    rows = GROUPS * qb
    if band:
        n_loc = 3 * BLOCK
        start = pl.multiple_of(jnp.clip((qi - 1) * BLOCK, 0, l - n_loc), BLOCK)
        keys = pl.ds(start, n_loc)
        tq = qi * qb + lax.broadcasted_iota(jnp.int32, (rows, 1), 0) % qb
        tk = start + lax.broadcasted_iota(jnp.int32, (1, n_loc), 1)
        valid = jnp.abs(tq - tk) <= BLOCK
    else:
        keys = slice(None)

    outs = []
    for kv in range(N_KV):
        lanes = slice(kv * HEAD_DIM, (kv + 1) * HEAD_DIM)
        qs = jnp.concatenate([q[:, (kv * GROUPS + g) * HEAD_DIM:(kv * GROUPS + g + 1) * HEAD_DIM]
                              for g in range(GROUPS)], 0)
        s = _bdot_nt(qs, kp_scr[kv, keys, :])
        if band:
            s = jnp.where(valid, s, -jnp.inf)
        if lc:
            s_c = _bdot_nt(qs, ck_ref[:, lanes])
        e_parts, ec_parts, dens = [], [], []
        for g in range(GROUPS):
            r = slice(g * qb, (g + 1) * qb)
            m = jnp.max(s[r], -1, keepdims=True)
            if lc:
                m = jnp.maximum(m, jnp.max(s_c[r], -1, keepdims=True))
            if has_sink:
                sink = sink_ref[kv * GROUPS + g]
                m = jnp.maximum(m, sink)
            e = jnp.exp(s[r] - m)
            den = jnp.sum(e, -1, keepdims=True)
            e_parts.append(e)
            if lc:
                e_c = jnp.exp(s_c[r] - m)
                den = den + jnp.sum(e_c, -1, keepdims=True)
                ec_parts.append(e_c)
            if has_sink:
                den = den + jnp.exp(sink - m)
            dens.append(den)
        o = _bdot(jnp.concatenate(e_parts, 0), v_ref[keys, lanes])
        if lc:
            o = o + _bdot(jnp.concatenate(ec_parts, 0), cv_ref[:, lanes])
        outs.extend(o[g * qb:(g + 1) * qb] / dens[g] for g in range(GROUPS))
    o_ref[...] = jnp.concatenate(outs, 1).astype(o_ref.dtype)


def _attention(u, *, row0, n_seq, l, q_col, k_col, v_col, into, qb=BLOCK, ctx=None, rope=False, qk_gain=None,
               band=False, sink=None, emit_k=False):
    nq = l // qb
    qw = N_HEADS * HEAD_DIM
    kw = N_KV * HEAD_DIM
    qb0, sb0 = row0 // qb, row0 // l
    seq_spec = lambda col: pl.BlockSpec((l, kw), lambda b, i: (sb0 + b, col // kw))
    const = lambda shape: pl.BlockSpec(shape, lambda b, i: (0, 0))
    in_specs = [pl.BlockSpec((qb, qw), lambda b, i: (qb0 + b * nq + i, q_col // qw)),
                seq_spec(k_col), seq_spec(v_col)]
    args = [u, u, u]
    lc = 0
    if ctx is not None:
        lc = ctx[0].shape[1]
        in_specs += [pl.BlockSpec((None, lc, kw), lambda b, i: (b, 0, 0))] * 2
        args += list(ctx)
    if rope:
        tabs = _rope_tables(l, HEAD_DIM, N_HEADS)
        in_specs += [pl.BlockSpec((qb, qw), lambda b, i: (i, 0))] * 3 + [const((l, kw))] * 3
        args += list(tabs) + list(tabs)
    if qk_gain is not None:
        bd = np.kron(np.eye(N_HEADS), np.ones((HEAD_DIM, HEAD_DIM)))
        in_specs += [const((1, qw)), const((1, kw)), const((qw, qw)), const((kw, kw))]
        args += [jnp.tile(qk_gain[0], N_HEADS).reshape(1, qw), jnp.tile(qk_gain[1], N_KV).reshape(1, kw),
                 jnp.asarray(bd, BF16), jnp.asarray(bd[:kw, :kw], BF16)]
    if sink is not None:
        in_specs.append(pl.BlockSpec(memory_space=pltpu.SMEM))
        args.append(sink)
    out_specs = [pl.BlockSpec((qb, qw), lambda b, i: (qb0 + b * nq + i, 0))]
    out_shape = [jax.ShapeDtypeStruct((T_ALL, qw), BF16)]
    if emit_k:
        out_specs.append(pl.BlockSpec((l, kw), lambda b, i: (b, 0)))
        out_shape.append(jax.ShapeDtypeStruct((n_seq * l, kw), F32))
    kern = functools.partial(_attn_kernel, l=l, qb=qb, lc=lc, rope=rope, qknorm=qk_gain is not None, band=band,
                             has_sink=sink is not None, emit_k=emit_k)
    kern, alias = _fill_rows(kern, in_specs, args, into)
    return pl.pallas_call(
        kern,
        grid=(n_seq, nq),
        in_specs=in_specs,
        out_specs=out_specs,
        out_shape=out_shape,
        input_output_aliases=alias,
        scratch_shapes=[pltpu.VMEM((N_KV, l, HEAD_DIM), BF16)],
        compiler_params=_cparams("parallel", "arbitrary"),
        name=f"attn_{l}_{'b' if sink is not None else 'c'}",
    )(*args)


def _ret_kernel(*refs, l, nb, rope, has_s0, emit_state):
    it = iter(refs)
    dec_ref = next(it)
    rq_ref, rk_ref, rv_ref, gf_ref, gb_ref = next(it), next(it), next(it), next(it), next(it)
    if rope:
        cos_ref, sa_ref, sb_ref = next(it), next(it), next(it)
    if has_s0:
        s0_ref = next(it)
    o_ref = next(it)
    if emit_state:
        st_ref = next(it)

    h = pl.program_id(1)
    qs, ks, vs = [], [], []
    for s in range(nb):
        seq = slice(s * l, (s + 1) * l)
        q = rq_ref[seq, :] * RET_D ** -0.5
        k = rk_ref[seq, :]
        if rope:
            q = _rope(q, cos_ref[...], sa_ref[...], sb_ref[...], RET_D // 4)
            k = _rope(k, cos_ref[...], sa_ref[...], sb_ref[...], RET_D // 4)
        qs.append(q)
        ks.append(k)
        vs.append(rv_ref[seq, :])
    n = l // CHUNK
    ii = lax.broadcasted_iota(jnp.int32, (CHUNK, 1), 0).astype(F32)
    jj = lax.broadcasted_iota(jnp.int32, (1, CHUNK), 1).astype(F32)
    diff = ii - jj
    ys = [None] * nb
    for d in range(2):
        log_g = jnp.log(jax.nn.sigmoid(jnp.full((1, 1), dec_ref[d, h], F32)))
        if d == 0:
            mask = jnp.exp(jnp.where(diff >= 0, diff * log_g, -jnp.inf))
            q_dec = jnp.exp((ii + 1.0) * log_g)
            k_dec = jnp.exp((CHUNK - 1.0 - ii) * log_g)
            order = range(n)
        else:
            mask = jnp.exp(jnp.where(diff <= 0, -diff * log_g, -jnp.inf))
            q_dec = jnp.exp((CHUNK - ii) * log_g)
            k_dec = jnp.exp(ii * log_g)
            order = reversed(range(n))
        c_dec = jnp.exp(CHUNK * log_g)
        order = list(order)
        for s in range(nb):
            seq = slice(s * l, (s + 1) * l)
            q, k, v = qs[s], ks[s], vs[s]
            state = s0_ref[s, d] if has_s0 else jnp.zeros((RET_D, RET_D), F32)
            o_chunks = [None] * n
            for c in order:
                sl = slice(c * CHUNK, (c + 1) * CHUNK)
                qc, kc, vc = q[sl], k[sl], v[sl]
                inner = _bdot_nt(qc, kc) * mask
                o_chunks[c] = _bdot(inner, vc) + _bdot(qc * q_dec, state)
                state = state * c_dec + _bdot_tn(kc * k_dec, vc)
            if emit_state:
                st_ref[s, d] = state
            o = jnp.concatenate(o_chunks, 0)
            o = o * lax.rsqrt(jnp.mean(o * o, -1, keepdims=True) + EPS)
            gate = _silu((gf_ref if d == 0 else gb_ref)[seq, :])
            ys[s] = gate * o if ys[s] is None else ys[s] + gate * o
    for s in range(nb):
        o_ref[s * l:(s + 1) * l, :] = ys[s].astype(o_ref.dtype)


def _retention(u, ret_decay, *, row0, n_seq, l, into, nb=1, rope=False, s0=None, emit_state=False):
    sb0 = row0 // (nb * l)
    col = lambda c0: pl.BlockSpec((nb * l, RET_D), lambda b, h: (sb0 + b, c0 // RET_D + h))
    in_specs = [pl.BlockSpec(memory_space=pltpu.SMEM),
                col(CD_RQ), col(CD_RK), col(CD_RV), col(CD_GF), col(CD_GB)]
    args = [ret_decay, u, u, u, u, u]
    if rope:
        in_specs += [pl.BlockSpec((l, RET_D), lambda b, h: (0, 0))] * 3
        args += list(_rope_tables(l, RET_D, 1))
    state_spec = pl.BlockSpec((nb, 2, None, RET_D, RET_D), lambda b, h: (b, 0, h, 0, 0))
    if s0 is not None:
        in_specs.append(state_spec)
        args.append(s0)
    out_specs = [pl.BlockSpec((nb * l, RET_D), lambda b, h: (sb0 + b, h))]
    out_shape = [jax.ShapeDtypeStruct((T_ALL, RET_HEADS * RET_D), BF16)]
    if emit_state:
        out_specs.append(state_spec)
        out_shape.append(jax.ShapeDtypeStruct((n_seq, 2, RET_HEADS, RET_D, RET_D), F32))
    kern = functools.partial(_ret_kernel, l=l, nb=nb, rope=rope, has_s0=s0 is not None, emit_state=emit_state)
    kern, alias = _fill_rows(kern, in_specs, args, into)
    return pl.pallas_call(
        kern,
        grid=(n_seq // nb, RET_HEADS),
        in_specs=in_specs,
        out_specs=out_specs,
        out_shape=out_shape,
        input_output_aliases=alias,
        compiler_params=_cparams("parallel", "parallel"),
        name=f"retention_{l}",
    )(*args)


def _outproj_kernel(*refs, tm, slabs):
    ya_ref, yb_ref, w_ref = refs[:3]
    x_refs = refs[3:3 + len(slabs)]
    gate_ref, lng_ref, lnb_ref, o_ref, w_scr = refs[3 + len(slabs):]
    i = pl.program_id(0)

    @pl.when(i == 0)
    def _():
        w_scr[...] = w_ref[...].astype(BF16)

    half = ya_ref.shape[1]
    m = (jnp.dot(ya_ref[...], w_scr[:half], preferred_element_type=F32)
         + jnp.dot(yb_ref[...], w_scr[half:], preferred_element_type=F32))
    gm = gate_ref[pl.ds(_group_of_tile(i, tm), 1), :] * m

    for x_ref, (first, end) in zip(x_refs, slabs):
        @pl.when((i >= first) & (i < end))
        def _():
            o_ref[...] = _layer_norm(ALPHA * x_ref[...] + gm, lng_ref[...], lnb_ref[...])


def _outproj(ya, yb, w, xs, mod, ln_g, ln_b):
    tm = 1024
    half = ya.shape[1]
    slabs = _slab_tiles(xs, tm)
    const = lambda shape: pl.BlockSpec(shape, lambda i: (0, 0))

    def slab_spec(first, end):
        return pl.BlockSpec((tm, D_MODEL), lambda i: (jnp.clip(i - first, 0, end - first - 1), 0))

    return pl.pallas_call(
        functools.partial(_outproj_kernel, tm=tm, slabs=slabs),
        grid=(T_ALL // tm,),
        in_specs=[pl.BlockSpec((tm, half), lambda i: (i, 0)),
                  pl.BlockSpec((tm, half), lambda i: (i, 0)),
                  const((2 * half, D_MODEL))]
        + [slab_spec(*s) for s in slabs]
        + [pl.BlockSpec((MOD_ROWS, D_MODEL), lambda i: (0, 2)),
           const((1, D_MODEL)), const((1, D_MODEL))],
        out_specs=pl.BlockSpec((tm, D_MODEL), lambda i: (i, 0)),
        out_shape=jax.ShapeDtypeStruct((T_ALL, D_MODEL), F32),
        scratch_shapes=[pltpu.VMEM((2 * half, D_MODEL), BF16)],
        compiler_params=_cparams("arbitrary"),
        name="outproj_ln",
    )(ya, yb, w, *xs, mod, ln_g.reshape(1, -1), ln_b.reshape(1, -1))


TOK_TILE = 256
N_TOK_TILES = T_ALL // TOK_TILE
SORT_TILE = 256
EXP_TILE = 512
N_EXP_TILES = (2 * T_ALL) // EXP_TILE + N_EXPERTS
N_SORT_TILES = N_EXP_TILES * (EXP_TILE // SORT_TILE)
N_PROMPT_TOK_TILES = T_PROMPT // TOK_TILE
CUM_ROWS = 32


def _route_kernel(x_ref, sh_ref, sc_ref, r_ref, h_ref, rank_ref, rank_t_ref, gate_t_ref, cum_ref,
                  carry_row, carry_col):
    c = pl.program_id(0)

    @pl.when(c == 0)
    def _():
        carry_row[...] = jnp.zeros_like(carry_row)
        carry_col[...] = jnp.zeros_like(carry_col)
        cum_ref[...] = jnp.zeros_like(cum_ref)

    g = _group_of_tile(c, TOK_TILE)
    h = x_ref[...] * (1.0 + sc_ref[pl.ds(g, 1), :]) + sh_ref[pl.ds(g, 1), :]
    h_ref[...] = h.astype(BF16)
    logits = jnp.dot(h, r_ref[...], precision=lax.Precision.HIGHEST, preferred_element_type=F32)
    lane = lax.broadcasted_iota(jnp.int32, logits.shape, 1)
    logits = jnp.where(lane < N_EXPERTS, logits, -jnp.inf)
    m1 = jnp.max(logits, -1, keepdims=True)
    i1 = jnp.min(jnp.where(logits == m1, lane, LANES), -1, keepdims=True)
    rest = jnp.where(lane == i1, -jnp.inf, logits)
    m2 = jnp.max(rest, -1, keepdims=True)
    i2 = jnp.min(jnp.where(rest == m2, lane, LANES), -1, keepdims=True)
    e2 = jnp.exp(m2 - m1)
    den = 1.0 + e2
    gates = jnp.where(lane == i1, 1.0 / den, 0.0) + jnp.where(lane == i2, e2 / den, 0.0)
    sel = jnp.where((lane == i1) | (lane == i2), 1.0, 0.0)
    sel_t = sel.T
    ti = lax.broadcasted_iota(jnp.int32, (TOK_TILE, TOK_TILE), 0)
    tj = lax.broadcasted_iota(jnp.int32, (TOK_TILE, TOK_TILE), 1)
    before = jnp.where(tj < ti, 1.0, 0.0).astype(BF16)
    rank = jnp.dot(before, sel.astype(BF16), preferred_element_type=F32) + carry_row[...]
    rank_t = lax.dot_general(sel_t.astype(BF16), before, (((1,), (1,)), ((), ())),
                             preferred_element_type=F32) + carry_col[...]
    rank_ref[...] = jnp.where(sel > 0.0, rank, -1.0)
    rank_t_ref[...] = jnp.where(sel_t > 0.0, rank_t, -1.0)[:N_EXPERTS]
    gate_t_ref[...] = gates.T[:N_EXPERTS]
    cum_ref[pl.ds(c, 1), :] = carry_row[...].astype(jnp.int32)
    carry_row[...] += jnp.sum(sel, 0, keepdims=True)
    carry_col[...] += jnp.sum(sel_t, 1, keepdims=True)

    @pl.when(c == N_TOK_TILES - 1)
    def _():
        cum_ref[pl.ds(N_TOK_TILES, 1), :] = carry_row[...].astype(jnp.int32)


def _route(x, mod, router):
    tile = lambda w: pl.BlockSpec((TOK_TILE, w), lambda c: (c, 0))
    tile_t = pl.BlockSpec((N_EXPERTS, TOK_TILE), lambda c: (0, c))
    return pl.pallas_call(
        _route_kernel,
        grid=(N_TOK_TILES,),
        in_specs=[tile(D_MODEL),
                  pl.BlockSpec((MOD_ROWS, D_MODEL), lambda c: (0, 3)),
                  pl.BlockSpec((MOD_ROWS, D_MODEL), lambda c: (0, 4)),
                  pl.BlockSpec((D_MODEL, LANES), lambda c: (0, 0))],
        out_specs=[tile(D_MODEL), tile(LANES), tile_t, tile_t,
                   pl.BlockSpec((CUM_ROWS, LANES), lambda c: (0, 0))],
        out_shape=[jax.ShapeDtypeStruct((T_ALL, D_MODEL), BF16),
                   jax.ShapeDtypeStruct((T_ALL, LANES), F32),
                   jax.ShapeDtypeStruct((N_EXPERTS, T_ALL), F32),
                   jax.ShapeDtypeStruct((N_EXPERTS, T_ALL), F32),
                   jax.ShapeDtypeStruct((CUM_ROWS, LANES), jnp.int32)],
        scratch_shapes=[pltpu.VMEM((1, LANES), F32), pltpu.VMEM((LANES, 1), F32)],
        compiler_params=_cparams("arbitrary"),
        name="route",
    )(x, mod, mod, jnp.pad(router, ((0, 0), (0, LANES - N_EXPERTS))))


def _gather_kernel(te_ref, off_ref, cum_ref, nt_ref, h_ref, rank_t_ref, gate_t_ref, xs_ref, gs_ref, acc_scr, g_scr):
    i = pl.program_id(0)
    e = te_ref[i]
    r0 = i * SORT_TILE - off_ref[e]
    acc_scr[...] = jnp.zeros_like(acc_scr)
    g_scr[...] = jnp.zeros_like(g_scr)
    want = (r0 + lax.broadcasted_iota(jnp.int32, (SORT_TILE, 1), 0)).astype(F32)
    for c in range(N_TOK_TILES):
        lo = cum_ref[c * N_EXPERTS + e]
        hi = cum_ref[(c + 1) * N_EXPERTS + e]

        @pl.when((i < nt_ref[0]) & (lo < r0 + SORT_TILE) & (hi > r0))
        def _():
            cols = slice(c * TOK_TILE, (c + 1) * TOK_TILE)
            pick = rank_t_ref[pl.ds(e, 1), cols] == want
            acc_scr[...] += jnp.dot(jnp.where(pick, 1.0, 0.0).astype(BF16), h_ref[cols, :],
                                    preferred_element_type=F32)
            g_scr[...] += jnp.sum(jnp.where(pick, gate_t_ref[pl.ds(e, 1), cols], 0.0), -1, keepdims=True)

    xs_ref[...] = acc_scr[...].astype(BF16)
    gs_ref[...] = jnp.broadcast_to(g_scr[...], gs_ref.shape)


def _gather(tile_expert, off, cum, n_tiles, h, rank_t, gate_t):
    const = lambda shape: pl.BlockSpec(shape, lambda i, *_: (0, 0))
    return pl.pallas_call(
        _gather_kernel,
        grid_spec=pltpu.PrefetchScalarGridSpec(
            num_scalar_prefetch=4,
            grid=(N_SORT_TILES,),
            in_specs=[const((T_ALL, D_MODEL)), const((N_EXPERTS, T_ALL)), const((N_EXPERTS, T_ALL))],
            out_specs=[pl.BlockSpec((SORT_TILE, D_MODEL), lambda i, *_: (i, 0)),
                       pl.BlockSpec((SORT_TILE, LANES), lambda i, *_: (i, 0))],
            scratch_shapes=[pltpu.VMEM((SORT_TILE, D_MODEL), F32), pltpu.VMEM((SORT_TILE, 1), F32)]),
        out_shape=[jax.ShapeDtypeStruct((N_SORT_TILES * SORT_TILE, D_MODEL), BF16),
                   jax.ShapeDtypeStruct((N_SORT_TILES * SORT_TILE, LANES), F32)],
        compiler_params=_cparams("parallel"),
        name="moe_gather",
    )(tile_expert, off, cum, n_tiles, h, rank_t, gate_t)


def _expert_kernel(te_ref, nt_ref, xs_ref, gs_ref, w1_ref, w3_ref, w2_ref, ys_ref, w1_scr, w3_scr, w2_scr, acc_scr):
    i = pl.program_id(0)

    @pl.when((i == 0) | (te_ref[i] != te_ref[jnp.maximum(i - 1, 0)]))
    def _():
        w1_scr[...] = w1_ref[...].astype(BF16)
        w3_scr[...] = w3_ref[...].astype(BF16)
        w2_scr[...] = w2_ref[...].astype(BF16)

    @pl.when(i < nt_ref[0])
    def _():
        x = xs_ref[...]
        gate = gs_ref[:, 0:1]
        acc_scr[...] = jnp.zeros_like(acc_scr)
        for c0 in range(0, EXPERT_FF, FF_CHUNK):
            cs = slice(c0, min(c0 + FF_CHUNK, EXPERT_FF))
            a = jnp.dot(x, w1_scr[:, cs], preferred_element_type=F32)
            b = jnp.dot(x, w3_scr[:, cs], preferred_element_type=F32)
            act = (_silu(a) * b * gate).astype(BF16)
            acc_scr[...] += jnp.dot(act, w2_scr[cs, :], preferred_element_type=F32)
        ys_ref[...] = acc_scr[...].astype(BF16)

    @pl.when(i >= nt_ref[0])
    def _():
        ys_ref[...] = jnp.zeros_like(ys_ref)


def _experts(tile_expert, n_tiles, xs, gs, w1, w3, w2):
    w_in = pl.BlockSpec((None, D_MODEL, EXPERT_FF), lambda i, te, nt: (te[i], 0, 0))
    w_out = pl.BlockSpec((None, EXPERT_FF, D_MODEL), lambda i, te, nt: (te[i], 0, 0))
    return pl.pallas_call(
        _expert_kernel,
        grid_spec=pltpu.PrefetchScalarGridSpec(
            num_scalar_prefetch=2,
            grid=(N_EXP_TILES,),
            in_specs=[pl.BlockSpec((EXP_TILE, D_MODEL), lambda i, te, nt: (i, 0)),
                      pl.BlockSpec((EXP_TILE, LANES), lambda i, te, nt: (i, 0)),
                      w_in, w_in, w_out],
            out_specs=pl.BlockSpec((EXP_TILE, D_MODEL), lambda i, te, nt: (i, 0)),
            scratch_shapes=[pltpu.VMEM((D_MODEL, EXPERT_FF), BF16), pltpu.VMEM((D_MODEL, EXPERT_FF), BF16),
                            pltpu.VMEM((EXPERT_FF, D_MODEL), BF16), pltpu.VMEM((EXP_TILE, D_MODEL), F32)]),
        out_shape=jax.ShapeDtypeStruct((N_EXP_TILES * EXP_TILE, D_MODEL), BF16),
        compiler_params=_cparams("arbitrary"),
        name="moe_experts",
    )(tile_expert, n_tiles, xs, gs, w1, w3, w2)


def _combine_kernel(off_ref, cum_ref, ys_ref, rank_ref, x_ref, gate_ref, lng_ref, lnb_ref, op_ref, os_ref, acc_scr):
    c = pl.program_id(0)
    acc_scr[...] = jnp.zeros_like(acc_scr)
    rank = rank_ref[...]
    lane = lax.broadcasted_iota(jnp.int32, rank.shape, 1)
    cols = lax.broadcasted_iota(jnp.int32, (1, SORT_TILE), 1)
    for e in range(N_EXPERTS):
        lo = off_ref[e] + cum_ref[c * N_EXPERTS + e]
        hi = off_ref[e] + cum_ref[(c + 1) * N_EXPERTS + e]
        r = jnp.sum(jnp.where(lane == e, rank, 0.0), -1, keepdims=True)
        pos = jnp.where(r >= 0.0, r + jnp.full((1, 1), off_ref[e], jnp.int32).astype(F32), -1.0)
        first = lo // SORT_TILE
        for k in range(2):
            s = first + k

            @pl.when((hi > lo) & (s * SORT_TILE < hi))
            def _():
                pick = pos == (s * SORT_TILE + cols).astype(F32)
                rows = ys_ref[pl.ds(pl.multiple_of(s * SORT_TILE, SORT_TILE), SORT_TILE), :]
                acc_scr[...] += jnp.dot(jnp.where(pick, 1.0, 0.0).astype(BF16), rows, preferred_element_type=F32)

    g = _group_of_tile(c, TOK_TILE)
    z = ALPHA * x_ref[...] + gate_ref[pl.ds(g, 1), :] * acc_scr[...]
    y = _layer_norm(z, lng_ref[...], lnb_ref[...])

    @pl.when(c < N_PROMPT_TOK_TILES)
    def _():
        op_ref[...] = y

    @pl.when(c >= N_PROMPT_TOK_TILES)
    def _():
        os_ref[...] = y


def _combine(off, cum, ys, rank, x, mod, ln_g, ln_b):
    const = lambda shape: pl.BlockSpec(shape, lambda c, *_: (0, 0))
    last_p = N_PROMPT_TOK_TILES - 1
    return pl.pallas_call(
        _combine_kernel,
        grid_spec=pltpu.PrefetchScalarGridSpec(
            num_scalar_prefetch=2,
            grid=(N_TOK_TILES,),
            in_specs=[pl.BlockSpec(ys.shape, lambda c, *_: (0, 0), pipeline_mode=pl.Buffered(1)),
                      pl.BlockSpec((TOK_TILE, LANES), lambda c, *_: (c, 0)),
                      pl.BlockSpec((TOK_TILE, D_MODEL), lambda c, *_: (c, 0)),
                      pl.BlockSpec((MOD_ROWS, D_MODEL), lambda c, *_: (0, 5)),
                      const((1, D_MODEL)), const((1, D_MODEL))],
            out_specs=[pl.BlockSpec((TOK_TILE, D_MODEL), lambda c, *_: (jnp.minimum(c, last_p), 0)),
                       pl.BlockSpec((TOK_TILE, D_MODEL), lambda c, *_: (jnp.maximum(c - last_p - 1, 0), 0))],
            scratch_shapes=[pltpu.VMEM((TOK_TILE, D_MODEL), F32)]),
        out_shape=[jax.ShapeDtypeStruct((T_PROMPT, D_MODEL), F32), jax.ShapeDtypeStruct((T_SAMPLE, D_MODEL), F32)],
        compiler_params=_cparams("arbitrary"),
        name="moe_combine_ln",
    )(off, cum, ys, rank, x, mod, ln_g.reshape(1, -1), ln_b.reshape(1, -1))


def _moe(x, mod, ln_g, ln_b, router, w1, w3, w2):
    h, rank, rank_t, gate_t, cum = _route(x, mod, router)
    counts = cum[N_TOK_TILES, :N_EXPERTS]
    tiles = (counts + EXP_TILE - 1) // EXP_TILE
    ends = jnp.cumsum(tiles)
    off = ((ends - tiles) * EXP_TILE).astype(jnp.int32)
    n_tiles = ends[-1:].astype(jnp.int32)
    tile_ids = jnp.minimum(jnp.arange(N_EXP_TILES, dtype=jnp.int32), n_tiles - 1)
    tile_expert = jnp.sum((tile_ids[:, None] >= ends[None, :]).astype(jnp.int32), -1)
    sub = EXP_TILE // SORT_TILE
    cum_flat = cum[:N_TOK_TILES + 1, :N_EXPERTS].reshape(-1)
    xs, gs = _gather(jnp.repeat(tile_expert, sub), off, cum_flat, n_tiles * sub, h, rank_t, gate_t)
    ys = _experts(tile_expert, n_tiles, xs, gs, w1, w3, w2)
    return _combine(off, cum_flat, ys, rank, x, mod, ln_g, ln_b)


FF_CHUNK = 256


def _ffn_kernel(x_ref, sh_ref, sc_ref, gate_ref, lng_ref, lnb_ref, w1_ref, w3_ref, w2_ref, o_ref, h_scr, acc_scr,
                *, tm, ff):
    i, j = pl.program_id(0), pl.program_id(1)
    g = _group_of_tile(i, tm)

    @pl.when(j == 0)
    def _():
        h_scr[...] = (x_ref[...] * (1.0 + sc_ref[pl.ds(g, 1), :]) + sh_ref[pl.ds(g, 1), :]).astype(BF16)
        acc_scr[...] = jnp.zeros_like(acc_scr)

    h = h_scr[...]
    for c0 in range(0, ff, FF_CHUNK):
        cs = slice(c0, min(c0 + FF_CHUNK, ff))
        a = jnp.dot(h, w1_ref[:, cs].astype(BF16), preferred_element_type=F32)
        b = jnp.dot(h, w3_ref[:, cs].astype(BF16), preferred_element_type=F32)
        act = _silu(a) * b
        acc_scr[...] += jnp.dot(act.astype(BF16), w2_ref[cs, :].astype(BF16), preferred_element_type=F32)

    @pl.when(j == pl.num_programs(1) - 1)
    def _():
        z = ALPHA * x_ref[...] + gate_ref[pl.ds(g, 1), :] * acc_scr[...]
        o_ref[...] = _layer_norm(z, lng_ref[...], lnb_ref[...])


def _ffn(x, mod, ln_g, ln_b, w1, w3, w2):
    tm = 1024
    ff = FF_CHUNK
    n_j = D_FF // ff
    w_in_spec = pl.BlockSpec((D_MODEL, ff), lambda i, j: (0, j))
    w_out_spec = pl.BlockSpec((ff, D_MODEL), lambda i, j: (j, 0))
    mod_spec = lambda col: pl.BlockSpec((MOD_ROWS, D_MODEL), lambda i, j: (0, col))
    const = lambda shape: pl.BlockSpec(shape, lambda i, j: (0, 0))
    in_specs = [pl.BlockSpec((tm, D_MODEL), lambda i, j: (i, 0)),
                mod_spec(3), mod_spec(4), mod_spec(5),
                const((1, D_MODEL)), const((1, D_MODEL)),
                w_in_spec, w_in_spec, w_out_spec]
    args = [x, mod, mod, mod, ln_g.reshape(1, -1), ln_b.reshape(1, -1), w1, w3, w2]
    return pl.pallas_call(
        functools.partial(_ffn_kernel, tm=tm, ff=ff),
        grid=(T_ALL // tm, n_j),
        in_specs=in_specs,
        out_specs=pl.BlockSpec((tm, D_MODEL), lambda i, j: (i, 0)),
        out_shape=jax.ShapeDtypeStruct((T_ALL, D_MODEL), F32),
        scratch_shapes=[pltpu.VMEM((tm, D_MODEL), BF16), pltpu.VMEM((tm, D_MODEL), F32)],
        compiler_params=_cparams("parallel", "arbitrary"),
        name="ffn",
    )(*args)


def kernel(x_prompt, x_sample, c, cache_k_b, cache_v_b, cache_k_c, cache_v_c, state_ret, c_ctx, ada_w, ada_b, ln_g, ln_b, w_in_ab, hy_conv_w, hy_conv_b, hf_w1, hf_b1, hf_freq, hf_w2, hf_b2, hf_w3, hy_skip, sink_b, w_out_ab, ffn_w1, ffn_w3, ffn_w2, w_in_cd, qn_g, kn_g, ret_decay, w_out_cd, moe_router, moe_w1, moe_w3, moe_w2):
    x_in = (x_prompt.reshape(T_PROMPT, D_MODEL), x_sample.reshape(T_SAMPLE, D_MODEL))
    cvec = jnp.concatenate([c_ctx[None], c, jnp.zeros((MOD_ROWS - 1 - DEC_BATCH, D_MODEL), F32)], 0)
    mod = _modulation(cvec, ada_w, ada_b)
    kw = N_KV * HEAD_DIM
    prompt = dict(row0=0, n_seq=BATCH, l=SEQ)
    sample = dict(row0=T_PROMPT, n_seq=DEC_BATCH, l=DEC_SEQ)

    u, k_b, v_b = _inproj(x_in, mod[0], w_in_ab[0], tn=1152, taps=(AB_K, AB_V))
    filt_args = (hf_w1[0], hf_b1[0], hf_freq[0], hf_w2[0], hf_b2[0], hf_w3[0])
    hy_args = (hy_conv_w[0], hy_conv_b[0], hy_skip[0])
    blank = lambda w: jnp.zeros((T_ALL, w), BF16)
    ya = _hyena(u, 0, BATCH, SEQ, *hy_args, _hyena_filter(SEQ, *filt_args), into=blank(HY_W))
    ya = _hyena(u, T_PROMPT // DEC_SEQ, DEC_BATCH, DEC_SEQ, *hy_args, _hyena_filter(DEC_SEQ, *filt_args), into=ya)
    cols_b = dict(q_col=AB_Q, k_col=AB_K, v_col=AB_V)
    ctx_b = (cache_k_b[:, 0].reshape(DEC_BATCH, PAST_LEN, kw), cache_v_b[:, 0].reshape(DEC_BATCH, PAST_LEN, kw))
    yb, = _attention(u, **prompt, **cols_b, qb=SEQ, sink=sink_b[0], into=blank(N_HEADS * HEAD_DIM))
    yb, = _attention(u, **sample, **cols_b, sink=sink_b[0], ctx=ctx_b, rope=True, band=True, into=yb)
    k_b = k_b.reshape(BATCH, 1, SEQ, N_KV, HEAD_DIM)
    v_b = v_b.reshape(BATCH, 1, SEQ, N_KV, HEAD_DIM)
    x = _outproj(ya, yb, w_out_ab[0], x_in, mod[0], ln_g[0, 0], ln_b[0, 0])
    x = _ffn(x, mod[0], ln_g[0, 1], ln_b[0, 1], ffn_w1[0], ffn_w3[0], ffn_w2[0])

    u, v_c = _inproj((x,), mod[1], w_in_cd[0], tn=1664, taps=(CD_V,))
    cols_c = dict(q_col=CD_Q, k_col=CD_K, v_col=CD_V)
    gains = (qn_g[0], kn_g[0])
    ctx_c = (cache_k_c[:, 0].reshape(DEC_BATCH, PAST_LEN, kw), cache_v_c[:, 0].reshape(DEC_BATCH, PAST_LEN, kw))
    yc, k_c = _attention(u, **prompt, **cols_c, qb=SEQ, qk_gain=gains, emit_k=True, into=blank(N_HEADS * HEAD_DIM))
    yc, = _attention(u, **sample, **cols_c, qb=2 * BLOCK, qk_gain=gains, ctx=ctx_c, rope=True, into=yc)
    yd, s_r = _retention(u, ret_decay[0], **prompt, nb=4, emit_state=True, into=blank(RET_HEADS * RET_D))
    yd, = _retention(u, ret_decay[0], **sample, rope=True, s0=state_ret[:, 0], into=yd)
    k_c = k_c.reshape(BATCH, 1, SEQ, N_KV, HEAD_DIM)
    v_c = v_c.reshape(BATCH, 1, SEQ, N_KV, HEAD_DIM)
    x = _outproj(yc, yd, w_out_cd[0], (x,), mod[1], ln_g[1, 0], ln_b[1, 0])
    y_prompt, y_sample = _moe(x, mod[1], ln_g[1, 1], ln_b[1, 1], moe_router[0], moe_w1[0], moe_w3[0], moe_w2[0])

    return (y_prompt.reshape(BATCH, SEQ, D_MODEL), y_sample.reshape(DEC_BATCH, DEC_SEQ, D_MODEL),
            k_b, v_b, k_c, v_c, s_r[:, None])
```

```python
import functools
import math

import numpy as np
import jax
import jax.numpy as jnp
from jax import lax
from jax.experimental import pallas as pl
from jax.experimental.pallas import tpu as pltpu

F32 = jnp.float32
BF16 = jnp.bfloat16

D_MODEL = 1024
BATCH = 16
SEQ = 256
DEC_BATCH = 2
DEC_SEQ = 1024
PAST_LEN = 512
GRID_W = 64
HEAD_DIM = 64
BLOCK = 128
HY_W = 512
POS_BANDS = 16
POS_EMB = 1 + 2 * POS_BANDS
FILT_HID = 64
HY_FAST_DECAY = 0.3
HY_SLOW_DECAY = 1.5
HY_TARGET = 1e-2
N_HEADS = 8
N_KV = 2
GROUPS = N_HEADS // N_KV
RET_HEADS = 4
RET_D = 128
CHUNK = 128
ROPE_BASE = 10000.0
D_FF = 2816
N_EXPERTS = 8
EXPERT_FF = 1408
DEPTH = 2
ALPHA = (2 * DEPTH) ** 0.25
EPS = 1e-6

T_PROMPT = BATCH * SEQ
T_SAMPLE = DEC_BATCH * DEC_SEQ
T_ALL = T_PROMPT + T_SAMPLE
GROUP_ROWS = 1024
N_PROMPT_GROUPS = T_PROMPT // GROUP_ROWS
MOD_ROWS = 16
LANES = 128
VMEM_LIMIT = 58 * 1024 * 1024

AB_Q = 3 * HY_W
AB_K = AB_Q + N_HEADS * HEAD_DIM
AB_V = AB_K + N_KV * HEAD_DIM
IN_AB = AB_V + N_KV * HEAD_DIM
CD_Q = 0
CD_K = N_HEADS * HEAD_DIM
CD_V = CD_K + N_KV * HEAD_DIM
CD_RQ = CD_V + N_KV * HEAD_DIM
CD_RK = CD_RQ + RET_HEADS * RET_D
CD_RV = CD_RK + RET_HEADS * RET_D
CD_GF = CD_RV + RET_HEADS * RET_D
CD_GB = CD_GF + RET_HEADS * RET_D
IN_CD = CD_GB + RET_HEADS * RET_D


def _cparams(*sem):
    return pltpu.CompilerParams(dimension_semantics=sem, vmem_limit_bytes=VMEM_LIMIT)


def _silu(x):
    return x * jax.nn.sigmoid(x)


def _bdot(a, b):
    return jnp.dot(a.astype(BF16), b.astype(BF16), preferred_element_type=F32)


def _bdot_nt(a, b):
    return lax.dot_general(a.astype(BF16), b.astype(BF16), (((1,), (1,)), ((), ())),
                           preferred_element_type=F32)


def _bdot_tn(a, b):
    return lax.dot_general(a.astype(BF16), b.astype(BF16), (((0,), (0,)), ((), ())),
                           preferred_element_type=F32)


def _layer_norm(z, g, b):
    mu = jnp.mean(z, -1, keepdims=True)
    zc = z - mu
    var = jnp.mean(zc * zc, -1, keepdims=True)
    return zc * lax.rsqrt(var + EPS) * g + b


def _fill_rows(kern, in_specs, args, into):
    n_in = len(args)
    in_specs.append(pl.BlockSpec(memory_space=pl.ANY))
    args.append(into)

    def kern_into(*refs):
        return kern(*refs[:n_in], *refs[n_in + 1:])

    return kern_into, {n_in: 0}


def _group_of_tile(i, tm):
    return jnp.maximum(i // (GROUP_ROWS // tm) - (N_PROMPT_GROUPS - 1), 0)


def _rope_tables(n_tokens, d, reps):
    nf = d // 4
    inv = ROPE_BASE ** (-np.arange(nf, dtype=np.float64) / nf)
    pos = np.arange(n_tokens)
    row, col = pos // GRID_W, pos % GRID_W
    ang_r = row[:, None] * inv[None, :]
    ang_c = col[:, None] * inv[None, :]
    zeros = np.zeros_like(ang_r)
    cos = np.concatenate([np.cos(ang_r), np.cos(ang_r), np.cos(ang_c), np.cos(ang_c)], -1)
    sin_a = np.concatenate([-np.sin(ang_r), zeros, -np.sin(ang_c), zeros], -1)
    sin_b = np.concatenate([zeros, np.sin(ang_r), zeros, np.sin(ang_c)], -1)
    tile = lambda a: jnp.asarray(np.tile(a, (1, reps)), F32)
    return tile(cos), tile(sin_a), tile(sin_b)


def _dft_mats(l):
    n = 2 * l
    k = np.arange(l, dtype=np.float64)
    ang = 2.0 * np.pi * np.outer(k, k) / n
    fc = np.cos(ang)
    fs = np.sin(ang)
    fs[0, :] = np.cos(np.pi * k)
    fwd = np.concatenate([fc, fs], 0)
    wk = np.full((l,), 2.0)
    wk[0] = 1.0
    inv = np.concatenate([fc.T * wk[None, :], fs.T * wk[None, :]], 1)
    return jnp.asarray(fwd, F32), jnp.asarray(inv, F32)


def _filter_consts(l):
    t = np.linspace(0.0, 1.0, l, dtype=np.float32).astype(np.float64)[:, None]
    w = (2.0 * math.pi * np.arange(l, dtype=np.float64) / l)[:, None]
    bands = np.linspace(1e-4, POS_BANDS - 1.0, POS_BANDS, dtype=np.float32).astype(np.float64)[None, :]
    z = np.concatenate([t, np.cos(bands * w), -np.sin(bands * w)], -1)
    z = np.pad(z, ((0, 0), (0, LANES - POS_EMB)))
    max_decay = math.log(HY_TARGET) / HY_FAST_DECAY
    min_decay = math.log(HY_TARGET) / HY_SLOW_DECAY
    deltas = np.linspace(min_decay, max_decay, HY_W, dtype=np.float32).astype(np.float64)
    window = np.exp(-t * np.abs(deltas)[None, :])
    return jnp.asarray(z, F32), jnp.asarray(window, F32)


def _mod_kernel(c_ref, w_ref, b_ref, o_ref):
    o_ref[...] = _bdot(_silu(c_ref[...]), w_ref[...]) + b_ref[...]


def _modulation(cvec, ada_w, ada_b):
    tn = 1536
    n = ada_w.shape[-1]
    return pl.pallas_call(
        _mod_kernel,
        grid=(DEPTH, n // tn),
        in_specs=[pl.BlockSpec((MOD_ROWS, D_MODEL), lambda l, j: (0, 0)),
                  pl.BlockSpec((None, D_MODEL, tn), lambda l, j: (l, 0, j)),
                  pl.BlockSpec((None, 1, tn), lambda l, j: (l, 0, j))],
        out_specs=pl.BlockSpec((None, MOD_ROWS, tn), lambda l, j: (l, 0, j)),
        out_shape=jax.ShapeDtypeStruct((DEPTH, MOD_ROWS, n), F32),
        compiler_params=_cparams("parallel", "parallel"),
        name="mod",
    )(cvec, ada_w, ada_b.reshape(DEPTH, 1, n))


def _slab_tiles(xs, tm):
    ends = np.cumsum([x.shape[0] // tm for x in xs])
    return [(int(e - x.shape[0] // tm), int(e)) for x, e in zip(xs, ends)]


def _inproj_kernel(*refs, tm, tn, slabs, taps):
    x_refs = refs[:len(slabs)]
    sh_ref, sc_ref, w_ref, o_ref = refs[len(slabs):len(slabs) + 4]
    tap_refs = refs[len(slabs) + 4:len(slabs) + 4 + len(taps)]
    h_scr, w_scr = refs[len(slabs) + 4 + len(taps):]
    j, i = pl.program_id(0), pl.program_id(1)
    rows = pl.ds(pl.multiple_of(i * tm, tm), tm)
    g = _group_of_tile(i, tm)

    for x_ref, (first, end) in zip(x_refs, slabs):
        @pl.when((j == 0) & (i >= first) & (i < end))
        def _():
            sc = sc_ref[pl.ds(g, 1), :]
            sh = sh_ref[pl.ds(g, 1), :]
            h_scr[rows, :] = (x_ref[...] * (1.0 + sc) + sh).astype(BF16)

    @pl.when(i == 0)
    def _():
        w_scr[...] = w_ref[...].astype(BF16)

    y = jnp.dot(h_scr[rows, :], w_scr[...], preferred_element_type=F32)
    o_ref[...] = y

    for tap_ref, col in zip(tap_refs, taps):
        @pl.when((j == col // tn) & (i < T_PROMPT // tm))
        def _():
            tap_ref[...] = y[:, col % tn:col % tn + tap_ref.shape[1]]


def _inproj(xs, mod, w, tn, taps):
    tm = 1024
    n = w.shape[1]
    slabs = _slab_tiles(xs, tm)
    tap_w = N_KV * HEAD_DIM
    last_p = T_PROMPT // tm - 1

    def tap_spec(col):
        jt = col // tn
        return pl.BlockSpec((tm, tap_w), lambda j, i: (
            jnp.where(j < jt, 0, jnp.where(j == jt, jnp.minimum(i, last_p), last_p)), 0))

    def slab_spec(first, end):
        last = end - first - 1
        return pl.BlockSpec((tm, D_MODEL), lambda j, i: (jnp.where(j == 0, jnp.clip(i - first, 0, last), last), 0))

    return pl.pallas_call(
        functools.partial(_inproj_kernel, tm=tm, tn=tn, slabs=slabs, taps=taps),
        grid=(n // tn, T_ALL // tm),
        in_specs=[slab_spec(*s) for s in slabs]
        + [pl.BlockSpec((MOD_ROWS, D_MODEL), lambda j, i: (0, 0)),
           pl.BlockSpec((MOD_ROWS, D_MODEL), lambda j, i: (0, 1)),
           pl.BlockSpec((D_MODEL, tn), lambda j, i: (0, j))],
        out_specs=[pl.BlockSpec((tm, tn), lambda j, i: (i, j))] + [tap_spec(c) for c in taps],
        out_shape=[jax.ShapeDtypeStruct((T_ALL, n), F32)]
        + [jax.ShapeDtypeStruct((T_PROMPT, tap_w), F32) for _ in taps],
        scratch_shapes=[pltpu.VMEM((T_ALL, D_MODEL), BF16), pltpu.VMEM((D_MODEL, tn), BF16)],
        compiler_params=_cparams("arbitrary", "arbitrary"),
        name="inproj",
    )(*xs, mod, mod, w)


def _hyena_filter_kernel(z_ref, w1_ref, b1_ref, fr_ref, w2_ref, b2_ref, w3_ref, win_ref, fwd32_ref, inv32_ref,
                         hc_ref, hs_ref, hc2_ref, fwd_ref, inv_ref, *, l):
    hi = lax.Precision.HIGHEST
    fwd = fwd32_ref[...].astype(BF16)
    fwd_ref[...] = fwd
    inv_ref[...] = inv32_ref[...].astype(BF16)
    fr = fr_ref[...]
    h = jnp.sin(fr * (jnp.dot(z_ref[...], w1_ref[...], precision=hi, preferred_element_type=F32) + b1_ref[...]))
    h = jnp.sin(fr * (jnp.dot(h, w2_ref[...], precision=hi, preferred_element_type=F32) + b2_ref[...]))
    h = jnp.dot(h, w3_ref[...], precision=hi, preferred_element_type=F32)
    win = win_ref[...]
    hf = h[:, :HY_W] * win
    hb = h[:, HY_W:] * win
    pq = _bdot(fwd, jnp.concatenate([hf + hb, hf - hb], 1))
    p = pq[:, :HY_W]
    q = pq[:, HY_W:]
    row0 = lax.broadcasted_iota(jnp.int32, (l, 1), 0) == 0
    hc = p[:l]
    hc_ref[...] = hc
    hs_ref[...] = jnp.where(row0, 0.0, q[l:])
    hc2_ref[...] = jnp.where(row0, p[l:l + 1], hc)


def _hyena_filter(l, fw1, fb1, ffreq, fw2, fb2, fw3):
    z, window = _filter_consts(l)
    fwd, inv = _dft_mats(l)
    pad_c = LANES - FILT_HID
    w1 = jnp.pad(fw1, ((0, LANES - POS_EMB), (0, pad_c)))
    w2 = jnp.pad(fw2, ((0, pad_c), (0, pad_c)))
    w3 = jnp.pad(fw3, ((0, pad_c), (0, 0)))
    row = lambda a: jnp.pad(a, (0, pad_c)).reshape(1, LANES)
    shp = jax.ShapeDtypeStruct((l, HY_W), F32)
    return pl.pallas_call(
        functools.partial(_hyena_filter_kernel, l=l),
        out_shape=(shp, shp, shp, jax.ShapeDtypeStruct(fwd.shape, BF16), jax.ShapeDtypeStruct(inv.shape, BF16)),
        compiler_params=pltpu.CompilerParams(vmem_limit_bytes=VMEM_LIMIT),
        name=f"hyena_filter_{l}",
    )(z, w1, row(fb1), row(ffreq), w2, row(fb2), w3, window, fwd, inv)


def _hyena_kernel(u_ref, cw_ref, cb_ref, skip_ref, fwd_ref, inv_ref, hc_ref, hs_ref, hc2_ref, o_ref, *, l):
    u = u_ref[...]
    rows = lax.broadcasted_iota(jnp.int32, (l, 1), 0)
    prev = jnp.where(rows == 0, 0.0, pltpu.roll(u, 1, 0))
    nxt = jnp.where(rows == l - 1, 0.0, pltpu.roll(u, l - 1, 0))
    uc = prev * cw_ref[0:1, :] + u * cw_ref[1:2, :] + nxt * cw_ref[2:3, :] + cb_ref[...]
    x0 = uc[:, :HY_W]
    x1 = uc[:, HY_W:2 * HY_W]
    v = uc[:, 2 * HY_W:] * x1
    ab = _bdot(fwd_ref[...], v)
    a, b = ab[:l], ab[l:]
    hs = hs_ref[...]
    re = a * hc_ref[...] - b * hs
    im = a * hs + b * hc2_ref[...]
    y = _bdot(inv_ref[...], jnp.concatenate([re, im], 0)) * (1.0 / (2 * l))
    o_ref[...] = ((y + skip_ref[...] * v) * x0).astype(o_ref.dtype)


def _hyena(u_all, row_block0, n_seq, l, conv_w, conv_b, skip, filt, into):
    hc, hs, hc2, fwd, inv = filt
    const = lambda shape: pl.BlockSpec(shape, lambda s: (0, 0))
    in_specs = [pl.BlockSpec((l, 3 * HY_W), lambda s: (row_block0 + s, 0)),
                const((3, 3 * HY_W)), const((1, 3 * HY_W)), const((1, HY_W)),
                const((2 * l, l)), const((l, 2 * l)),
                const((l, HY_W)), const((l, HY_W)), const((l, HY_W))]
    args = [u_all, conv_w, conv_b.reshape(1, -1), skip.reshape(1, -1), fwd, inv, hc, hs, hc2]
    kern, alias = _fill_rows(functools.partial(_hyena_kernel, l=l), in_specs, args, into)
    return pl.pallas_call(
        kern,
        grid=(n_seq,),
        in_specs=in_specs,
        out_specs=pl.BlockSpec((l, HY_W), lambda s: (row_block0 + s, 0)),
        out_shape=jax.ShapeDtypeStruct((T_ALL, HY_W), BF16),
        input_output_aliases=alias,
        compiler_params=_cparams("parallel"),
        name=f"hyena_{l}",
    )(*args)


def _seg_rms_norm(x, bd_ref, g):
    sq = x * x
    hi = sq.astype(BF16)
    lo = (sq - hi.astype(F32)).astype(BF16)
    bd = bd_ref[...]
    ss = (jnp.dot(hi, bd, preferred_element_type=F32) + jnp.dot(lo, bd, preferred_element_type=F32))
    return x * lax.rsqrt(ss * (1.0 / HEAD_DIM) + EPS) * g


def _rope(x, cos, sin_a, sin_b, quarter):
    w = x.shape[-1]
    return x * cos + pltpu.roll(x, w - quarter, 1) * sin_a + pltpu.roll(x, quarter, 1) * sin_b


def _attn_kernel(*refs, l, qb, lc, rope, qknorm, band, has_sink, emit_k):
    it = iter(refs)
    q_ref, k_ref, v_ref = next(it), next(it), next(it)
    if lc:
        ck_ref, cv_ref = next(it), next(it)
    if rope:
        cq_ref, saq_ref, sbq_ref = next(it), next(it), next(it)
        ckk_ref, sak_ref, sbk_ref = next(it), next(it), next(it)
    if qknorm:
        qg_ref, kg_ref, bdq_ref, bdk_ref = next(it), next(it), next(it), next(it)
    if has_sink:
        sink_ref = next(it)
    o_ref = next(it)
    if emit_k:
        kout_ref = next(it)
    kp_scr = next(it)

    qi = pl.program_id(1)

    @pl.when(qi == 0)
    def _():
        k = k_ref[...]
        if qknorm:
            k = _seg_rms_norm(k, bdk_ref, kg_ref[...])
        if emit_k:
            kout_ref[...] = k
        if rope:
            k = _rope(k, ckk_ref[...], sak_ref[...], sbk_ref[...], HEAD_DIM // 4)
        for kv in range(N_KV):
            kp_scr[kv] = k[:, kv * HEAD_DIM:(kv + 1) * HEAD_DIM].astype(BF16)

    q = q_ref[...]
    if qknorm:
        q = _seg_rms_norm(q, bdq_ref, qg_ref[...])
    if rope:
        q = _rope(q, cq_ref[...], saq_ref[...], sbq_ref[...], HEAD_DIM // 4)
    q = q * HEAD_DIM ** -0.5

    rows = GROUPS * qb
    if band:
        n_loc = 3 * BLOCK
        start = pl.multiple_of(jnp.clip((qi - 1) * BLOCK, 0, l - n_loc), BLOCK)
        keys = pl.ds(start, n_loc)
        tq = qi * qb + lax.broadcasted_iota(jnp.int32, (rows, 1), 0) % qb
        tk = start + lax.broadcasted_iota(jnp.int32, (1, n_loc), 1)
        valid = jnp.abs(tq - tk) <= BLOCK
    else:
        keys = slice(None)

    outs = []
    for kv in range(N_KV):
        lanes = slice(kv * HEAD_DIM, (kv + 1) * HEAD_DIM)
        qs = jnp.concatenate([q[:, (kv * GROUPS + g) * HEAD_DIM:(kv * GROUPS + g + 1) * HEAD_DIM]
                              for g in range(GROUPS)], 0)
        s = _bdot_nt(qs, kp_scr[kv, keys, :])
        if band:
            s = jnp.where(valid, s, -jnp.inf)
        if lc:
            s_c = _bdot_nt(qs, ck_ref[:, lanes])
        e_parts, ec_parts, dens = [], [], []
        for g in range(GROUPS):
            r = slice(g * qb, (g + 1) * qb)
            m = jnp.max(s[r], -1, keepdims=True)
            if lc:
                m = jnp.maximum(m, jnp.max(s_c[r], -1, keepdims=True))
            if has_sink:
                sink = sink_ref[kv * GROUPS + g]
                m = jnp.maximum(m, sink)
            e = jnp.exp(s[r] - m)
            den = jnp.sum(e, -1, keepdims=True)
            e_parts.append(e)
            if lc:
                e_c = jnp.exp(s_c[r] - m)
                den = den + jnp.sum(e_c, -1, keepdims=True)
                ec_parts.append(e_c)
            if has_sink:
                den = den + jnp.exp(sink - m)
            dens.append(den)
        o = _bdot(jnp.concatenate(e_parts, 0), v_ref[keys, lanes])
        if lc:
            o = o + _bdot(jnp.concatenate(ec_parts, 0), cv_ref[:, lanes])
        outs.extend(o[g * qb:(g + 1) * qb] / dens[g] for g in range(GROUPS))
    o_ref[...] = jnp.concatenate(outs, 1).astype(o_ref.dtype)


def _attention(u, *, row0, n_seq, l, q_col, k_col, v_col, into, qb=BLOCK, ctx=None, rope=False, qk_gain=None,
               band=False, sink=None, emit_k=False):
    nq = l // qb
    qw = N_HEADS * HEAD_DIM
    kw = N_KV * HEAD_DIM
    qb0, sb0 = row0 // qb, row0 // l
    seq_spec = lambda col: pl.BlockSpec((l, kw), lambda b, i: (sb0 + b, col // kw))
    const = lambda shape: pl.BlockSpec(shape, lambda b, i: (0, 0))
    in_specs = [pl.BlockSpec((qb, qw), lambda b, i: (qb0 + b * nq + i, q_col // qw)),
                seq_spec(k_col), seq_spec(v_col)]
    args = [u, u, u]
    lc = 0
    if ctx is not None:
        lc = ctx[0].shape[1]
        in_specs += [pl.BlockSpec((None, lc, kw), lambda b, i: (b, 0, 0))] * 2
        args += list(ctx)
    if rope:
        tabs = _rope_tables(l, HEAD_DIM, N_HEADS)
        in_specs += [pl.BlockSpec((qb, qw), lambda b, i: (i, 0))] * 3 + [const((l, kw))] * 3
        args += list(tabs) + list(tabs)
    if qk_gain is not None:
        bd = np.kron(np.eye(N_HEADS), np.ones((HEAD_DIM, HEAD_DIM)))
        in_specs += [const((1, qw)), const((1, kw)), const((qw, qw)), const((kw, kw))]
        args += [jnp.tile(qk_gain[0], N_HEADS).reshape(1, qw), jnp.tile(qk_gain[1], N_KV).reshape(1, kw),
                 jnp.asarray(bd, BF16), jnp.asarray(bd[:kw, :kw], BF16)]
    if sink is not None:
        in_specs.append(pl.BlockSpec(memory_space=pltpu.SMEM))
        args.append(sink)
    out_specs = [pl.BlockSpec((qb, qw), lambda b, i: (qb0 + b * nq + i, 0))]
    out_shape = [jax.ShapeDtypeStruct((T_ALL, qw), BF16)]
    if emit_k:
        out_specs.append(pl.BlockSpec((l, kw), lambda b, i: (b, 0)))
        out_shape.append(jax.ShapeDtypeStruct((n_seq * l, kw), F32))
    kern = functools.partial(_attn_kernel, l=l, qb=qb, lc=lc, rope=rope, qknorm=qk_gain is not None, band=band,
                             has_sink=sink is not None, emit_k=emit_k)
    kern, alias = _fill_rows(kern, in_specs, args, into)
    return pl.pallas_call(
        kern,
        grid=(n_seq, nq),
        in_specs=in_specs,
        out_specs=out_specs,
        out_shape=out_shape,
        input_output_aliases=alias,
        scratch_shapes=[pltpu.VMEM((N_KV, l, HEAD_DIM), BF16)],
        compiler_params=_cparams("parallel", "arbitrary"),
        name=f"attn_{l}_{'b' if sink is not None else 'c'}",
    )(*args)


def _ret_kernel(*refs, l, nb, rope, has_s0, emit_state):
    it = iter(refs)
    dec_ref = next(it)
    rq_ref, rk_ref, rv_ref, gf_ref, gb_ref = next(it), next(it), next(it), next(it), next(it)
    if rope:
        cos_ref, sa_ref, sb_ref = next(it), next(it), next(it)
    if has_s0:
        s0_ref = next(it)
    o_ref = next(it)
    if emit_state:
        st_ref = next(it)

    h = pl.program_id(1)
    qs, ks, vs = [], [], []
    for s in range(nb):
        seq = slice(s * l, (s + 1) * l)
        q = rq_ref[seq, :] * RET_D ** -0.5
        k = rk_ref[seq, :]
        if rope:
            q = _rope(q, cos_ref[...], sa_ref[...], sb_ref[...], RET_D // 4)
            k = _rope(k, cos_ref[...], sa_ref[...], sb_ref[...], RET_D // 4)
        qs.append(q)
        ks.append(k)
        vs.append(rv_ref[seq, :])
    n = l // CHUNK
    ii = lax.broadcasted_iota(jnp.int32, (CHUNK, 1), 0).astype(F32)
    jj = lax.broadcasted_iota(jnp.int32, (1, CHUNK), 1).astype(F32)
    diff = ii - jj
    ys = [None] * nb
    for d in range(2):
        log_g = jnp.log(jax.nn.sigmoid(jnp.full((1, 1), dec_ref[d, h], F32)))
        if d == 0:
            mask = jnp.exp(jnp.where(diff >= 0, diff * log_g, -jnp.inf))
            q_dec = jnp.exp((ii + 1.0) * log_g)
            k_dec = jnp.exp((CHUNK - 1.0 - ii) * log_g)
            order = range(n)
        else:
            mask = jnp.exp(jnp.where(diff <= 0, -diff * log_g, -jnp.inf))
            q_dec = jnp.exp((CHUNK - ii) * log_g)
            k_dec = jnp.exp(ii * log_g)
            order = reversed(range(n))
        c_dec = jnp.exp(CHUNK * log_g)
        order = list(order)
        for s in range(nb):
            seq = slice(s * l, (s + 1) * l)
            q, k, v = qs[s], ks[s], vs[s]
            state = s0_ref[s, d] if has_s0 else jnp.zeros((RET_D, RET_D), F32)
            o_chunks = [None] * n
            for c in order:
                sl = slice(c * CHUNK, (c + 1) * CHUNK)
                qc, kc, vc = q[sl], k[sl], v[sl]
                inner = _bdot_nt(qc, kc) * mask
                o_chunks[c] = _bdot(inner, vc) + _bdot(qc * q_dec, state)
                state = state * c_dec + _bdot_tn(kc * k_dec, vc)
            if emit_state:
                st_ref[s, d] = state
            o = jnp.concatenate(o_chunks, 0)
            o = o * lax.rsqrt(jnp.mean(o * o, -1, keepdims=True) + EPS)
            gate = _silu((gf_ref if d == 0 else gb_ref)[seq, :])
            ys[s] = gate * o if ys[s] is None else ys[s] + gate * o
    for s in range(nb):
        o_ref[s * l:(s + 1) * l, :] = ys[s].astype(o_ref.dtype)


def _retention(u, ret_decay, *, row0, n_seq, l, into, nb=1, rope=False, s0=None, emit_state=False):
    sb0 = row0 // (nb * l)
    col = lambda c0: pl.BlockSpec((nb * l, RET_D), lambda b, h: (sb0 + b, c0 // RET_D + h))
    in_specs = [pl.BlockSpec(memory_space=pltpu.SMEM),
                col(CD_RQ), col(CD_RK), col(CD_RV), col(CD_GF), col(CD_GB)]
    args = [ret_decay, u, u, u, u, u]
    if rope:
        in_specs += [pl.BlockSpec((l, RET_D), lambda b, h: (0, 0))] * 3
        args += list(_rope_tables(l, RET_D, 1))
    state_spec = pl.BlockSpec((nb, 2, None, RET_D, RET_D), lambda b, h: (b, 0, h, 0, 0))
    if s0 is not None:
        in_specs.append(state_spec)
        args.append(s0)
    out_specs = [pl.BlockSpec((nb * l, RET_D), lambda b, h: (sb0 + b, h))]
    out_shape = [jax.ShapeDtypeStruct((T_ALL, RET_HEADS * RET_D), BF16)]
    if emit_state:
        out_specs.append(state_spec)
        out_shape.append(jax.ShapeDtypeStruct((n_seq, 2, RET_HEADS, RET_D, RET_D), F32))
    kern = functools.partial(_ret_kernel, l=l, nb=nb, rope=rope, has_s0=s0 is not None, emit_state=emit_state)
    kern, alias = _fill_rows(kern, in_specs, args, into)
    return pl.pallas_call(
        kern,
        grid=(n_seq // nb, RET_HEADS),
        in_specs=in_specs,
        out_specs=out_specs,
        out_shape=out_shape,
        input_output_aliases=alias,
        compiler_params=_cparams("parallel", "parallel"),
        name=f"retention_{l}",
    )(*args)


def _outproj_kernel(*refs, tm, slabs):
    ya_ref, yb_ref, w_ref = refs[:3]
    x_refs = refs[3:3 + len(slabs)]
    gate_ref, lng_ref, lnb_ref, o_ref, w_scr = refs[3 + len(slabs):]
    i = pl.program_id(0)

    @pl.when(i == 0)
    def _():
        w_scr[...] = w_ref[...].astype(BF16)

    half = ya_ref.shape[1]
    m = (jnp.dot(ya_ref[...], w_scr[:half], preferred_element_type=F32)
         + jnp.dot(yb_ref[...], w_scr[half:], preferred_element_type=F32))
    gm = gate_ref[pl.ds(_group_of_tile(i, tm), 1), :] * m

    for x_ref, (first, end) in zip(x_refs, slabs):
        @pl.when((i >= first) & (i < end))
        def _():
            o_ref[...] = _layer_norm(ALPHA * x_ref[...] + gm, lng_ref[...], lnb_ref[...])


def _outproj(ya, yb, w, xs, mod, ln_g, ln_b):
    tm = 1024
    half = ya.shape[1]
    slabs = _slab_tiles(xs, tm)
    const = lambda shape: pl.BlockSpec(shape, lambda i: (0, 0))

    def slab_spec(first, end):
        return pl.BlockSpec((tm, D_MODEL), lambda i: (jnp.clip(i - first, 0, end - first - 1), 0))

    return pl.pallas_call(
        functools.partial(_outproj_kernel, tm=tm, slabs=slabs),
        grid=(T_ALL // tm,),
        in_specs=[pl.BlockSpec((tm, half), lambda i: (i, 0)),
                  pl.BlockSpec((tm, half), lambda i: (i, 0)),
                  const((2 * half, D_MODEL))]
        + [slab_spec(*s) for s in slabs]
        + [pl.BlockSpec((MOD_ROWS, D_MODEL), lambda i: (0, 2)),
           const((1, D_MODEL)), const((1, D_MODEL))],
        out_specs=pl.BlockSpec((tm, D_MODEL), lambda i: (i, 0)),
        out_shape=jax.ShapeDtypeStruct((T_ALL, D_MODEL), F32),
        scratch_shapes=[pltpu.VMEM((2 * half, D_MODEL), BF16)],
        compiler_params=_cparams("arbitrary"),
        name="outproj_ln",
    )(ya, yb, w, *xs, mod, ln_g.reshape(1, -1), ln_b.reshape(1, -1))


TOK_TILE = 256
N_TOK_TILES = T_ALL // TOK_TILE
SORT_TILE = 256
EXP_TILE = 512
N_EXP_TILES = (2 * T_ALL) // EXP_TILE + N_EXPERTS
N_PROMPT_TOK_TILES = T_PROMPT // TOK_TILE
CUM_ROWS = 32


def _route_kernel(x_ref, sh_ref, sc_ref, r_ref, h_ref, rank_ref, rank_t_ref, gate_t_ref, cum_ref,
                  carry_row, carry_col):
    c = pl.program_id(0)

    @pl.when(c == 0)
    def _():
        carry_row[...] = jnp.zeros_like(carry_row)
        carry_col[...] = jnp.zeros_like(carry_col)
        cum_ref[...] = jnp.zeros_like(cum_ref)

    g = _group_of_tile(c, TOK_TILE)
    h = x_ref[...] * (1.0 + sc_ref[pl.ds(g, 1), :]) + sh_ref[pl.ds(g, 1), :]
    h_ref[...] = h.astype(BF16)
    logits = jnp.dot(h, r_ref[...], precision=lax.Precision.HIGHEST, preferred_element_type=F32)
    lane = lax.broadcasted_iota(jnp.int32, logits.shape, 1)
    logits = jnp.where(lane < N_EXPERTS, logits, -jnp.inf)
    m1 = jnp.max(logits, -1, keepdims=True)
    i1 = jnp.min(jnp.where(logits == m1, lane, LANES), -1, keepdims=True)
    rest = jnp.where(lane == i1, -jnp.inf, logits)
    m2 = jnp.max(rest, -1, keepdims=True)
    i2 = jnp.min(jnp.where(rest == m2, lane, LANES), -1, keepdims=True)
    e2 = jnp.exp(m2 - m1)
    den = 1.0 + e2
    gates = jnp.where(lane == i1, 1.0 / den, 0.0) + jnp.where(lane == i2, e2 / den, 0.0)
    sel = jnp.where((lane == i1) | (lane == i2), 1.0, 0.0)
    sel_t = sel.T
    ti = lax.broadcasted_iota(jnp.int32, (TOK_TILE, TOK_TILE), 0)
    tj = lax.broadcasted_iota(jnp.int32, (TOK_TILE, TOK_TILE), 1)
    before = jnp.where(tj < ti, 1.0, 0.0).astype(BF16)
    rank = jnp.dot(before, sel.astype(BF16), preferred_element_type=F32) + carry_row[...]
    rank_t = lax.dot_general(sel_t.astype(BF16), before, (((1,), (1,)), ((), ())),
                             preferred_element_type=F32) + carry_col[...]
    rank_ref[...] = jnp.where(sel > 0.0, rank, -1.0)
    rank_t_ref[...] = jnp.where(sel_t > 0.0, rank_t, -1.0)[:N_EXPERTS]
    gate_t_ref[...] = gates.T[:N_EXPERTS]
    cum_ref[pl.ds(c, 1), :] = carry_row[...].astype(jnp.int32)
    carry_row[...] += jnp.sum(sel, 0, keepdims=True)
    carry_col[...] += jnp.sum(sel_t, 1, keepdims=True)

    @pl.when(c == N_TOK_TILES - 1)
    def _():
        cum_ref[pl.ds(N_TOK_TILES, 1), :] = carry_row[...].astype(jnp.int32)


def _route(x, mod, router):
    tile = lambda w: pl.BlockSpec((TOK_TILE, w), lambda c: (c, 0))
    tile_t = pl.BlockSpec((N_EXPERTS, TOK_TILE), lambda c: (0, c))
    return pl.pallas_call(
        _route_kernel,
        grid=(N_TOK_TILES,),
        in_specs=[tile(D_MODEL),
                  pl.BlockSpec((MOD_ROWS, D_MODEL), lambda c: (0, 3)),
                  pl.BlockSpec((MOD_ROWS, D_MODEL), lambda c: (0, 4)),
                  pl.BlockSpec((D_MODEL, LANES), lambda c: (0, 0))],
        out_specs=[tile(D_MODEL), tile(LANES), tile_t, tile_t,
                   pl.BlockSpec((CUM_ROWS, LANES), lambda c: (0, 0))],
        out_shape=[jax.ShapeDtypeStruct((T_ALL, D_MODEL), BF16),
                   jax.ShapeDtypeStruct((T_ALL, LANES), F32),
                   jax.ShapeDtypeStruct((N_EXPERTS, T_ALL), F32),
                   jax.ShapeDtypeStruct((N_EXPERTS, T_ALL), F32),
                   jax.ShapeDtypeStruct((CUM_ROWS, LANES), jnp.int32)],
        scratch_shapes=[pltpu.VMEM((1, LANES), F32), pltpu.VMEM((LANES, 1), F32)],
        compiler_params=_cparams("arbitrary"),
        name="route",
    )(x, mod, mod, jnp.pad(router, ((0, 0), (0, LANES - N_EXPERTS))))


def _gather_kernel(te_ref, off_ref, cum_ref, nt_ref, h_ref, rank_t_ref, gate_t_ref, xs_ref, gs_ref, acc_scr, g_scr):
    i = pl.program_id(0)
    e = te_ref[i]
    r0 = i * EXP_TILE - off_ref[e]
    acc_scr[...] = jnp.zeros_like(acc_scr)
    g_scr[...] = jnp.zeros_like(g_scr)
    want = (r0 + lax.broadcasted_iota(jnp.int32, (EXP_TILE, 1), 0)).astype(F32)
    for c in range(N_TOK_TILES):
        lo = cum_ref[c * N_EXPERTS + e]
        hi = cum_ref[(c + 1) * N_EXPERTS + e]

        @pl.when((i < nt_ref[0]) & (lo < r0 + EXP_TILE) & (hi > r0))
        def _():
            cols = slice(c * TOK_TILE, (c + 1) * TOK_TILE)
            pick = rank_t_ref[pl.ds(e, 1), cols] == want
            acc_scr[...] += jnp.dot(jnp.where(pick, 1.0, 0.0).astype(BF16), h_ref[cols, :],
                                    preferred_element_type=F32)
            g_scr[...] += jnp.sum(jnp.where(pick, gate_t_ref[pl.ds(e, 1), cols], 0.0), -1, keepdims=True)

    xs_ref[...] = acc_scr[...].astype(BF16)
    gs_ref[...] = jnp.broadcast_to(g_scr[...], gs_ref.shape)


def _gather(tile_expert, off, cum, n_tiles, h, rank_t, gate_t):
    const = lambda shape: pl.BlockSpec(shape, lambda i, *_: (0, 0))
    return pl.pallas_call(
        _gather_kernel,
        grid_spec=pltpu.PrefetchScalarGridSpec(
            num_scalar_prefetch=4,
            grid=(N_EXP_TILES,),
            in_specs=[const((T_ALL, D_MODEL)), const((N_EXPERTS, T_ALL)), const((N_EXPERTS, T_ALL))],
            out_specs=[pl.BlockSpec((EXP_TILE, D_MODEL), lambda i, *_: (i, 0)),
                       pl.BlockSpec((EXP_TILE, LANES), lambda i, *_: (i, 0))],
            scratch_shapes=[pltpu.VMEM((EXP_TILE, D_MODEL), F32), pltpu.VMEM((EXP_TILE, 1), F32)]),
        out_shape=[jax.ShapeDtypeStruct((N_EXP_TILES * EXP_TILE, D_MODEL), BF16),
                   jax.ShapeDtypeStruct((N_EXP_TILES * EXP_TILE, LANES), F32)],
        compiler_params=_cparams("parallel"),
        name="moe_gather",
    )(tile_expert, off, cum, n_tiles, h, rank_t, gate_t)


def _expert_kernel(te_ref, nt_ref, xs_ref, gs_ref, w1_ref, w3_ref, w2_ref, ys_ref, w1_scr, w3_scr, w2_scr, acc_scr):
    i = pl.program_id(0)

    @pl.when((i == 0) | (te_ref[i] != te_ref[jnp.maximum(i - 1, 0)]))
    def _():
        w1_scr[...] = w1_ref[...].astype(BF16)
        w3_scr[...] = w3_ref[...].astype(BF16)
        w2_scr[...] = w2_ref[...].astype(BF16)

    @pl.when(i < nt_ref[0])
    def _():
        x = xs_ref[...]
        gate = gs_ref[:, 0:1]
        acc_scr[...] = jnp.zeros_like(acc_scr)
        for c0 in range(0, EXPERT_FF, FF_CHUNK):
            cs = slice(c0, min(c0 + FF_CHUNK, EXPERT_FF))
            a = jnp.dot(x, w1_scr[:, cs], preferred_element_type=F32)
            b = jnp.dot(x, w3_scr[:, cs], preferred_element_type=F32)
            act = (_silu(a) * b * gate).astype(BF16)
            acc_scr[...] += jnp.dot(act, w2_scr[cs, :], preferred_element_type=F32)
        ys_ref[...] = acc_scr[...].astype(BF16)

    @pl.when(i >= nt_ref[0])
    def _():
        ys_ref[...] = jnp.zeros_like(ys_ref)


def _experts(tile_expert, n_tiles, xs, gs, w1, w3, w2):
    w_in = pl.BlockSpec((None, D_MODEL, EXPERT_FF), lambda i, te, nt: (te[i], 0, 0))
    w_out = pl.BlockSpec((None, EXPERT_FF, D_MODEL), lambda i, te, nt: (te[i], 0, 0))
    return pl.pallas_call(
        _expert_kernel,
        grid_spec=pltpu.PrefetchScalarGridSpec(
            num_scalar_prefetch=2,
            grid=(N_EXP_TILES,),
            in_specs=[pl.BlockSpec((EXP_TILE, D_MODEL), lambda i, te, nt: (i, 0)),
                      pl.BlockSpec((EXP_TILE, LANES), lambda i, te, nt: (i, 0)),
                      w_in, w_in, w_out],
            out_specs=pl.BlockSpec((EXP_TILE, D_MODEL), lambda i, te, nt: (i, 0)),
            scratch_shapes=[pltpu.VMEM((D_MODEL, EXPERT_FF), BF16), pltpu.VMEM((D_MODEL, EXPERT_FF), BF16),
                            pltpu.VMEM((EXPERT_FF, D_MODEL), BF16), pltpu.VMEM((EXP_TILE, D_MODEL), F32)]),
        out_shape=jax.ShapeDtypeStruct((N_EXP_TILES * EXP_TILE, D_MODEL), BF16),
        compiler_params=_cparams("arbitrary"),
        name="moe_experts",
    )(tile_expert, n_tiles, xs, gs, w1, w3, w2)


def _combine_kernel(off_ref, cum_ref, ys_ref, rank_ref, x_ref, gate_ref, lng_ref, lnb_ref, op_ref, os_ref, acc_scr):
    c = pl.program_id(0)
    acc_scr[...] = jnp.zeros_like(acc_scr)
    rank = rank_ref[...]
    lane = lax.broadcasted_iota(jnp.int32, rank.shape, 1)
    cols = lax.broadcasted_iota(jnp.int32, (1, SORT_TILE), 1)
    for e in range(N_EXPERTS):
        lo = off_ref[e] + cum_ref[c * N_EXPERTS + e]
        hi = off_ref[e] + cum_ref[(c + 1) * N_EXPERTS + e]
        r = jnp.sum(jnp.where(lane == e, rank, 0.0), -1, keepdims=True)
        pos = jnp.where(r >= 0.0, r + jnp.full((1, 1), off_ref[e], jnp.int32).astype(F32), -1.0)
        first = lo // SORT_TILE
        for k in range(2):
            s = first + k

            @pl.when((hi > lo) & (s * SORT_TILE < hi))
            def _():
                pick = pos == (s * SORT_TILE + cols).astype(F32)
                rows = ys_ref[pl.ds(pl.multiple_of(s * SORT_TILE, SORT_TILE), SORT_TILE), :]
                acc_scr[...] += jnp.dot(jnp.where(pick, 1.0, 0.0).astype(BF16), rows, preferred_element_type=F32)

    g = _group_of_tile(c, TOK_TILE)
    z = ALPHA * x_ref[...] + gate_ref[pl.ds(g, 1), :] * acc_scr[...]
    y = _layer_norm(z, lng_ref[...], lnb_ref[...])

    @pl.when(c < N_PROMPT_TOK_TILES)
    def _():
        op_ref[...] = y

    @pl.when(c >= N_PROMPT_TOK_TILES)
    def _():
        os_ref[...] = y


def _combine(off, cum, ys, rank, x, mod, ln_g, ln_b):
    const = lambda shape: pl.BlockSpec(shape, lambda c, *_: (0, 0))
    last_p = N_PROMPT_TOK_TILES - 1
    return pl.pallas_call(
        _combine_kernel,
        grid_spec=pltpu.PrefetchScalarGridSpec(
            num_scalar_prefetch=2,
            grid=(N_TOK_TILES,),
            in_specs=[pl.BlockSpec(ys.shape, lambda c, *_: (0, 0), pipeline_mode=pl.Buffered(1)),
                      pl.BlockSpec((TOK_TILE, LANES), lambda c, *_: (c, 0)),
                      pl.BlockSpec((TOK_TILE, D_MODEL), lambda c, *_: (c, 0)),
                      pl.BlockSpec((MOD_ROWS, D_MODEL), lambda c, *_: (0, 5)),
                      const((1, D_MODEL)), const((1, D_MODEL))],
            out_specs=[pl.BlockSpec((TOK_TILE, D_MODEL), lambda c, *_: (jnp.minimum(c, last_p), 0)),
                       pl.BlockSpec((TOK_TILE, D_MODEL), lambda c, *_: (jnp.maximum(c - last_p - 1, 0), 0))],
            scratch_shapes=[pltpu.VMEM((TOK_TILE, D_MODEL), F32)]),
        out_shape=[jax.ShapeDtypeStruct((T_PROMPT, D_MODEL), F32), jax.ShapeDtypeStruct((T_SAMPLE, D_MODEL), F32)],
        compiler_params=_cparams("arbitrary"),
        name="moe_combine_ln",
    )(off, cum, ys, rank, x, mod, ln_g.reshape(1, -1), ln_b.reshape(1, -1))


def _moe(x, mod, ln_g, ln_b, router, w1, w3, w2):
    h, rank, rank_t, gate_t, cum = _route(x, mod, router)
    counts = cum[N_TOK_TILES, :N_EXPERTS]
    tiles = (counts + EXP_TILE - 1) // EXP_TILE
    ends = jnp.cumsum(tiles)
    off = ((ends - tiles) * EXP_TILE).astype(jnp.int32)
    n_tiles = ends[-1:].astype(jnp.int32)
    tile_ids = jnp.minimum(jnp.arange(N_EXP_TILES, dtype=jnp.int32), n_tiles - 1)
    tile_expert = jnp.sum((tile_ids[:, None] >= ends[None, :]).astype(jnp.int32), -1)
    cum_flat = cum[:N_TOK_TILES + 1, :N_EXPERTS].reshape(-1)
    xs, gs = _gather(tile_expert, off, cum_flat, n_tiles, h, rank_t, gate_t)
    ys = _experts(tile_expert, n_tiles, xs, gs, w1, w3, w2)
    return _combine(off, cum_flat, ys, rank, x, mod, ln_g, ln_b)


FF_CHUNK = 256


def _ffn_kernel(x_ref, sh_ref, sc_ref, gate_ref, lng_ref, lnb_ref, w1_ref, w3_ref, w2_ref, o_ref, h_scr, acc_scr,
                *, tm, ff):
    i, j = pl.program_id(0), pl.program_id(1)
    g = _group_of_tile(i, tm)

    @pl.when(j == 0)
    def _():
        h_scr[...] = (x_ref[...] * (1.0 + sc_ref[pl.ds(g, 1), :]) + sh_ref[pl.ds(g, 1), :]).astype(BF16)
        acc_scr[...] = jnp.zeros_like(acc_scr)

    h = h_scr[...]
    for c0 in range(0, ff, FF_CHUNK):
        cs = slice(c0, min(c0 + FF_CHUNK, ff))
        a = jnp.dot(h, w1_ref[:, cs].astype(BF16), preferred_element_type=F32)
        b = jnp.dot(h, w3_ref[:, cs].astype(BF16), preferred_element_type=F32)
        act = _silu(a) * b
        acc_scr[...] += jnp.dot(act.astype(BF16), w2_ref[cs, :].astype(BF16), preferred_element_type=F32)

    @pl.when(j == pl.num_programs(1) - 1)
    def _():
        z = ALPHA * x_ref[...] + gate_ref[pl.ds(g, 1), :] * acc_scr[...]
        o_ref[...] = _layer_norm(z, lng_ref[...], lnb_ref[...])


def _ffn(x, mod, ln_g, ln_b, w1, w3, w2):
    tm = 1024
    ff = FF_CHUNK
    n_j = D_FF // ff
    w_in_spec = pl.BlockSpec((D_MODEL, ff), lambda i, j: (0, j))
    w_out_spec = pl.BlockSpec((ff, D_MODEL), lambda i, j: (j, 0))
    mod_spec = lambda col: pl.BlockSpec((MOD_ROWS, D_MODEL), lambda i, j: (0, col))
    const = lambda shape: pl.BlockSpec(shape, lambda i, j: (0, 0))
    in_specs = [pl.BlockSpec((tm, D_MODEL), lambda i, j: (i, 0)),
                mod_spec(3), mod_spec(4), mod_spec(5),
                const((1, D_MODEL)), const((1, D_MODEL)),
                w_in_spec, w_in_spec, w_out_spec]
    args = [x, mod, mod, mod, ln_g.reshape(1, -1), ln_b.reshape(1, -1), w1, w3, w2]
    return pl.pallas_call(
        functools.partial(_ffn_kernel, tm=tm, ff=ff),
        grid=(T_ALL // tm, n_j),
        in_specs=in_specs,
        out_specs=pl.BlockSpec((tm, D_MODEL), lambda i, j: (i, 0)),
        out_shape=jax.ShapeDtypeStruct((T_ALL, D_MODEL), F32),
        scratch_shapes=[pltpu.VMEM((tm, D_MODEL), BF16), pltpu.VMEM((tm, D_MODEL), F32)],
        compiler_params=_cparams("parallel", "arbitrary"),
        name="ffn",
    )(*args)


def kernel(x_prompt, x_sample, c, cache_k_b, cache_v_b, cache_k_c, cache_v_c, state_ret, c_ctx, ada_w, ada_b, ln_g, ln_b, w_in_ab, hy_conv_w, hy_conv_b, hf_w1, hf_b1, hf_freq, hf_w2, hf_b2, hf_w3, hy_skip, sink_b, w_out_ab, ffn_w1, ffn_w3, ffn_w2, w_in_cd, qn_g, kn_g, ret_decay, w_out_cd, moe_router, moe_w1, moe_w3, moe_w2):
    x_in = (x_prompt.reshape(T_PROMPT, D_MODEL), x_sample.reshape(T_SAMPLE, D_MODEL))
    cvec = jnp.concatenate([c_ctx[None], c, jnp.zeros((MOD_ROWS - 1 - DEC_BATCH, D_MODEL), F32)], 0)
    mod = _modulation(cvec, ada_w, ada_b)
    kw = N_KV * HEAD_DIM
    prompt = dict(row0=0, n_seq=BATCH, l=SEQ)
    sample = dict(row0=T_PROMPT, n_seq=DEC_BATCH, l=DEC_SEQ)

    u, k_b, v_b = _inproj(x_in, mod[0], w_in_ab[0], tn=1152, taps=(AB_K, AB_V))
    filt_args = (hf_w1[0], hf_b1[0], hf_freq[0], hf_w2[0], hf_b2[0], hf_w3[0])
    hy_args = (hy_conv_w[0], hy_conv_b[0], hy_skip[0])
    blank = lambda w: jnp.zeros((T_ALL, w), BF16)
    ya = _hyena(u, 0, BATCH, SEQ, *hy_args, _hyena_filter(SEQ, *filt_args), into=blank(HY_W))
    ya = _hyena(u, T_PROMPT // DEC_SEQ, DEC_BATCH, DEC_SEQ, *hy_args, _hyena_filter(DEC_SEQ, *filt_args), into=ya)
    cols_b = dict(q_col=AB_Q, k_col=AB_K, v_col=AB_V)
    ctx_b = (cache_k_b[:, 0].reshape(DEC_BATCH, PAST_LEN, kw), cache_v_b[:, 0].reshape(DEC_BATCH, PAST_LEN, kw))
    yb, = _attention(u, **prompt, **cols_b, qb=SEQ, sink=sink_b[0], into=blank(N_HEADS * HEAD_DIM))
    yb, = _attention(u, **sample, **cols_b, sink=sink_b[0], ctx=ctx_b, rope=True, band=True, into=yb)
    k_b = k_b.reshape(BATCH, 1, SEQ, N_KV, HEAD_DIM)
    v_b = v_b.reshape(BATCH, 1, SEQ, N_KV, HEAD_DIM)
    x = _outproj(ya, yb, w_out_ab[0], x_in, mod[0], ln_g[0, 0], ln_b[0, 0])
    x = _ffn(x, mod[0], ln_g[0, 1], ln_b[0, 1], ffn_w1[0], ffn_w3[0], ffn_w2[0])

    u, v_c = _inproj((x,), mod[1], w_in_cd[0], tn=1664, taps=(CD_V,))
    cols_c = dict(q_col=CD_Q, k_col=CD_K, v_col=CD_V)
    gains = (qn_g[0], kn_g[0])
    ctx_c = (cache_k_c[:, 0].reshape(DEC_BATCH, PAST_LEN, kw), cache_v_c[:, 0].reshape(DEC_BATCH, PAST_LEN, kw))
    yc, k_c = _attention(u, **prompt, **cols_c, qb=SEQ, qk_gain=gains, emit_k=True, into=blank(N_HEADS * HEAD_DIM))
    yc, = _attention(u, **sample, **cols_c, qb=2 * BLOCK, qk_gain=gains, ctx=ctx_c, rope=True, into=yc)
    yd, s_r = _retention(u, ret_decay[0], **prompt, nb=4, emit_state=True, into=blank(RET_HEADS * RET_D))
    yd, = _retention(u, ret_decay[0], **sample, nb=DEC_BATCH, rope=True, s0=state_ret[:, 0], into=yd)
    k_c = k_c.reshape(BATCH, 1, SEQ, N_KV, HEAD_DIM)
    v_c = v_c.reshape(BATCH, 1, SEQ, N_KV, HEAD_DIM)
    x = _outproj(yc, yd, w_out_cd[0], (x,), mod[1], ln_g[1, 0], ln_b[1, 0])
    y_prompt, y_sample = _moe(x, mod[1], ln_g[1, 1], ln_b[1, 1], moe_router[0], moe_w1[0], moe_w3[0], moe_w2[0])

    return (y_prompt.reshape(BATCH, SEQ, D_MODEL), y_sample.reshape(DEC_BATCH, DEC_SEQ, D_MODEL),
            k_b, v_b, k_c, v_c, s_r[:, None])
```

```python
import functools
import math

import numpy as np
import jax
import jax.numpy as jnp
from jax import lax
from jax.experimental import pallas as pl
from jax.experimental.pallas import tpu as pltpu

F32 = jnp.float32
BF16 = jnp.bfloat16

D_MODEL = 1024
BATCH = 16
SEQ = 256
DEC_BATCH = 2
DEC_SEQ = 1024
PAST_LEN = 512
GRID_W = 64
HEAD_DIM = 64
BLOCK = 128
HY_W = 512
POS_BANDS = 16
POS_EMB = 1 + 2 * POS_BANDS
FILT_HID = 64
HY_FAST_DECAY = 0.3
HY_SLOW_DECAY = 1.5
HY_TARGET = 1e-2
N_HEADS = 8
N_KV = 2
GROUPS = N_HEADS // N_KV
RET_HEADS = 4
RET_D = 128
CHUNK = 128
ROPE_BASE = 10000.0
D_FF = 2816
N_EXPERTS = 8
EXPERT_FF = 1408
DEPTH = 2
ALPHA = (2 * DEPTH) ** 0.25
EPS = 1e-6

T_PROMPT = BATCH * SEQ
T_SAMPLE = DEC_BATCH * DEC_SEQ
T_ALL = T_PROMPT + T_SAMPLE
GROUP_ROWS = 1024
N_PROMPT_GROUPS = T_PROMPT // GROUP_ROWS
MOD_ROWS = 16
LANES = 128
VMEM_LIMIT = 58 * 1024 * 1024

AB_Q = 3 * HY_W
AB_K = AB_Q + N_HEADS * HEAD_DIM
AB_V = AB_K + N_KV * HEAD_DIM
IN_AB = AB_V + N_KV * HEAD_DIM
CD_Q = 0
CD_K = N_HEADS * HEAD_DIM
CD_V = CD_K + N_KV * HEAD_DIM
CD_RQ = CD_V + N_KV * HEAD_DIM
CD_RK = CD_RQ + RET_HEADS * RET_D
CD_RV = CD_RK + RET_HEADS * RET_D
CD_GF = CD_RV + RET_HEADS * RET_D
CD_GB = CD_GF + RET_HEADS * RET_D
IN_CD = CD_GB + RET_HEADS * RET_D


def _cparams(*sem):
    return pltpu.CompilerParams(dimension_semantics=sem, vmem_limit_bytes=VMEM_LIMIT)


def _silu(x):
    return x * jax.nn.sigmoid(x)


def _bdot(a, b):
    return jnp.dot(a.astype(BF16), b.astype(BF16), preferred_element_type=F32)


def _bdot_nt(a, b):
    return lax.dot_general(a.astype(BF16), b.astype(BF16), (((1,), (1,)), ((), ())),
                           preferred_element_type=F32)


def _bdot_tn(a, b):
    return lax.dot_general(a.astype(BF16), b.astype(BF16), (((0,), (0,)), ((), ())),
                           preferred_element_type=F32)


def _layer_norm(z, g, b):
    mu = jnp.mean(z, -1, keepdims=True)
    zc = z - mu
    var = jnp.mean(zc * zc, -1, keepdims=True)
    return zc * lax.rsqrt(var + EPS) * g + b


def _fill_rows(kern, in_specs, args, into):
    n_in = len(args)
    in_specs.append(pl.BlockSpec(memory_space=pl.ANY))
    args.append(into)

    def kern_into(*refs):
        return kern(*refs[:n_in], *refs[n_in + 1:])

    return kern_into, {n_in: 0}


def _group_of_tile(i, tm):
    return jnp.maximum(i // (GROUP_ROWS // tm) - (N_PROMPT_GROUPS - 1), 0)


def _rope_tables(n_tokens, d, reps):
    nf = d // 4
    inv = ROPE_BASE ** (-np.arange(nf, dtype=np.float64) / nf)
    pos = np.arange(n_tokens)
    row, col = pos // GRID_W, pos % GRID_W
    ang_r = row[:, None] * inv[None, :]
    ang_c = col[:, None] * inv[None, :]
    zeros = np.zeros_like(ang_r)
    cos = np.concatenate([np.cos(ang_r), np.cos(ang_r), np.cos(ang_c), np.cos(ang_c)], -1)
    sin_a = np.concatenate([-np.sin(ang_r), zeros, -np.sin(ang_c), zeros], -1)
    sin_b = np.concatenate([zeros, np.sin(ang_r), zeros, np.sin(ang_c)], -1)
    tile = lambda a: jnp.asarray(np.tile(a, (1, reps)), F32)
    return tile(cos), tile(sin_a), tile(sin_b)


def _dft_mats(l):
    n = 2 * l
    k = np.arange(l, dtype=np.float64)
    ang = 2.0 * np.pi * np.outer(k, k) / n
    fc = np.cos(ang)
    fs = np.sin(ang)
    fs[0, :] = np.cos(np.pi * k)
    fwd = np.concatenate([fc, fs], 0)
    wk = np.full((l,), 2.0)
    wk[0] = 1.0
    inv = np.concatenate([fc.T * wk[None, :], fs.T * wk[None, :]], 1)
    return jnp.asarray(fwd, F32), jnp.asarray(inv, F32)


def _filter_consts(l):
    t = np.linspace(0.0, 1.0, l, dtype=np.float32).astype(np.float64)[:, None]
    w = (2.0 * math.pi * np.arange(l, dtype=np.float64) / l)[:, None]
    bands = np.linspace(1e-4, POS_BANDS - 1.0, POS_BANDS, dtype=np.float32).astype(np.float64)[None, :]
    z = np.concatenate([t, np.cos(bands * w), -np.sin(bands * w)], -1)
    z = np.pad(z, ((0, 0), (0, LANES - POS_EMB)))
    max_decay = math.log(HY_TARGET) / HY_FAST_DECAY
    min_decay = math.log(HY_TARGET) / HY_SLOW_DECAY
    deltas = np.linspace(min_decay, max_decay, HY_W, dtype=np.float32).astype(np.float64)
    window = np.exp(-t * np.abs(deltas)[None, :])
    return jnp.asarray(z, F32), jnp.asarray(window, F32)


def _mod_kernel(c_ref, w_ref, b_ref, o_ref):
    o_ref[...] = _bdot(_silu(c_ref[...]), w_ref[...]) + b_ref[...]


def _modulation(cvec, ada_w, ada_b):
    tn = 1536
    n = ada_w.shape[-1]
    return pl.pallas_call(
        _mod_kernel,
        grid=(DEPTH, n // tn),
        in_specs=[pl.BlockSpec((MOD_ROWS, D_MODEL), lambda l, j: (0, 0)),
                  pl.BlockSpec((None, D_MODEL, tn), lambda l, j: (l, 0, j)),
                  pl.BlockSpec((None, 1, tn), lambda l, j: (l, 0, j))],
        out_specs=pl.BlockSpec((None, MOD_ROWS, tn), lambda l, j: (l, 0, j)),
        out_shape=jax.ShapeDtypeStruct((DEPTH, MOD_ROWS, n), F32),
        compiler_params=_cparams("parallel", "parallel"),
        name="mod",
    )(cvec, ada_w, ada_b.reshape(DEPTH, 1, n))


def _slab_tiles(xs, tm):
    ends = np.cumsum([x.shape[0] // tm for x in xs])
    return [(int(e - x.shape[0] // tm), int(e)) for x, e in zip(xs, ends)]


def _inproj_kernel(*refs, tm, tn, slabs, taps):
    x_refs = refs[:len(slabs)]
    sh_ref, sc_ref, w_ref, o_ref = refs[len(slabs):len(slabs) + 4]
    tap_refs = refs[len(slabs) + 4:len(slabs) + 4 + len(taps)]
    h_scr, w_scr = refs[len(slabs) + 4 + len(taps):]
    j, i = pl.program_id(0), pl.program_id(1)
    rows = pl.ds(pl.multiple_of(i * tm, tm), tm)
    g = _group_of_tile(i, tm)

    for x_ref, (first, end) in zip(x_refs, slabs):
        @pl.when((j == 0) & (i >= first) & (i < end))
        def _():
            sc = sc_ref[pl.ds(g, 1), :]
            sh = sh_ref[pl.ds(g, 1), :]
            h_scr[rows, :] = (x_ref[...] * (1.0 + sc) + sh).astype(BF16)

    @pl.when(i == 0)
    def _():
        w_scr[...] = w_ref[...].astype(BF16)

    y = jnp.dot(h_scr[rows, :], w_scr[...], preferred_element_type=F32)
    o_ref[...] = y

    for tap_ref, col in zip(tap_refs, taps):
        @pl.when((j == col // tn) & (i < T_PROMPT // tm))
        def _():
            tap_ref[...] = y[:, col % tn:col % tn + tap_ref.shape[1]]


def _inproj(xs, mod, w, tn, taps):
    tm = 1024
    n = w.shape[1]
    slabs = _slab_tiles(xs, tm)
    tap_w = N_KV * HEAD_DIM
    last_p = T_PROMPT // tm - 1

    def tap_spec(col):
        jt = col // tn
        return pl.BlockSpec((tm, tap_w), lambda j, i: (
            jnp.where(j < jt, 0, jnp.where(j == jt, jnp.minimum(i, last_p), last_p)), 0))

    def slab_spec(first, end):
        last = end - first - 1
        return pl.BlockSpec((tm, D_MODEL), lambda j, i: (jnp.where(j == 0, jnp.clip(i - first, 0, last), last), 0))

    return pl.pallas_call(
        functools.partial(_inproj_kernel, tm=tm, tn=tn, slabs=slabs, taps=taps),
        grid=(n // tn, T_ALL // tm),
        in_specs=[slab_spec(*s) for s in slabs]
        + [pl.BlockSpec((MOD_ROWS, D_MODEL), lambda j, i: (0, 0)),
           pl.BlockSpec((MOD_ROWS, D_MODEL), lambda j, i: (0, 1)),
           pl.BlockSpec((D_MODEL, tn), lambda j, i: (0, j))],
        out_specs=[pl.BlockSpec((tm, tn), lambda j, i: (i, j))] + [tap_spec(c) for c in taps],
        out_shape=[jax.ShapeDtypeStruct((T_ALL, n), F32)]
        + [jax.ShapeDtypeStruct((T_PROMPT, tap_w), F32) for _ in taps],
        scratch_shapes=[pltpu.VMEM((T_ALL, D_MODEL), BF16), pltpu.VMEM((D_MODEL, tn), BF16)],
        compiler_params=_cparams("arbitrary", "arbitrary"),
        name="inproj",
    )(*xs, mod, mod, w)


def _hyena_filter_kernel(z_ref, w1_ref, b1_ref, fr_ref, w2_ref, b2_ref, w3_ref, win_ref, fwd32_ref, inv32_ref,
                         hc_ref, hs_ref, hc2_ref, fwd_ref, inv_ref, *, l):
    hi = lax.Precision.HIGHEST
    fwd = fwd32_ref[...].astype(BF16)
    fwd_ref[...] = fwd
    inv_ref[...] = inv32_ref[...].astype(BF16)
    fr = fr_ref[...]
    h = jnp.sin(fr * (jnp.dot(z_ref[...], w1_ref[...], precision=hi, preferred_element_type=F32) + b1_ref[...]))
    h = jnp.sin(fr * (jnp.dot(h, w2_ref[...], precision=hi, preferred_element_type=F32) + b2_ref[...]))
    h = jnp.dot(h, w3_ref[...], precision=hi, preferred_element_type=F32)
    win = win_ref[...]
    hf = h[:, :HY_W] * win
    hb = h[:, HY_W:] * win
    pq = _bdot(fwd, jnp.concatenate([hf + hb, hf - hb], 1))
    p = pq[:, :HY_W]
    q = pq[:, HY_W:]
    row0 = lax.broadcasted_iota(jnp.int32, (l, 1), 0) == 0
    hc = p[:l]
    hc_ref[...] = hc
    hs_ref[...] = jnp.where(row0, 0.0, q[l:])
    hc2_ref[...] = jnp.where(row0, p[l:l + 1], hc)


def _hyena_filter(l, fw1, fb1, ffreq, fw2, fb2, fw3):
    z, window = _filter_consts(l)
    fwd, inv = _dft_mats(l)
    pad_c = LANES - FILT_HID
    w1 = jnp.pad(fw1, ((0, LANES - POS_EMB), (0, pad_c)))
    w2 = jnp.pad(fw2, ((0, pad_c), (0, pad_c)))
    w3 = jnp.pad(fw3, ((0, pad_c), (0, 0)))
    row = lambda a: jnp.pad(a, (0, pad_c)).reshape(1, LANES)
    shp = jax.ShapeDtypeStruct((l, HY_W), F32)
    return pl.pallas_call(
        functools.partial(_hyena_filter_kernel, l=l),
        out_shape=(shp, shp, shp, jax.ShapeDtypeStruct(fwd.shape, BF16), jax.ShapeDtypeStruct(inv.shape, BF16)),
        compiler_params=pltpu.CompilerParams(vmem_limit_bytes=VMEM_LIMIT),
        name=f"hyena_filter_{l}",
    )(z, w1, row(fb1), row(ffreq), w2, row(fb2), w3, window, fwd, inv)


def _hyena_kernel(u_ref, cw_ref, cb_ref, skip_ref, fwd_ref, inv_ref, hc_ref, hs_ref, hc2_ref, o_ref, *, l, nb):
    rows = lax.broadcasted_iota(jnp.int32, (l, 1), 0)
    hs = hs_ref[...]
    for s in range(nb):
        seq = slice(s * l, (s + 1) * l)
        u = u_ref[seq, :]
        prev = jnp.where(rows == 0, 0.0, pltpu.roll(u, 1, 0))
        nxt = jnp.where(rows == l - 1, 0.0, pltpu.roll(u, l - 1, 0))
        uc = prev * cw_ref[0:1, :] + u * cw_ref[1:2, :] + nxt * cw_ref[2:3, :] + cb_ref[...]
        x0 = uc[:, :HY_W]
        x1 = uc[:, HY_W:2 * HY_W]
        v = uc[:, 2 * HY_W:] * x1
        ab = _bdot(fwd_ref[...], v)
        a, b = ab[:l], ab[l:]
        re = a * hc_ref[...] - b * hs
        im = a * hs + b * hc2_ref[...]
        y = _bdot(inv_ref[...], jnp.concatenate([re, im], 0)) * (1.0 / (2 * l))
        o_ref[seq, :] = ((y + skip_ref[...] * v) * x0).astype(o_ref.dtype)


def _hyena(u_all, row_block0, n_seq, l, conv_w, conv_b, skip, filt, into, nb=1):
    hc, hs, hc2, fwd, inv = filt
    const = lambda shape: pl.BlockSpec(shape, lambda s: (0, 0))
    in_specs = [pl.BlockSpec((nb * l, 3 * HY_W), lambda s: (row_block0 + s, 0)),
                const((3, 3 * HY_W)), const((1, 3 * HY_W)), const((1, HY_W)),
                const((2 * l, l)), const((l, 2 * l)),
                const((l, HY_W)), const((l, HY_W)), const((l, HY_W))]
    args = [u_all, conv_w, conv_b.reshape(1, -1), skip.reshape(1, -1), fwd, inv, hc, hs, hc2]
    kern, alias = _fill_rows(functools.partial(_hyena_kernel, l=l, nb=nb), in_specs, args, into)
    return pl.pallas_call(
        kern,
        grid=(n_seq // nb,),
        in_specs=in_specs,
        out_specs=pl.BlockSpec((nb * l, HY_W), lambda s: (row_block0 + s, 0)),
        out_shape=jax.ShapeDtypeStruct((T_ALL, HY_W), BF16),
        input_output_aliases=alias,
        compiler_params=_cparams("parallel"),
        name=f"hyena_{l}",
    )(*args)


def _seg_rms_norm(x, bd_ref, g):
    sq = x * x
    hi = sq.astype(BF16)
    lo = (sq - hi.astype(F32)).astype(BF16)
    bd = bd_ref[...]
    ss = (jnp.dot(hi, bd, preferred_element_type=F32) + jnp.dot(lo, bd, preferred_element_type=F32))
    return x * lax.rsqrt(ss * (1.0 / HEAD_DIM) + EPS) * g


def _rope(x, cos, sin_a, sin_b, quarter):
    w = x.shape[-1]
    return x * cos + pltpu.roll(x, w - quarter, 1) * sin_a + pltpu.roll(x, quarter, 1) * sin_b


def _attn_kernel(*refs, l, qb, lc, rope, qknorm, band, has_sink, emit_k):
    it = iter(refs)
    q_ref, k_ref, v_ref = next(it), next(it), next(it)
    if lc:
        ck_ref, cv_ref = next(it), next(it)
    if rope:
        cq_ref, saq_ref, sbq_ref = next(it), next(it), next(it)
        ckk_ref, sak_ref, sbk_ref = next(it), next(it), next(it)
    if qknorm:
        qg_ref, kg_ref, bdq_ref, bdk_ref = next(it), next(it), next(it), next(it)
    if has_sink:
        sink_ref = next(it)
    o_ref = next(it)
    if emit_k:
        kout_ref = next(it)
    kp_scr = next(it)

    qi = pl.program_id(1)

    @pl.when(qi == 0)
    def _():
        k = k_ref[...]
        if qknorm:
            k = _seg_rms_norm(k, bdk_ref, kg_ref[...])
        if emit_k:
            kout_ref[...] = k
        if rope:
            k = _rope(k, ckk_ref[...], sak_ref[...], sbk_ref[...], HEAD_DIM // 4)
        for kv in range(N_KV):
            kp_scr[kv] = k[:, kv * HEAD_DIM:(kv + 1) * HEAD_DIM].astype(BF16)

    q = q_ref[...]
    if qknorm:
        q = _seg_rms_norm(q, bdq_ref, qg_ref[...])
    if rope:
        q = _rope(q, cq_ref[...], saq_ref[...], sbq_ref[...], HEAD_DIM // 4)
    q = q * HEAD_DIM ** -0.5

    rows = GROUPS * qb
    if band:
        n_loc = 3 * BLOCK
        start = pl.multiple_of(jnp.clip((qi - 1) * BLOCK, 0, l - n_loc), BLOCK)
        keys = pl.ds(start, n_loc)
        tq = qi * qb + lax.broadcasted_iota(jnp.int32, (rows, 1), 0) % qb
        tk = start + lax.broadcasted_iota(jnp.int32, (1, n_loc), 1)
        valid = jnp.abs(tq - tk) <= BLOCK
    else:
        keys = slice(None)

    outs = []
    for kv in range(N_KV):
        lanes = slice(kv * HEAD_DIM, (kv + 1) * HEAD_DIM)
        qs = jnp.concatenate([q[:, (kv * GROUPS + g) * HEAD_DIM:(kv * GROUPS + g + 1) * HEAD_DIM]
                              for g in range(GROUPS)], 0)
        s = _bdot_nt(qs, kp_scr[kv, keys, :])
        if band:
            s = jnp.where(valid, s, -jnp.inf)
        if lc:
            s_c = _bdot_nt(qs, ck_ref[:, lanes])
        e_parts, ec_parts, dens = [], [], []
        for g in range(GROUPS):
            r = slice(g * qb, (g + 1) * qb)
            m = jnp.max(s[r], -1, keepdims=True)
            if lc:
                m = jnp.maximum(m, jnp.max(s_c[r], -1, keepdims=True))
            if has_sink:
                sink = sink_ref[kv * GROUPS + g]
                m = jnp.maximum(m, sink)
            e = jnp.exp(s[r] - m)
            den = jnp.sum(e, -1, keepdims=True)
            e_parts.append(e)
            if lc:
                e_c = jnp.exp(s_c[r] - m)
                den = den + jnp.sum(e_c, -1, keepdims=True)
                ec_parts.append(e_c)
            if has_sink:
                den = den + jnp.exp(sink - m)
            dens.append(den)
        o = _bdot(jnp.concatenate(e_parts, 0), v_ref[keys, lanes])
        if lc:
            o = o + _bdot(jnp.concatenate(ec_parts, 0), cv_ref[:, lanes])
        outs.extend(o[g * qb:(g + 1) * qb] / dens[g] for g in range(GROUPS))
    o_ref[...] = jnp.concatenate(outs, 1).astype(o_ref.dtype)


def _attention(u, *, row0, n_seq, l, q_col, k_col, v_col, into, qb=BLOCK, ctx=None, rope=False, qk_gain=None,
               band=False, sink=None, emit_k=False):
    nq = l // qb
    qw = N_HEADS * HEAD_DIM
    kw = N_KV * HEAD_DIM
    qb0, sb0 = row0 // qb, row0 // l
    seq_spec = lambda col: pl.BlockSpec((l, kw), lambda b, i: (sb0 + b, col // kw))
    const = lambda shape: pl.BlockSpec(shape, lambda b, i: (0, 0))
    in_specs = [pl.BlockSpec((qb, qw), lambda b, i: (qb0 + b * nq + i, q_col // qw)),
                seq_spec(k_col), seq_spec(v_col)]
    args = [u, u, u]
    lc = 0
    if ctx is not None:
        lc = ctx[0].shape[1]
        in_specs += [pl.BlockSpec((None, lc, kw), lambda b, i: (b, 0, 0))] * 2
        args += list(ctx)
    if rope:
        tabs = _rope_tables(l, HEAD_DIM, N_HEADS)
        in_specs += [pl.BlockSpec((qb, qw), lambda b, i: (i, 0))] * 3 + [const((l, kw))] * 3
        args += list(tabs) + list(tabs)
    if qk_gain is not None:
        bd = np.kron(np.eye(N_HEADS), np.ones((HEAD_DIM, HEAD_DIM)))
        in_specs += [const((1, qw)), const((1, kw)), const((qw, qw)), const((kw, kw))]
        args += [jnp.tile(qk_gain[0], N_HEADS).reshape(1, qw), jnp.tile(qk_gain[1], N_KV).reshape(1, kw),
                 jnp.asarray(bd, BF16), jnp.asarray(bd[:kw, :kw], BF16)]
    if sink is not None:
        in_specs.append(pl.BlockSpec(memory_space=pltpu.SMEM))
        args.append(sink)
    out_specs = [pl.BlockSpec((qb, qw), lambda b, i: (qb0 + b * nq + i, 0))]
    out_shape = [jax.ShapeDtypeStruct((T_ALL, qw), BF16)]
    if emit_k:
        out_specs.append(pl.BlockSpec((l, kw), lambda b, i: (b, 0)))
        out_shape.append(jax.ShapeDtypeStruct((n_seq * l, kw), F32))
    kern = functools.partial(_attn_kernel, l=l, qb=qb, lc=lc, rope=rope, qknorm=qk_gain is not None, band=band,
                             has_sink=sink is not None, emit_k=emit_k)
    kern, alias = _fill_rows(kern, in_specs, args, into)
    return pl.pallas_call(
        kern,
        grid=(n_seq, nq),
        in_specs=in_specs,
        out_specs=out_specs,
        out_shape=out_shape,
        input_output_aliases=alias,
        scratch_shapes=[pltpu.VMEM((N_KV, l, HEAD_DIM), BF16)],
        compiler_params=_cparams("parallel", "arbitrary"),
        name=f"attn_{l}_{'b' if sink is not None else 'c'}",
    )(*args)


def _ret_kernel(*refs, l, nb, rope, has_s0, emit_state):
    it = iter(refs)
    dec_ref = next(it)
    rq_ref, rk_ref, rv_ref, gf_ref, gb_ref = next(it), next(it), next(it), next(it), next(it)
    if rope:
        cos_ref, sa_ref, sb_ref = next(it), next(it), next(it)
    if has_s0:
        s0_ref = next(it)
    o_ref = next(it)
    if emit_state:
        st_ref = next(it)

    h = pl.program_id(1)
    qs, ks, vs = [], [], []
    for s in range(nb):
        seq = slice(s * l, (s + 1) * l)
        q = rq_ref[seq, :] * RET_D ** -0.5
        k = rk_ref[seq, :]
        if rope:
            q = _rope(q, cos_ref[...], sa_ref[...], sb_ref[...], RET_D // 4)
            k = _rope(k, cos_ref[...], sa_ref[...], sb_ref[...], RET_D // 4)
        qs.append(q)
        ks.append(k)
        vs.append(rv_ref[seq, :])
    n = l // CHUNK
    ii = lax.broadcasted_iota(jnp.int32, (CHUNK, 1), 0).astype(F32)
    jj = lax.broadcasted_iota(jnp.int32, (1, CHUNK), 1).astype(F32)
    diff = ii - jj
    ys = [None] * nb
    for d in range(2):
        log_g = jnp.log(jax.nn.sigmoid(jnp.full((1, 1), dec_ref[d, h], F32)))
        if d == 0:
            mask = jnp.exp(jnp.where(diff >= 0, diff * log_g, -jnp.inf))
            q_dec = jnp.exp((ii + 1.0) * log_g)
            k_dec = jnp.exp((CHUNK - 1.0 - ii) * log_g)
            order = range(n)
        else:
            mask = jnp.exp(jnp.where(diff <= 0, -diff * log_g, -jnp.inf))
            q_dec = jnp.exp((CHUNK - ii) * log_g)
            k_dec = jnp.exp(ii * log_g)
            order = reversed(range(n))
        c_dec = jnp.exp(CHUNK * log_g)
        order = list(order)
        for s in range(nb):
            seq = slice(s * l, (s + 1) * l)
            q, k, v = qs[s], ks[s], vs[s]
            state = s0_ref[s, d] if has_s0 else jnp.zeros((RET_D, RET_D), F32)
            o_chunks = [None] * n
            for c in order:
                sl = slice(c * CHUNK, (c + 1) * CHUNK)
                qc, kc, vc = q[sl], k[sl], v[sl]
                inner = _bdot_nt(qc, kc) * mask
                o_chunks[c] = _bdot(inner, vc) + _bdot(qc * q_dec, state)
                state = state * c_dec + _bdot_tn(kc * k_dec, vc)
            if emit_state:
                st_ref[s, d] = state
            o = jnp.concatenate(o_chunks, 0)
            o = o * lax.rsqrt(jnp.mean(o * o, -1, keepdims=True) + EPS)
            gate = _silu((gf_ref if d == 0 else gb_ref)[seq, :])
            ys[s] = gate * o if ys[s] is None else ys[s] + gate * o
    for s in range(nb):
        o_ref[s * l:(s + 1) * l, :] = ys[s].astype(o_ref.dtype)


def _retention(u, ret_decay, *, row0, n_seq, l, into, nb=1, rope=False, s0=None, emit_state=False):
    sb0 = row0 // (nb * l)
    col = lambda c0: pl.BlockSpec((nb * l, RET_D), lambda b, h: (sb0 + b, c0 // RET_D + h))
    in_specs = [pl.BlockSpec(memory_space=pltpu.SMEM),
                col(CD_RQ), col(CD_RK), col(CD_RV), col(CD_GF), col(CD_GB)]
    args = [ret_decay, u, u, u, u, u]
    if rope:
        in_specs += [pl.BlockSpec((l, RET_D), lambda b, h: (0, 0))] * 3
        args += list(_rope_tables(l, RET_D, 1))
    state_spec = pl.BlockSpec((nb, 2, None, RET_D, RET_D), lambda b, h: (b, 0, h, 0, 0))
    if s0 is not None:
        in_specs.append(state_spec)
        args.append(s0)
    out_specs = [pl.BlockSpec((nb * l, RET_D), lambda b, h: (sb0 + b, h))]
    out_shape = [jax.ShapeDtypeStruct((T_ALL, RET_HEADS * RET_D), BF16)]
    if emit_state:
        out_specs.append(state_spec)
        out_shape.append(jax.ShapeDtypeStruct((n_seq, 2, RET_HEADS, RET_D, RET_D), F32))
    kern = functools.partial(_ret_kernel, l=l, nb=nb, rope=rope, has_s0=s0 is not None, emit_state=emit_state)
    kern, alias = _fill_rows(kern, in_specs, args, into)
    return pl.pallas_call(
        kern,
        grid=(n_seq // nb, RET_HEADS),
        in_specs=in_specs,
        out_specs=out_specs,
        out_shape=out_shape,
        input_output_aliases=alias,
        compiler_params=_cparams("parallel", "parallel"),
        name=f"retention_{l}",
    )(*args)


def _outproj_kernel(*refs, tm, slabs):
    ya_ref, yb_ref, w_ref = refs[:3]
    x_refs = refs[3:3 + len(slabs)]
    gate_ref, lng_ref, lnb_ref, o_ref, w_scr = refs[3 + len(slabs):]
    i = pl.program_id(0)

    @pl.when(i == 0)
    def _():
        w_scr[...] = w_ref[...].astype(BF16)

    half = ya_ref.shape[1]
    m = (jnp.dot(ya_ref[...], w_scr[:half], preferred_element_type=F32)
         + jnp.dot(yb_ref[...], w_scr[half:], preferred_element_type=F32))
    gm = gate_ref[pl.ds(_group_of_tile(i, tm), 1), :] * m

    for x_ref, (first, end) in zip(x_refs, slabs):
        @pl.when((i >= first) & (i < end))
        def _():
            o_ref[...] = _layer_norm(ALPHA * x_ref[...] + gm, lng_ref[...], lnb_ref[...])


def _outproj(ya, yb, w, xs, mod, ln_g, ln_b):
    tm = 1024
    half = ya.shape[1]
    slabs = _slab_tiles(xs, tm)
    const = lambda shape: pl.BlockSpec(shape, lambda i: (0, 0))

    def slab_spec(first, end):
        return pl.BlockSpec((tm, D_MODEL), lambda i: (jnp.clip(i - first, 0, end - first - 1), 0))

    return pl.pallas_call(
        functools.partial(_outproj_kernel, tm=tm, slabs=slabs),
        grid=(T_ALL // tm,),
        in_specs=[pl.BlockSpec((tm, half), lambda i: (i, 0)),
                  pl.BlockSpec((tm, half), lambda i: (i, 0)),
                  const((2 * half, D_MODEL))]
        + [slab_spec(*s) for s in slabs]
        + [pl.BlockSpec((MOD_ROWS, D_MODEL), lambda i: (0, 2)),
           const((1, D_MODEL)), const((1, D_MODEL))],
        out_specs=pl.BlockSpec((tm, D_MODEL), lambda i: (i, 0)),
        out_shape=jax.ShapeDtypeStruct((T_ALL, D_MODEL), F32),
        scratch_shapes=[pltpu.VMEM((2 * half, D_MODEL), BF16)],
        compiler_params=_cparams("arbitrary"),
        name="outproj_ln",
    )(ya, yb, w, *xs, mod, ln_g.reshape(1, -1), ln_b.reshape(1, -1))


TOK_TILE = 256
N_TOK_TILES = T_ALL // TOK_TILE
SORT_TILE = 256
EXP_TILE = 512
N_EXP_TILES = (2 * T_ALL) // EXP_TILE + N_EXPERTS
N_SORT_TILES = N_EXP_TILES * (EXP_TILE // SORT_TILE)
N_PROMPT_TOK_TILES = T_PROMPT // TOK_TILE
CUM_ROWS = 32


def _route_kernel(x_ref, sh_ref, sc_ref, r_ref, h_ref, rank_ref, rank_t_ref, gate_t_ref, cum_ref,
                  carry_row, carry_col):
    c = pl.program_id(0)

    @pl.when(c == 0)
    def _():
        carry_row[...] = jnp.zeros_like(carry_row)
        carry_col[...] = jnp.zeros_like(carry_col)
        cum_ref[...] = jnp.zeros_like(cum_ref)

    g = _group_of_tile(c, TOK_TILE)
    h = x_ref[...] * (1.0 + sc_ref[pl.ds(g, 1), :]) + sh_ref[pl.ds(g, 1), :]
    h_ref[...] = h.astype(BF16)
    logits = jnp.dot(h, r_ref[...], precision=lax.Precision.HIGHEST, preferred_element_type=F32)
    lane = lax.broadcasted_iota(jnp.int32, logits.shape, 1)
    logits = jnp.where(lane < N_EXPERTS, logits, -jnp.inf)
    m1 = jnp.max(logits, -1, keepdims=True)
    i1 = jnp.min(jnp.where(logits == m1, lane, LANES), -1, keepdims=True)
    rest = jnp.where(lane == i1, -jnp.inf, logits)
    m2 = jnp.max(rest, -1, keepdims=True)
    i2 = jnp.min(jnp.where(rest == m2, lane, LANES), -1, keepdims=True)
    e2 = jnp.exp(m2 - m1)
    den = 1.0 + e2
    gates = jnp.where(lane == i1, 1.0 / den, 0.0) + jnp.where(lane == i2, e2 / den, 0.0)
    sel = jnp.where((lane == i1) | (lane == i2), 1.0, 0.0)
    sel_t = sel.T
    ti = lax.broadcasted_iota(jnp.int32, (TOK_TILE, TOK_TILE), 0)
    tj = lax.broadcasted_iota(jnp.int32, (TOK_TILE, TOK_TILE), 1)
    before = jnp.where(tj < ti, 1.0, 0.0).astype(BF16)
    rank = jnp.dot(before, sel.astype(BF16), preferred_element_type=F32) + carry_row[...]
    rank_t = lax.dot_general(sel_t.astype(BF16), before, (((1,), (1,)), ((), ())),
                             preferred_element_type=F32) + carry_col[...]
    rank_ref[...] = jnp.where(sel > 0.0, rank, -1.0)
    rank_t_ref[...] = jnp.where(sel_t > 0.0, rank_t, -1.0)[:N_EXPERTS]
    gate_t_ref[...] = gates.T[:N_EXPERTS]
    cum_ref[pl.ds(c, 1), :] = carry_row[...].astype(jnp.int32)
    carry_row[...] += jnp.sum(sel, 0, keepdims=True)
    carry_col[...] += jnp.sum(sel_t, 1, keepdims=True)

    @pl.when(c == N_TOK_TILES - 1)
    def _():
        cum_ref[pl.ds(N_TOK_TILES, 1), :] = carry_row[...].astype(jnp.int32)


def _route(x, mod, router):
    tile = lambda w: pl.BlockSpec((TOK_TILE, w), lambda c: (c, 0))
    tile_t = pl.BlockSpec((N_EXPERTS, TOK_TILE), lambda c: (0, c))
    return pl.pallas_call(
        _route_kernel,
        grid=(N_TOK_TILES,),
        in_specs=[tile(D_MODEL),
                  pl.BlockSpec((MOD_ROWS, D_MODEL), lambda c: (0, 3)),
                  pl.BlockSpec((MOD_ROWS, D_MODEL), lambda c: (0, 4)),
                  pl.BlockSpec((D_MODEL, LANES), lambda c: (0, 0))],
        out_specs=[tile(D_MODEL), tile(LANES), tile_t, tile_t,
                   pl.BlockSpec((CUM_ROWS, LANES), lambda c: (0, 0))],
        out_shape=[jax.ShapeDtypeStruct((T_ALL, D_MODEL), BF16),
                   jax.ShapeDtypeStruct((T_ALL, LANES), F32),
                   jax.ShapeDtypeStruct((N_EXPERTS, T_ALL), F32),
                   jax.ShapeDtypeStruct((N_EXPERTS, T_ALL), F32),
                   jax.ShapeDtypeStruct((CUM_ROWS, LANES), jnp.int32)],
        scratch_shapes=[pltpu.VMEM((1, LANES), F32), pltpu.VMEM((LANES, 1), F32)],
        compiler_params=_cparams("arbitrary"),
        name="route",
    )(x, mod, mod, jnp.pad(router, ((0, 0), (0, LANES - N_EXPERTS))))


def _gather_kernel(te_ref, off_ref, cum_ref, nt_ref, h_ref, rank_t_ref, gate_t_ref, xs_ref, gs_ref, acc_scr, g_scr):
    i = pl.program_id(0)
    e = te_ref[i]
    r0 = i * SORT_TILE - off_ref[e]
    acc_scr[...] = jnp.zeros_like(acc_scr)
    g_scr[...] = jnp.zeros_like(g_scr)
    want = (r0 + lax.broadcasted_iota(jnp.int32, (SORT_TILE, 1), 0)).astype(F32)
    for c in range(N_TOK_TILES):
        lo = cum_ref[c * N_EXPERTS + e]
        hi = cum_ref[(c + 1) * N_EXPERTS + e]

        @pl.when((i < nt_ref[0]) & (lo < r0 + SORT_TILE) & (hi > r0))
        def _():
            cols = slice(c * TOK_TILE, (c + 1) * TOK_TILE)
            pick = rank_t_ref[pl.ds(e, 1), cols] == want
            acc_scr[...] += jnp.dot(jnp.where(pick, 1.0, 0.0).astype(BF16), h_ref[cols, :],
                                    preferred_element_type=F32)
            g_scr[...] += jnp.sum(jnp.where(pick, gate_t_ref[pl.ds(e, 1), cols], 0.0), -1, keepdims=True)

    xs_ref[...] = acc_scr[...].astype(BF16)
    gs_ref[...] = jnp.broadcast_to(g_scr[...], gs_ref.shape)


def _gather(tile_expert, off, cum, n_tiles, h, rank_t, gate_t):
    const = lambda shape: pl.BlockSpec(shape, lambda i, *_: (0, 0))
    return pl.pallas_call(
        _gather_kernel,
        grid_spec=pltpu.PrefetchScalarGridSpec(
            num_scalar_prefetch=4,
            grid=(N_SORT_TILES,),
            in_specs=[const((T_ALL, D_MODEL)), const((N_EXPERTS, T_ALL)), const((N_EXPERTS, T_ALL))],
            out_specs=[pl.BlockSpec((SORT_TILE, D_MODEL), lambda i, *_: (i, 0)),
                       pl.BlockSpec((SORT_TILE, LANES), lambda i, *_: (i, 0))],
            scratch_shapes=[pltpu.VMEM((SORT_TILE, D_MODEL), F32), pltpu.VMEM((SORT_TILE, 1), F32)]),
        out_shape=[jax.ShapeDtypeStruct((N_SORT_TILES * SORT_TILE, D_MODEL), BF16),
                   jax.ShapeDtypeStruct((N_SORT_TILES * SORT_TILE, LANES), F32)],
        compiler_params=_cparams("parallel"),
        name="moe_gather",
    )(tile_expert, off, cum, n_tiles, h, rank_t, gate_t)


def _expert_kernel(te_ref, nt_ref, xs_ref, gs_ref, w1_ref, w3_ref, w2_ref, ys_ref, w1_scr, w3_scr, w2_scr, acc_scr):
    i = pl.program_id(0)

    @pl.when((i == 0) | (te_ref[i] != te_ref[jnp.maximum(i - 1, 0)]))
    def _():
        w1_scr[...] = w1_ref[...].astype(BF16)
        w3_scr[...] = w3_ref[...].astype(BF16)
        w2_scr[...] = w2_ref[...].astype(BF16)

    @pl.when(i < nt_ref[0])
    def _():
        x = xs_ref[...]
        gate = gs_ref[:, 0:1]
        acc_scr[...] = jnp.zeros_like(acc_scr)
        for c0 in range(0, EXPERT_FF, FF_CHUNK):
            cs = slice(c0, min(c0 + FF_CHUNK, EXPERT_FF))
            a = jnp.dot(x, w1_scr[:, cs], preferred_element_type=F32)
            b = jnp.dot(x, w3_scr[:, cs], preferred_element_type=F32)
            act = (_silu(a) * b * gate).astype(BF16)
            acc_scr[...] += jnp.dot(act, w2_scr[cs, :], preferred_element_type=F32)
        ys_ref[...] = acc_scr[...].astype(BF16)

    @pl.when(i >= nt_ref[0])
    def _():
        ys_ref[...] = jnp.zeros_like(ys_ref)


def _experts(tile_expert, n_tiles, xs, gs, w1, w3, w2):
    w_in = pl.BlockSpec((None, D_MODEL, EXPERT_FF), lambda i, te, nt: (te[i], 0, 0))
    w_out = pl.BlockSpec((None, EXPERT_FF, D_MODEL), lambda i, te, nt: (te[i], 0, 0))
    return pl.pallas_call(
        _expert_kernel,
        grid_spec=pltpu.PrefetchScalarGridSpec(
            num_scalar_prefetch=2,
            grid=(N_EXP_TILES,),
            in_specs=[pl.BlockSpec((EXP_TILE, D_MODEL), lambda i, te, nt: (i, 0)),
                      pl.BlockSpec((EXP_TILE, LANES), lambda i, te, nt: (i, 0)),
                      w_in, w_in, w_out],
            out_specs=pl.BlockSpec((EXP_TILE, D_MODEL), lambda i, te, nt: (i, 0)),
            scratch_shapes=[pltpu.VMEM((D_MODEL, EXPERT_FF), BF16), pltpu.VMEM((D_MODEL, EXPERT_FF), BF16),
                            pltpu.VMEM((EXPERT_FF, D_MODEL), BF16), pltpu.VMEM((EXP_TILE, D_MODEL), F32)]),
        out_shape=jax.ShapeDtypeStruct((N_EXP_TILES * EXP_TILE, D_MODEL), BF16),
        compiler_params=_cparams("arbitrary"),
        name="moe_experts",
    )(tile_expert, n_tiles, xs, gs, w1, w3, w2)


def _combine_kernel(off_ref, cum_ref, ys_ref, rank_ref, x_ref, gate_ref, lng_ref, lnb_ref, op_ref, os_ref, acc_scr):
    c = pl.program_id(0)
    acc_scr[...] = jnp.zeros_like(acc_scr)
    rank = rank_ref[...]
    lane = lax.broadcasted_iota(jnp.int32, rank.shape, 1)
    cols = lax.broadcasted_iota(jnp.int32, (1, SORT_TILE), 1)
    for e in range(N_EXPERTS):
        lo = off_ref[e] + cum_ref[c * N_EXPERTS + e]
        hi = off_ref[e] + cum_ref[(c + 1) * N_EXPERTS + e]
        r = jnp.sum(jnp.where(lane == e, rank, 0.0), -1, keepdims=True)
        pos = jnp.where(r >= 0.0, r + jnp.full((1, 1), off_ref[e], jnp.int32).astype(F32), -1.0)
        first = lo // SORT_TILE
        for k in range(2):
            s = first + k

            @pl.when((hi > lo) & (s * SORT_TILE < hi))
            def _():
                pick = pos == (s * SORT_TILE + cols).astype(F32)
                rows = ys_ref[pl.ds(pl.multiple_of(s * SORT_TILE, SORT_TILE), SORT_TILE), :]
                acc_scr[...] += jnp.dot(jnp.where(pick, 1.0, 0.0).astype(BF16), rows, preferred_element_type=F32)

    g = _group_of_tile(c, TOK_TILE)
    z = ALPHA * x_ref[...] + gate_ref[pl.ds(g, 1), :] * acc_scr[...]
    y = _layer_norm(z, lng_ref[...], lnb_ref[...])

    @pl.when(c < N_PROMPT_TOK_TILES)
    def _():
        op_ref[...] = y

    @pl.when(c >= N_PROMPT_TOK_TILES)
    def _():
        os_ref[...] = y


def _combine(off, cum, ys, rank, x, mod, ln_g, ln_b):
    const = lambda shape: pl.BlockSpec(shape, lambda c, *_: (0, 0))
    last_p = N_PROMPT_TOK_TILES - 1
    return pl.pallas_call(
        _combine_kernel,
        grid_spec=pltpu.PrefetchScalarGridSpec(
            num_scalar_prefetch=2,
            grid=(N_TOK_TILES,),
            in_specs=[pl.BlockSpec(ys.shape, lambda c, *_: (0, 0), pipeline_mode=pl.Buffered(1)),
                      pl.BlockSpec((TOK_TILE, LANES), lambda c, *_: (c, 0)),
                      pl.BlockSpec((TOK_TILE, D_MODEL), lambda c, *_: (c, 0)),
                      pl.BlockSpec((MOD_ROWS, D_MODEL), lambda c, *_: (0, 5)),
                      const((1, D_MODEL)), const((1, D_MODEL))],
            out_specs=[pl.BlockSpec((TOK_TILE, D_MODEL), lambda c, *_: (jnp.minimum(c, last_p), 0)),
                       pl.BlockSpec((TOK_TILE, D_MODEL), lambda c, *_: (jnp.maximum(c - last_p - 1, 0), 0))],
            scratch_shapes=[pltpu.VMEM((TOK_TILE, D_MODEL), F32)]),
        out_shape=[jax.ShapeDtypeStruct((T_PROMPT, D_MODEL), F32), jax.ShapeDtypeStruct((T_SAMPLE, D_MODEL), F32)],
        compiler_params=_cparams("arbitrary"),
        name="moe_combine_ln",
    )(off, cum, ys, rank, x, mod, ln_g.reshape(1, -1), ln_b.reshape(1, -1))


def _moe(x, mod, ln_g, ln_b, router, w1, w3, w2):
    h, rank, rank_t, gate_t, cum = _route(x, mod, router)
    counts = cum[N_TOK_TILES, :N_EXPERTS]
    tiles = (counts + EXP_TILE - 1) // EXP_TILE
    ends = jnp.cumsum(tiles)
    off = ((ends - tiles) * EXP_TILE).astype(jnp.int32)
    n_tiles = ends[-1:].astype(jnp.int32)
    tile_ids = jnp.minimum(jnp.arange(N_EXP_TILES, dtype=jnp.int32), n_tiles - 1)
    tile_expert = jnp.sum((tile_ids[:, None] >= ends[None, :]).astype(jnp.int32), -1)
    sub = EXP_TILE // SORT_TILE
    cum_flat = cum[:N_TOK_TILES + 1, :N_EXPERTS].reshape(-1)
    xs, gs = _gather(jnp.repeat(tile_expert, sub), off, cum_flat, n_tiles * sub, h, rank_t, gate_t)
    ys = _experts(tile_expert, n_tiles, xs, gs, w1, w3, w2)
    return _combine(off, cum_flat, ys, rank, x, mod, ln_g, ln_b)


FF_CHUNK = 256


def _ffn_kernel(x_ref, sh_ref, sc_ref, gate_ref, lng_ref, lnb_ref, w1_ref, w3_ref, w2_ref, o_ref, h_scr, acc_scr,
                *, tm, ff):
    i, j = pl.program_id(0), pl.program_id(1)
    g = _group_of_tile(i, tm)

    @pl.when(j == 0)
    def _():
        h_scr[...] = (x_ref[...] * (1.0 + sc_ref[pl.ds(g, 1), :]) + sh_ref[pl.ds(g, 1), :]).astype(BF16)
        acc_scr[...] = jnp.zeros_like(acc_scr)

    h = h_scr[...]
    for c0 in range(0, ff, FF_CHUNK):
        cs = slice(c0, min(c0 + FF_CHUNK, ff))
        a = jnp.dot(h, w1_ref[:, cs].astype(BF16), preferred_element_type=F32)
        b = jnp.dot(h, w3_ref[:, cs].astype(BF16), preferred_element_type=F32)
        act = _silu(a) * b
        acc_scr[...] += jnp.dot(act.astype(BF16), w2_ref[cs, :].astype(BF16), preferred_element_type=F32)

    @pl.when(j == pl.num_programs(1) - 1)
    def _():
        z = ALPHA * x_ref[...] + gate_ref[pl.ds(g, 1), :] * acc_scr[...]
        o_ref[...] = _layer_norm(z, lng_ref[...], lnb_ref[...])


def _ffn(x, mod, ln_g, ln_b, w1, w3, w2):
    tm = 1024
    ff = FF_CHUNK
    n_j = D_FF // ff
    w_in_spec = pl.BlockSpec((D_MODEL, ff), lambda i, j: (0, j))
    w_out_spec = pl.BlockSpec((ff, D_MODEL), lambda i, j: (j, 0))
    mod_spec = lambda col: pl.BlockSpec((MOD_ROWS, D_MODEL), lambda i, j: (0, col))
    const = lambda shape: pl.BlockSpec(shape, lambda i, j: (0, 0))
    in_specs = [pl.BlockSpec((tm, D_MODEL), lambda i, j: (i, 0)),
                mod_spec(3), mod_spec(4), mod_spec(5),
                const((1, D_MODEL)), const((1, D_MODEL)),
                w_in_spec, w_in_spec, w_out_spec]
    args = [x, mod, mod, mod, ln_g.reshape(1, -1), ln_b.reshape(1, -1), w1, w3, w2]
    return pl.pallas_call(
        functools.partial(_ffn_kernel, tm=tm, ff=ff),
        grid=(T_ALL // tm, n_j),
        in_specs=in_specs,
        out_specs=pl.BlockSpec((tm, D_MODEL), lambda i, j: (i, 0)),
        out_shape=jax.ShapeDtypeStruct((T_ALL, D_MODEL), F32),
        scratch_shapes=[pltpu.VMEM((tm, D_MODEL), BF16), pltpu.VMEM((tm, D_MODEL), F32)],
        compiler_params=_cparams("parallel", "arbitrary"),
        name="ffn",
    )(*args)


def kernel(x_prompt, x_sample, c, cache_k_b, cache_v_b, cache_k_c, cache_v_c, state_ret, c_ctx, ada_w, ada_b, ln_g, ln_b, w_in_ab, hy_conv_w, hy_conv_b, hf_w1, hf_b1, hf_freq, hf_w2, hf_b2, hf_w3, hy_skip, sink_b, w_out_ab, ffn_w1, ffn_w3, ffn_w2, w_in_cd, qn_g, kn_g, ret_decay, w_out_cd, moe_router, moe_w1, moe_w3, moe_w2):
    x_in = (x_prompt.reshape(T_PROMPT, D_MODEL), x_sample.reshape(T_SAMPLE, D_MODEL))
    cvec = jnp.concatenate([c_ctx[None], c, jnp.zeros((MOD_ROWS - 1 - DEC_BATCH, D_MODEL), F32)], 0)
    mod = _modulation(cvec, ada_w, ada_b)
    kw = N_KV * HEAD_DIM
    prompt = dict(row0=0, n_seq=BATCH, l=SEQ)
    sample = dict(row0=T_PROMPT, n_seq=DEC_BATCH, l=DEC_SEQ)

    u, k_b, v_b = _inproj(x_in, mod[0], w_in_ab[0], tn=1152, taps=(AB_K, AB_V))
    filt_args = (hf_w1[0], hf_b1[0], hf_freq[0], hf_w2[0], hf_b2[0], hf_w3[0])
    hy_args = (hy_conv_w[0], hy_conv_b[0], hy_skip[0])
    blank = lambda w: jnp.zeros((T_ALL, w), BF16)
    ya = _hyena(u, 0, BATCH, SEQ, *hy_args, _hyena_filter(SEQ, *filt_args), into=blank(HY_W), nb=4)
    ya = _hyena(u, T_PROMPT // DEC_SEQ, DEC_BATCH, DEC_SEQ, *hy_args, _hyena_filter(DEC_SEQ, *filt_args), into=ya)
    cols_b = dict(q_col=AB_Q, k_col=AB_K, v_col=AB_V)
    ctx_b = (cache_k_b[:, 0].reshape(DEC_BATCH, PAST_LEN, kw), cache_v_b[:, 0].reshape(DEC_BATCH, PAST_LEN, kw))
    yb, = _attention(u, **prompt, **cols_b, qb=SEQ, sink=sink_b[0], into=blank(N_HEADS * HEAD_DIM))
    yb, = _attention(u, **sample, **cols_b, sink=sink_b[0], ctx=ctx_b, rope=True, band=True, into=yb)
    k_b = k_b.reshape(BATCH, 1, SEQ, N_KV, HEAD_DIM)
    v_b = v_b.reshape(BATCH, 1, SEQ, N_KV, HEAD_DIM)
    x = _outproj(ya, yb, w_out_ab[0], x_in, mod[0], ln_g[0, 0], ln_b[0, 0])
    x = _ffn(x, mod[0], ln_g[0, 1], ln_b[0, 1], ffn_w1[0], ffn_w3[0], ffn_w2[0])

    u, v_c = _inproj((x,), mod[1], w_in_cd[0], tn=1664, taps=(CD_V,))
    cols_c = dict(q_col=CD_Q, k_col=CD_K, v_col=CD_V)
    gains = (qn_g[0], kn_g[0])
    ctx_c = (cache_k_c[:, 0].reshape(DEC_BATCH, PAST_LEN, kw), cache_v_c[:, 0].reshape(DEC_BATCH, PAST_LEN, kw))
    yc, k_c = _attention(u, **prompt, **cols_c, qb=SEQ, qk_gain=gains, emit_k=True, into=blank(N_HEADS * HEAD_DIM))
    yc, = _attention(u, **sample, **cols_c, qb=2 * BLOCK, qk_gain=gains, ctx=ctx_c, rope=True, into=yc)
    yd, s_r = _retention(u, ret_decay[0], **prompt, nb=4, emit_state=True, into=blank(RET_HEADS * RET_D))
    yd, = _retention(u, ret_decay[0], **sample, nb=DEC_BATCH, rope=True, s0=state_ret[:, 0], into=yd)
    k_c = k_c.reshape(BATCH, 1, SEQ, N_KV, HEAD_DIM)
    v_c = v_c.reshape(BATCH, 1, SEQ, N_KV, HEAD_DIM)
    x = _outproj(yc, yd, w_out_cd[0], (x,), mod[1], ln_g[1, 0], ln_b[1, 0])
    y_prompt, y_sample = _moe(x, mod[1], ln_g[1, 1], ln_b[1, 1], moe_router[0], moe_w1[0], moe_w3[0], moe_w2[0])

    return (y_prompt.reshape(BATCH, SEQ, D_MODEL), y_sample.reshape(DEC_BATCH, DEC_SEQ, D_MODEL),
            k_b, v_b, k_c, v_c, s_r[:, None])
```

```python
import functools
import math

import numpy as np
import jax
import jax.numpy as jnp
from jax import lax
from jax.experimental import pallas as pl
from jax.experimental.pallas import tpu as pltpu

F32 = jnp.float32
BF16 = jnp.bfloat16

D_MODEL = 1024
BATCH = 16
SEQ = 256
DEC_BATCH = 2
DEC_SEQ = 1024
PAST_LEN = 512
GRID_W = 64
HEAD_DIM = 64
BLOCK = 128
HY_W = 512
POS_BANDS = 16
POS_EMB = 1 + 2 * POS_BANDS
FILT_HID = 64
HY_FAST_DECAY = 0.3
HY_SLOW_DECAY = 1.5
HY_TARGET = 1e-2
N_HEADS = 8
N_KV = 2
GROUPS = N_HEADS // N_KV
RET_HEADS = 4
RET_D = 128
CHUNK = 128
ROPE_BASE = 10000.0
D_FF = 2816
N_EXPERTS = 8
EXPERT_FF = 1408
DEPTH = 2
ALPHA = (2 * DEPTH) ** 0.25
EPS = 1e-6

T_PROMPT = BATCH * SEQ
T_SAMPLE = DEC_BATCH * DEC_SEQ
T_ALL = T_PROMPT + T_SAMPLE
GROUP_ROWS = 1024
N_PROMPT_GROUPS = T_PROMPT // GROUP_ROWS
MOD_ROWS = 16
LANES = 128
VMEM_LIMIT = 58 * 1024 * 1024

AB_Q = 3 * HY_W
AB_K = AB_Q + N_HEADS * HEAD_DIM
AB_V = AB_K + N_KV * HEAD_DIM
IN_AB = AB_V + N_KV * HEAD_DIM
CD_Q = 0
CD_K = N_HEADS * HEAD_DIM
CD_V = CD_K + N_KV * HEAD_DIM
CD_RQ = CD_V + N_KV * HEAD_DIM
CD_RK = CD_RQ + RET_HEADS * RET_D
CD_RV = CD_RK + RET_HEADS * RET_D
CD_GF = CD_RV + RET_HEADS * RET_D
CD_GB = CD_GF + RET_HEADS * RET_D
IN_CD = CD_GB + RET_HEADS * RET_D


def _cparams(*sem):
    return pltpu.CompilerParams(dimension_semantics=sem, vmem_limit_bytes=VMEM_LIMIT)


def _silu(x):
    return x * jax.nn.sigmoid(x)


def _bdot(a, b):
    return jnp.dot(a.astype(BF16), b.astype(BF16), preferred_element_type=F32)


def _bdot_nt(a, b):
    return lax.dot_general(a.astype(BF16), b.astype(BF16), (((1,), (1,)), ((), ())),
                           preferred_element_type=F32)


def _bdot_tn(a, b):
    return lax.dot_general(a.astype(BF16), b.astype(BF16), (((0,), (0,)), ((), ())),
                           preferred_element_type=F32)


def _layer_norm(z, g, b):
    mu = jnp.mean(z, -1, keepdims=True)
    zc = z - mu
    var = jnp.mean(zc * zc, -1, keepdims=True)
    return zc * lax.rsqrt(var + EPS) * g + b


def _fill_rows(kern, in_specs, args, into):
    n_in = len(args)
    in_specs.append(pl.BlockSpec(memory_space=pl.ANY))
    args.append(into)

    def kern_into(*refs):
        return kern(*refs[:n_in], *refs[n_in + 1:])

    return kern_into, {n_in: 0}


def _group_of_tile(i, tm):
    return jnp.maximum(i // (GROUP_ROWS // tm) - (N_PROMPT_GROUPS - 1), 0)


def _rope_tables(n_tokens, d, reps):
    nf = d // 4
    inv = ROPE_BASE ** (-np.arange(nf, dtype=np.float64) / nf)
    pos = np.arange(n_tokens)
    row, col = pos // GRID_W, pos % GRID_W
    ang_r = row[:, None] * inv[None, :]
    ang_c = col[:, None] * inv[None, :]
    zeros = np.zeros_like(ang_r)
    cos = np.concatenate([np.cos(ang_r), np.cos(ang_r), np.cos(ang_c), np.cos(ang_c)], -1)
    sin_a = np.concatenate([-np.sin(ang_r), zeros, -np.sin(ang_c), zeros], -1)
    sin_b = np.concatenate([zeros, np.sin(ang_r), zeros, np.sin(ang_c)], -1)
    tile = lambda a: jnp.asarray(np.tile(a, (1, reps)), F32)
    return tile(cos), tile(sin_a), tile(sin_b)


def _dft_mats(l):
    n = 2 * l
    k = np.arange(l, dtype=np.float64)
    ang = 2.0 * np.pi * np.outer(k, k) / n
    fc = np.cos(ang)
    fs = np.sin(ang)
    fs[0, :] = np.cos(np.pi * k)
    fwd = np.concatenate([fc, fs], 0)
    wk = np.full((l,), 2.0)
    wk[0] = 1.0
    inv = np.concatenate([fc.T * wk[None, :], fs.T * wk[None, :]], 1)
    return jnp.asarray(fwd, F32), jnp.asarray(inv, F32)


def _filter_consts(l):
    t = np.linspace(0.0, 1.0, l, dtype=np.float32).astype(np.float64)[:, None]
    w = (2.0 * math.pi * np.arange(l, dtype=np.float64) / l)[:, None]
    bands = np.linspace(1e-4, POS_BANDS - 1.0, POS_BANDS, dtype=np.float32).astype(np.float64)[None, :]
    z = np.concatenate([t, np.cos(bands * w), -np.sin(bands * w)], -1)
    z = np.pad(z, ((0, 0), (0, LANES - POS_EMB)))
    max_decay = math.log(HY_TARGET) / HY_FAST_DECAY
    min_decay = math.log(HY_TARGET) / HY_SLOW_DECAY
    deltas = np.linspace(min_decay, max_decay, HY_W, dtype=np.float32).astype(np.float64)
    window = np.exp(-t * np.abs(deltas)[None, :])
    return jnp.asarray(z, F32), jnp.asarray(window, F32)


def _mod_kernel(c_ref, w_ref, b_ref, o_ref):
    o_ref[...] = _bdot(_silu(c_ref[...]), w_ref[...]) + b_ref[...]


def _modulation(cvec, ada_w, ada_b):
    tn = 1536
    n = ada_w.shape[-1]
    return pl.pallas_call(
        _mod_kernel,
        grid=(DEPTH, n // tn),
        in_specs=[pl.BlockSpec((MOD_ROWS, D_MODEL), lambda l, j: (0, 0)),
                  pl.BlockSpec((None, D_MODEL, tn), lambda l, j: (l, 0, j)),
                  pl.BlockSpec((None, 1, tn), lambda l, j: (l, 0, j))],
        out_specs=pl.BlockSpec((None, MOD_ROWS, tn), lambda l, j: (l, 0, j)),
        out_shape=jax.ShapeDtypeStruct((DEPTH, MOD_ROWS, n), F32),
        compiler_params=_cparams("parallel", "parallel"),
        name="mod",
    )(cvec, ada_w, ada_b.reshape(DEPTH, 1, n))


def _slab_tiles(xs, tm):
    ends = np.cumsum([x.shape[0] // tm for x in xs])
    return [(int(e - x.shape[0] // tm), int(e)) for x, e in zip(xs, ends)]


def _inproj_kernel(*refs, tm, tn, slabs, taps):
    x_refs = refs[:len(slabs)]
    sh_ref, sc_ref, w_ref, o_ref = refs[len(slabs):len(slabs) + 4]
    tap_refs = refs[len(slabs) + 4:len(slabs) + 4 + len(taps)]
    h_scr, w_scr = refs[len(slabs) + 4 + len(taps):]
    j, i = pl.program_id(0), pl.program_id(1)
    rows = pl.ds(pl.multiple_of(i * tm, tm), tm)
    g = _group_of_tile(i, tm)

    for x_ref, (first, end) in zip(x_refs, slabs):
        @pl.when((j == 0) & (i >= first) & (i < end))
        def _():
            sc = sc_ref[pl.ds(g, 1), :]
            sh = sh_ref[pl.ds(g, 1), :]
            h_scr[rows, :] = (x_ref[...] * (1.0 + sc) + sh).astype(BF16)

    @pl.when(i == 0)
    def _():
        w_scr[...] = w_ref[...].astype(BF16)

    y = jnp.dot(h_scr[rows, :], w_scr[...], preferred_element_type=F32)
    o_ref[...] = y

    for tap_ref, col in zip(tap_refs, taps):
        @pl.when((j == col // tn) & (i < T_PROMPT // tm))
        def _():
            tap_ref[...] = y[:, col % tn:col % tn + tap_ref.shape[1]]


def _inproj(xs, mod, w, tn, taps):
    tm = 1024
    n = w.shape[1]
    slabs = _slab_tiles(xs, tm)
    tap_w = N_KV * HEAD_DIM
    last_p = T_PROMPT // tm - 1

    def tap_spec(col):
        jt = col // tn
        return pl.BlockSpec((tm, tap_w), lambda j, i: (
            jnp.where(j < jt, 0, jnp.where(j == jt, jnp.minimum(i, last_p), last_p)), 0))

    def slab_spec(first, end):
        last = end - first - 1
        return pl.BlockSpec((tm, D_MODEL), lambda j, i: (jnp.where(j == 0, jnp.clip(i - first, 0, last), last), 0))

    return pl.pallas_call(
        functools.partial(_inproj_kernel, tm=tm, tn=tn, slabs=slabs, taps=taps),
        grid=(n // tn, T_ALL // tm),
        in_specs=[slab_spec(*s) for s in slabs]
        + [pl.BlockSpec((MOD_ROWS, D_MODEL), lambda j, i: (0, 0)),
           pl.BlockSpec((MOD_ROWS, D_MODEL), lambda j, i: (0, 1)),
           pl.BlockSpec((D_MODEL, tn), lambda j, i: (0, j))],
        out_specs=[pl.BlockSpec((tm, tn), lambda j, i: (i, j))] + [tap_spec(c) for c in taps],
        out_shape=[jax.ShapeDtypeStruct((T_ALL, n), F32)]
        + [jax.ShapeDtypeStruct((T_PROMPT, tap_w), F32) for _ in taps],
        scratch_shapes=[pltpu.VMEM((T_ALL, D_MODEL), BF16), pltpu.VMEM((D_MODEL, tn), BF16)],
        compiler_params=_cparams("arbitrary", "arbitrary"),
        name="inproj",
    )(*xs, mod, mod, w)


def _hyena_filter_kernel(z_ref, w1_ref, b1_ref, fr_ref, w2_ref, b2_ref, w3_ref, win_ref, fwd32_ref, inv32_ref,
                         hc_ref, hs_ref, hc2_ref, fwd_ref, inv_ref, *, l):
    hi = lax.Precision.HIGHEST
    fwd = fwd32_ref[...].astype(BF16)
    fwd_ref[...] = fwd
    inv_ref[...] = inv32_ref[...].astype(BF16)
    fr = fr_ref[...]
    h = jnp.sin(fr * (jnp.dot(z_ref[...], w1_ref[...], precision=hi, preferred_element_type=F32) + b1_ref[...]))
    h = jnp.sin(fr * (jnp.dot(h, w2_ref[...], precision=hi, preferred_element_type=F32) + b2_ref[...]))
    h = jnp.dot(h, w3_ref[...], precision=hi, preferred_element_type=F32)
    win = win_ref[...]
    hf = h[:, :HY_W] * win
    hb = h[:, HY_W:] * win
    pq = _bdot(fwd, jnp.concatenate([hf + hb, hf - hb], 1))
    p = pq[:, :HY_W]
    q = pq[:, HY_W:]
    row0 = lax.broadcasted_iota(jnp.int32, (l, 1), 0) == 0
    hc = p[:l]
    hc_ref[...] = hc
    hs_ref[...] = jnp.where(row0, 0.0, q[l:])
    hc2_ref[...] = jnp.where(row0, p[l:l + 1], hc)


def _hyena_filter(l, fw1, fb1, ffreq, fw2, fb2, fw3):
    z, window = _filter_consts(l)
    fwd, inv = _dft_mats(l)
    pad_c = LANES - FILT_HID
    w1 = jnp.pad(fw1, ((0, LANES - POS_EMB), (0, pad_c)))
    w2 = jnp.pad(fw2, ((0, pad_c), (0, pad_c)))
    w3 = jnp.pad(fw3, ((0, pad_c), (0, 0)))
    row = lambda a: jnp.pad(a, (0, pad_c)).reshape(1, LANES)
    shp = jax.ShapeDtypeStruct((l, HY_W), F32)
    return pl.pallas_call(
        functools.partial(_hyena_filter_kernel, l=l),
        out_shape=(shp, shp, shp, jax.ShapeDtypeStruct(fwd.shape, BF16), jax.ShapeDtypeStruct(inv.shape, BF16)),
        compiler_params=pltpu.CompilerParams(vmem_limit_bytes=VMEM_LIMIT),
        name=f"hyena_filter_{l}",
    )(z, w1, row(fb1), row(ffreq), w2, row(fb2), w3, window, fwd, inv)


def _hyena_kernel(u_ref, cw_ref, cb_ref, skip_ref, fwd_ref, inv_ref, hc_ref, hs_ref, hc2_ref, o_ref, *, l, nb):
    rows = lax.broadcasted_iota(jnp.int32, (l, 1), 0)
    hs = hs_ref[...]
    for s in range(nb):
        seq = slice(s * l, (s + 1) * l)
        u = u_ref[seq, :]
        prev = jnp.where(rows == 0, 0.0, pltpu.roll(u, 1, 0))
        nxt = jnp.where(rows == l - 1, 0.0, pltpu.roll(u, l - 1, 0))
        uc = prev * cw_ref[0:1, :] + u * cw_ref[1:2, :] + nxt * cw_ref[2:3, :] + cb_ref[...]
        x0 = uc[:, :HY_W]
        x1 = uc[:, HY_W:2 * HY_W]
        v = uc[:, 2 * HY_W:] * x1
        ab = _bdot(fwd_ref[...], v)
        a, b = ab[:l], ab[l:]
        re = a * hc_ref[...] - b * hs
        im = a * hs + b * hc2_ref[...]
        y = _bdot(inv_ref[...], jnp.concatenate([re, im], 0)) * (1.0 / (2 * l))
        o_ref[seq, :] = ((y + skip_ref[...] * v) * x0).astype(o_ref.dtype)


def _hyena(u_all, row_block0, n_seq, l, conv_w, conv_b, skip, filt, into, nb=1):
    hc, hs, hc2, fwd, inv = filt
    const = lambda shape: pl.BlockSpec(shape, lambda s: (0, 0))
    in_specs = [pl.BlockSpec((nb * l, 3 * HY_W), lambda s: (row_block0 + s, 0)),
                const((3, 3 * HY_W)), const((1, 3 * HY_W)), const((1, HY_W)),
                const((2 * l, l)), const((l, 2 * l)),
                const((l, HY_W)), const((l, HY_W)), const((l, HY_W))]
    args = [u_all, conv_w, conv_b.reshape(1, -1), skip.reshape(1, -1), fwd, inv, hc, hs, hc2]
    kern, alias = _fill_rows(functools.partial(_hyena_kernel, l=l, nb=nb), in_specs, args, into)
    return pl.pallas_call(
        kern,
        grid=(n_seq // nb,),
        in_specs=in_specs,
        out_specs=pl.BlockSpec((nb * l, HY_W), lambda s: (row_block0 + s, 0)),
        out_shape=jax.ShapeDtypeStruct((T_ALL, HY_W), BF16),
        input_output_aliases=alias,
        compiler_params=_cparams("parallel"),
        name=f"hyena_{l}",
    )(*args)


def _seg_rms_norm(x, bd_ref, g):
    sq = x * x
    hi = sq.astype(BF16)
    lo = (sq - hi.astype(F32)).astype(BF16)
    bd = bd_ref[...]
    ss = (jnp.dot(hi, bd, preferred_element_type=F32) + jnp.dot(lo, bd, preferred_element_type=F32))
    return x * lax.rsqrt(ss * (1.0 / HEAD_DIM) + EPS) * g


def _rope(x, cos, sin_a, sin_b, quarter):
    w = x.shape[-1]
    return x * cos + pltpu.roll(x, w - quarter, 1) * sin_a + pltpu.roll(x, quarter, 1) * sin_b


def _attn_kernel(*refs, l, qb, lc, rope, qknorm, band, has_sink, emit_k):
    it = iter(refs)
    q_ref, k_ref, v_ref = next(it), next(it), next(it)
    if lc:
        ck_ref, cv_ref = next(it), next(it)
    if rope:
        cq_ref, saq_ref, sbq_ref = next(it), next(it), next(it)
        ckk_ref, sak_ref, sbk_ref = next(it), next(it), next(it)
    if qknorm:
        qg_ref, kg_ref, bdq_ref, bdk_ref = next(it), next(it), next(it), next(it)
    if has_sink:
        sink_ref = next(it)
    o_ref = next(it)
    if emit_k:
        kout_ref = next(it)
    kp_scr = next(it)

    qi = pl.program_id(1)

    @pl.when(qi == 0)
    def _():
        k = k_ref[...]
        if qknorm:
            k = _seg_rms_norm(k, bdk_ref, kg_ref[...])
        if emit_k:
            kout_ref[...] = k
        if rope:
            k = _rope(k, ckk_ref[...], sak_ref[...], sbk_ref[...], HEAD_DIM // 4)
        for kv in range(N_KV):
            kp_scr[kv] = k[:, kv * HEAD_DIM:(kv + 1) * HEAD_DIM].astype(BF16)

    q = q_ref[...]
    if qknorm:
        q = _seg_rms_norm(q, bdq_ref, qg_ref[...])
    if rope:
        q = _rope(q, cq_ref[...], saq_ref[...], sbq_ref[...], HEAD_DIM // 4)
    q = q * HEAD_DIM ** -0.5

    rows = GROUPS * qb
    if band:
        n_loc = 3 * BLOCK
        start = pl.multiple_of(jnp.clip((qi - 1) * BLOCK, 0, l - n_loc), BLOCK)
        keys = pl.ds(start, n_loc)
        tq = qi * qb + lax.broadcasted_iota(jnp.int32, (rows, 1), 0) % qb
        tk = start + lax.broadcasted_iota(jnp.int32, (1, n_loc), 1)
        valid = jnp.abs(tq - tk) <= BLOCK
    else:
        keys = slice(None)

    outs = []
    for kv in range(N_KV):
        lanes = slice(kv * HEAD_DIM, (kv + 1) * HEAD_DIM)
        qs = jnp.concatenate([q[:, (kv * GROUPS + g) * HEAD_DIM:(kv * GROUPS + g + 1) * HEAD_DIM]
                              for g in range(GROUPS)], 0)
        s = _bdot_nt(qs, kp_scr[kv, keys, :])
        if band:
            s = jnp.where(valid, s, -jnp.inf)
        if lc:
            s_c = _bdot_nt(qs, ck_ref[:, lanes])
        e_parts, ec_parts, dens = [], [], []
        for g in range(GROUPS):
            r = slice(g * qb, (g + 1) * qb)
            m = jnp.max(s[r], -1, keepdims=True)
            if lc:
                m = jnp.maximum(m, jnp.max(s_c[r], -1, keepdims=True))
            if has_sink:
                sink = sink_ref[kv * GROUPS + g]
                m = jnp.maximum(m, sink)
            e = jnp.exp(s[r] - m)
            den = jnp.sum(e, -1, keepdims=True)
            e_parts.append(e)
            if lc:
                e_c = jnp.exp(s_c[r] - m)
                den = den + jnp.sum(e_c, -1, keepdims=True)
                ec_parts.append(e_c)
            if has_sink:
                den = den + jnp.exp(sink - m)
            dens.append(den)
        o = _bdot(jnp.concatenate(e_parts, 0), v_ref[keys, lanes])
        if lc:
            o = o + _bdot(jnp.concatenate(ec_parts, 0), cv_ref[:, lanes])
        outs.extend(o[g * qb:(g + 1) * qb] / dens[g] for g in range(GROUPS))
    o_ref[...] = jnp.concatenate(outs, 1).astype(o_ref.dtype)


def _attention(u, *, row0, n_seq, l, q_col, k_col, v_col, into, qb=BLOCK, ctx=None, rope=False, qk_gain=None,
               band=False, sink=None, emit_k=False):
    nq = l // qb
    qw = N_HEADS * HEAD_DIM
    kw = N_KV * HEAD_DIM
    qb0, sb0 = row0 // qb, row0 // l
    seq_spec = lambda col: pl.BlockSpec((l, kw), lambda b, i: (sb0 + b, col // kw))
    const = lambda shape: pl.BlockSpec(shape, lambda b, i: (0, 0))
    in_specs = [pl.BlockSpec((qb, qw), lambda b, i: (qb0 + b * nq + i, q_col // qw)),
                seq_spec(k_col), seq_spec(v_col)]
    args = [u, u, u]
    lc = 0
    if ctx is not None:
        lc = ctx[0].shape[1]
        in_specs += [pl.BlockSpec((None, lc, kw), lambda b, i: (b, 0, 0))] * 2
        args += list(ctx)
    if rope:
        tabs = _rope_tables(l, HEAD_DIM, N_HEADS)
        in_specs += [pl.BlockSpec((qb, qw), lambda b, i: (i, 0))] * 3 + [const((l, kw))] * 3
        args += list(tabs) + list(tabs)
    if qk_gain is not None:
        bd = np.kron(np.eye(N_HEADS), np.ones((HEAD_DIM, HEAD_DIM)))
        in_specs += [const((1, qw)), const((1, kw)), const((qw, qw)), const((kw, kw))]
        args += [jnp.tile(qk_gain[0], N_HEADS).reshape(1, qw), jnp.tile(qk_gain[1], N_KV).reshape(1, kw),
                 jnp.asarray(bd, BF16), jnp.asarray(bd[:kw, :kw], BF16)]
    if sink is not None:
        in_specs.append(pl.BlockSpec(memory_space=pltpu.SMEM))
        args.append(sink)
    out_specs = [pl.BlockSpec((qb, qw), lambda b, i: (qb0 + b * nq + i, 0))]
    out_shape = [jax.ShapeDtypeStruct((T_ALL, qw), BF16)]
    if emit_k:
        out_specs.append(pl.BlockSpec((l, kw), lambda b, i: (b, 0)))
        out_shape.append(jax.ShapeDtypeStruct((n_seq * l, kw), F32))
    kern = functools.partial(_attn_kernel, l=l, qb=qb, lc=lc, rope=rope, qknorm=qk_gain is not None, band=band,
                             has_sink=sink is not None, emit_k=emit_k)
    kern, alias = _fill_rows(kern, in_specs, args, into)
    return pl.pallas_call(
        kern,
        grid=(n_seq, nq),
        in_specs=in_specs,
        out_specs=out_specs,
        out_shape=out_shape,
        input_output_aliases=alias,
        scratch_shapes=[pltpu.VMEM((N_KV, l, HEAD_DIM), BF16)],
        compiler_params=_cparams("parallel", "arbitrary"),
        name=f"attn_{l}_{'b' if sink is not None else 'c'}",
    )(*args)


def _ret_kernel(*refs, l, nb, rope, has_s0, emit_state):
    it = iter(refs)
    dec_ref = next(it)
    rq_ref, rk_ref, rv_ref, gf_ref, gb_ref = next(it), next(it), next(it), next(it), next(it)
    if rope:
        cos_ref, sa_ref, sb_ref = next(it), next(it), next(it)
    if has_s0:
        s0_ref = next(it)
    o_ref = next(it)
    if emit_state:
        st_ref = next(it)

    h = pl.program_id(1)
    qs, ks, vs = [], [], []
    for s in range(nb):
        seq = slice(s * l, (s + 1) * l)
        q = rq_ref[seq, :] * RET_D ** -0.5
        k = rk_ref[seq, :]
        if rope:
            q = _rope(q, cos_ref[...], sa_ref[...], sb_ref[...], RET_D // 4)
            k = _rope(k, cos_ref[...], sa_ref[...], sb_ref[...], RET_D // 4)
        qs.append(q)
        ks.append(k)
        vs.append(rv_ref[seq, :])
    n = l // CHUNK
    ii = lax.broadcasted_iota(jnp.int32, (CHUNK, 1), 0).astype(F32)
    jj = lax.broadcasted_iota(jnp.int32, (1, CHUNK), 1).astype(F32)
    diff = ii - jj
    ys = [None] * nb
    for d in range(2):
        log_g = jnp.log(jax.nn.sigmoid(jnp.full((1, 1), dec_ref[d, h], F32)))
        if d == 0:
            mask = jnp.exp(jnp.where(diff >= 0, diff * log_g, -jnp.inf))
            q_dec = jnp.exp((ii + 1.0) * log_g)
            k_dec = jnp.exp((CHUNK - 1.0 - ii) * log_g)
            order = range(n)
        else:
            mask = jnp.exp(jnp.where(diff <= 0, -diff * log_g, -jnp.inf))
            q_dec = jnp.exp((CHUNK - ii) * log_g)
            k_dec = jnp.exp(ii * log_g)
            order = reversed(range(n))
        c_dec = jnp.exp(CHUNK * log_g)
        order = list(order)
        for s in range(nb):
            seq = slice(s * l, (s + 1) * l)
            q, k, v = qs[s], ks[s], vs[s]
            state = s0_ref[s, d] if has_s0 else jnp.zeros((RET_D, RET_D), F32)
            o_chunks = [None] * n
            for c in order:
                sl = slice(c * CHUNK, (c + 1) * CHUNK)
                qc, kc, vc = q[sl], k[sl], v[sl]
                inner = _bdot_nt(qc, kc) * mask
                o_chunks[c] = _bdot(inner, vc) + _bdot(qc * q_dec, state)
                state = state * c_dec + _bdot_tn(kc * k_dec, vc)
            if emit_state:
                st_ref[s, d] = state
            o = jnp.concatenate(o_chunks, 0)
            o = o * lax.rsqrt(jnp.mean(o * o, -1, keepdims=True) + EPS)
            gate = _silu((gf_ref if d == 0 else gb_ref)[seq, :])
            ys[s] = gate * o if ys[s] is None else ys[s] + gate * o
    for s in range(nb):
        o_ref[s * l:(s + 1) * l, :] = ys[s].astype(o_ref.dtype)


def _retention(u, ret_decay, *, row0, n_seq, l, into, nb=1, rope=False, s0=None, emit_state=False):
    sb0 = row0 // (nb * l)
    col = lambda c0: pl.BlockSpec((nb * l, RET_D), lambda b, h: (sb0 + b, c0 // RET_D + h))
    in_specs = [pl.BlockSpec(memory_space=pltpu.SMEM),
                col(CD_RQ), col(CD_RK), col(CD_RV), col(CD_GF), col(CD_GB)]
    args = [ret_decay, u, u, u, u, u]
    if rope:
        in_specs += [pl.BlockSpec((l, RET_D), lambda b, h: (0, 0))] * 3
        args += list(_rope_tables(l, RET_D, 1))
    state_spec = pl.BlockSpec((nb, 2, None, RET_D, RET_D), lambda b, h: (b, 0, h, 0, 0))
    if s0 is not None:
        in_specs.append(state_spec)
        args.append(s0)
    out_specs = [pl.BlockSpec((nb * l, RET_D), lambda b, h: (sb0 + b, h))]
    out_shape = [jax.ShapeDtypeStruct((T_ALL, RET_HEADS * RET_D), BF16)]
    if emit_state:
        out_specs.append(state_spec)
        out_shape.append(jax.ShapeDtypeStruct((n_seq, 2, RET_HEADS, RET_D, RET_D), F32))
    kern = functools.partial(_ret_kernel, l=l, nb=nb, rope=rope, has_s0=s0 is not None, emit_state=emit_state)
    kern, alias = _fill_rows(kern, in_specs, args, into)
    return pl.pallas_call(
        kern,
        grid=(n_seq // nb, RET_HEADS),
        in_specs=in_specs,
        out_specs=out_specs,
        out_shape=out_shape,
        input_output_aliases=alias,
        compiler_params=_cparams("parallel", "parallel"),
        name=f"retention_{l}",
    )(*args)


def _outproj_kernel(*refs, tm, slabs):
    ya_ref, yb_ref, w_ref = refs[:3]
    x_refs = refs[3:3 + len(slabs)]
    gate_ref, lng_ref, lnb_ref, o_ref, w_scr = refs[3 + len(slabs):]
    i = pl.program_id(0)

    @pl.when(i == 0)
    def _():
        w_scr[...] = w_ref[...].astype(BF16)

    half = ya_ref.shape[1]
    m = (jnp.dot(ya_ref[...], w_scr[:half], preferred_element_type=F32)
         + jnp.dot(yb_ref[...], w_scr[half:], preferred_element_type=F32))
    gm = gate_ref[pl.ds(_group_of_tile(i, tm), 1), :] * m

    for x_ref, (first, end) in zip(x_refs, slabs):
        @pl.when((i >= first) & (i < end))
        def _():
            o_ref[...] = _layer_norm(ALPHA * x_ref[...] + gm, lng_ref[...], lnb_ref[...])


def _outproj(ya, yb, w, xs, mod, ln_g, ln_b):
    tm = 1024
    half = ya.shape[1]
    slabs = _slab_tiles(xs, tm)
    const = lambda shape: pl.BlockSpec(shape, lambda i: (0, 0))

    def slab_spec(first, end):
        return pl.BlockSpec((tm, D_MODEL), lambda i: (jnp.clip(i - first, 0, end - first - 1), 0))

    return pl.pallas_call(
        functools.partial(_outproj_kernel, tm=tm, slabs=slabs),
        grid=(T_ALL // tm,),
        in_specs=[pl.BlockSpec((tm, half), lambda i: (i, 0)),
                  pl.BlockSpec((tm, half), lambda i: (i, 0)),
                  const((2 * half, D_MODEL))]
        + [slab_spec(*s) for s in slabs]
        + [pl.BlockSpec((MOD_ROWS, D_MODEL), lambda i: (0, 2)),
           const((1, D_MODEL)), const((1, D_MODEL))],
        out_specs=pl.BlockSpec((tm, D_MODEL), lambda i: (i, 0)),
        out_shape=jax.ShapeDtypeStruct((T_ALL, D_MODEL), F32),
        scratch_shapes=[pltpu.VMEM((2 * half, D_MODEL), BF16)],
        compiler_params=_cparams("arbitrary"),
        name="outproj_ln",
    )(ya, yb, w, *xs, mod, ln_g.reshape(1, -1), ln_b.reshape(1, -1))


TOK_TILE = 256
N_TOK_TILES = T_ALL // TOK_TILE
SORT_TILE = 256
EXP_TILE = 512
N_EXP_TILES = (2 * T_ALL) // EXP_TILE + N_EXPERTS
N_SORT_TILES = N_EXP_TILES * (EXP_TILE // SORT_TILE)
N_PROMPT_TOK_TILES = T_PROMPT // TOK_TILE
CUM_ROWS = 32


def _route_kernel(x_ref, sh_ref, sc_ref, r_ref, h_ref, rank_ref, rank_t_ref, gate_t_ref, cum_ref,
                  carry_row, carry_col):
    c = pl.program_id(0)

    @pl.when(c == 0)
    def _():
        carry_row[...] = jnp.zeros_like(carry_row)
        carry_col[...] = jnp.zeros_like(carry_col)
        cum_ref[...] = jnp.zeros_like(cum_ref)

    g = _group_of_tile(c, TOK_TILE)
    h = x_ref[...] * (1.0 + sc_ref[pl.ds(g, 1), :]) + sh_ref[pl.ds(g, 1), :]
    h_hi = h.astype(BF16)
    h_ref[...] = h_hi
    r = r_ref[...]
    r_hi = r.astype(BF16)
    h_lo = (h - h_hi.astype(F32)).astype(BF16)
    r_lo = (r - r_hi.astype(F32)).astype(BF16)
    logits = (jnp.dot(h_hi, r_hi, preferred_element_type=F32) + jnp.dot(h_lo, r_hi, preferred_element_type=F32)
              + jnp.dot(h_hi, r_lo, preferred_element_type=F32))
    lane = lax.broadcasted_iota(jnp.int32, logits.shape, 1)
    logits = jnp.where(lane < N_EXPERTS, logits, -jnp.inf)
    m1 = jnp.max(logits, -1, keepdims=True)
    i1 = jnp.min(jnp.where(logits == m1, lane, LANES), -1, keepdims=True)
    rest = jnp.where(lane == i1, -jnp.inf, logits)
    m2 = jnp.max(rest, -1, keepdims=True)
    i2 = jnp.min(jnp.where(rest == m2, lane, LANES), -1, keepdims=True)
    e2 = jnp.exp(m2 - m1)
    den = 1.0 + e2
    gates = jnp.where(lane == i1, 1.0 / den, 0.0) + jnp.where(lane == i2, e2 / den, 0.0)
    sel = jnp.where((lane == i1) | (lane == i2), 1.0, 0.0)
    sel_t = sel.T
    ti = lax.broadcasted_iota(jnp.int32, (TOK_TILE, TOK_TILE), 0)
    tj = lax.broadcasted_iota(jnp.int32, (TOK_TILE, TOK_TILE), 1)
    before = jnp.where(tj < ti, 1.0, 0.0).astype(BF16)
    rank = jnp.dot(before, sel.astype(BF16), preferred_element_type=F32) + carry_row[...]
    rank_t = lax.dot_general(sel_t.astype(BF16), before, (((1,), (1,)), ((), ())),
                             preferred_element_type=F32) + carry_col[...]
    rank_ref[...] = jnp.where(sel > 0.0, rank, -1.0)
    rank_t_ref[...] = jnp.where(sel_t > 0.0, rank_t, -1.0)[:N_EXPERTS]
    gate_t_ref[...] = gates.T[:N_EXPERTS]
    cum_ref[pl.ds(c, 1), :] = carry_row[...].astype(jnp.int32)
    carry_row[...] += jnp.sum(sel, 0, keepdims=True)
    carry_col[...] += jnp.sum(sel_t, 1, keepdims=True)

    @pl.when(c == N_TOK_TILES - 1)
    def _():
        cum_ref[pl.ds(N_TOK_TILES, 1), :] = carry_row[...].astype(jnp.int32)


def _route(x, mod, router):
    tile = lambda w: pl.BlockSpec((TOK_TILE, w), lambda c: (c, 0))
    tile_t = pl.BlockSpec((N_EXPERTS, TOK_TILE), lambda c: (0, c))
    return pl.pallas_call(
        _route_kernel,
        grid=(N_TOK_TILES,),
        in_specs=[tile(D_MODEL),
                  pl.BlockSpec((MOD_ROWS, D_MODEL), lambda c: (0, 3)),
                  pl.BlockSpec((MOD_ROWS, D_MODEL), lambda c: (0, 4)),
                  pl.BlockSpec((D_MODEL, LANES), lambda c: (0, 0))],
        out_specs=[tile(D_MODEL), tile(LANES), tile_t, tile_t,
                   pl.BlockSpec((CUM_ROWS, LANES), lambda c: (0, 0))],
        out_shape=[jax.ShapeDtypeStruct((T_ALL, D_MODEL), BF16),
                   jax.ShapeDtypeStruct((T_ALL, LANES), F32),
                   jax.ShapeDtypeStruct((N_EXPERTS, T_ALL), F32),
                   jax.ShapeDtypeStruct((N_EXPERTS, T_ALL), F32),
                   jax.ShapeDtypeStruct((CUM_ROWS, LANES), jnp.int32)],
        scratch_shapes=[pltpu.VMEM((1, LANES), F32), pltpu.VMEM((LANES, 1), F32)],
        compiler_params=_cparams("arbitrary"),
        name="route",
    )(x, mod, mod, jnp.pad(router, ((0, 0), (0, LANES - N_EXPERTS))))


def _gather_kernel(te_ref, off_ref, cum_ref, nt_ref, h_ref, rank_t_ref, gate_t_ref, xs_ref, gs_ref, acc_scr, g_scr):
    i = pl.program_id(0)
    e = te_ref[i]
    r0 = i * SORT_TILE - off_ref[e]
    acc_scr[...] = jnp.zeros_like(acc_scr)
    g_scr[...] = jnp.zeros_like(g_scr)
    want = (r0 + lax.broadcasted_iota(jnp.int32, (SORT_TILE, 1), 0)).astype(F32)
    for c in range(N_TOK_TILES):
        lo = cum_ref[c * N_EXPERTS + e]
        hi = cum_ref[(c + 1) * N_EXPERTS + e]

        @pl.when((i < nt_ref[0]) & (lo < r0 + SORT_TILE) & (hi > r0))
        def _():
            cols = slice(c * TOK_TILE, (c + 1) * TOK_TILE)
            pick = rank_t_ref[pl.ds(e, 1), cols] == want
            acc_scr[...] += jnp.dot(jnp.where(pick, 1.0, 0.0).astype(BF16), h_ref[cols, :],
                                    preferred_element_type=F32)
            g_scr[...] += jnp.sum(jnp.where(pick, gate_t_ref[pl.ds(e, 1), cols], 0.0), -1, keepdims=True)

    xs_ref[...] = acc_scr[...].astype(BF16)
    gs_ref[...] = jnp.broadcast_to(g_scr[...], gs_ref.shape)


def _gather(tile_expert, off, cum, n_tiles, h, rank_t, gate_t):
    const = lambda shape: pl.BlockSpec(shape, lambda i, *_: (0, 0))
    return pl.pallas_call(
        _gather_kernel,
        grid_spec=pltpu.PrefetchScalarGridSpec(
            num_scalar_prefetch=4,
            grid=(N_SORT_TILES,),
            in_specs=[const((T_ALL, D_MODEL)), const((N_EXPERTS, T_ALL)), const((N_EXPERTS, T_ALL))],
            out_specs=[pl.BlockSpec((SORT_TILE, D_MODEL), lambda i, *_: (i, 0)),
                       pl.BlockSpec((SORT_TILE, LANES), lambda i, *_: (i, 0))],
            scratch_shapes=[pltpu.VMEM((SORT_TILE, D_MODEL), F32), pltpu.VMEM((SORT_TILE, 1), F32)]),
        out_shape=[jax.ShapeDtypeStruct((N_SORT_TILES * SORT_TILE, D_MODEL), BF16),
                   jax.ShapeDtypeStruct((N_SORT_TILES * SORT_TILE, LANES), F32)],
        compiler_params=_cparams("parallel"),
        name="moe_gather",
    )(tile_expert, off, cum, n_tiles, h, rank_t, gate_t)


def _expert_kernel(te_ref, nt_ref, xs_ref, gs_ref, w1_ref, w3_ref, w2_ref, ys_ref, w1_scr, w3_scr, w2_scr, acc_scr):
    i = pl.program_id(0)

    @pl.when((i == 0) | (te_ref[i] != te_ref[jnp.maximum(i - 1, 0)]))
    def _():
        w1_scr[...] = w1_ref[...].astype(BF16)
        w3_scr[...] = w3_ref[...].astype(BF16)
        w2_scr[...] = w2_ref[...].astype(BF16)

    @pl.when(i < nt_ref[0])
    def _():
        x = xs_ref[...]
        gate = gs_ref[:, 0:1]
        acc_scr[...] = jnp.zeros_like(acc_scr)
        for c0 in range(0, EXPERT_FF, FF_CHUNK):
            cs = slice(c0, min(c0 + FF_CHUNK, EXPERT_FF))
            a = jnp.dot(x, w1_scr[:, cs], preferred_element_type=F32)
            b = jnp.dot(x, w3_scr[:, cs], preferred_element_type=F32)
            act = (_silu(a) * b * gate).astype(BF16)
            acc_scr[...] += jnp.dot(act, w2_scr[cs, :], preferred_element_type=F32)
        ys_ref[...] = acc_scr[...].astype(BF16)

    @pl.when(i >= nt_ref[0])
    def _():
        ys_ref[...] = jnp.zeros_like(ys_ref)


def _experts(tile_expert, n_tiles, xs, gs, w1, w3, w2):
    w_in = pl.BlockSpec((None, D_MODEL, EXPERT_FF), lambda i, te, nt: (te[i], 0, 0))
    w_out = pl.BlockSpec((None, EXPERT_FF, D_MODEL), lambda i, te, nt: (te[i], 0, 0))
    return pl.pallas_call(
        _expert_kernel,
        grid_spec=pltpu.PrefetchScalarGridSpec(
            num_scalar_prefetch=2,
            grid=(N_EXP_TILES,),
            in_specs=[pl.BlockSpec((EXP_TILE, D_MODEL), lambda i, te, nt: (i, 0)),
                      pl.BlockSpec((EXP_TILE, LANES), lambda i, te, nt: (i, 0)),
                      w_in, w_in, w_out],
            out_specs=pl.BlockSpec((EXP_TILE, D_MODEL), lambda i, te, nt: (i, 0)),
            scratch_shapes=[pltpu.VMEM((D_MODEL, EXPERT_FF), BF16), pltpu.VMEM((D_MODEL, EXPERT_FF), BF16),
                            pltpu.VMEM((EXPERT_FF, D_MODEL), BF16), pltpu.VMEM((EXP_TILE, D_MODEL), F32)]),
        out_shape=jax.ShapeDtypeStruct((N_EXP_TILES * EXP_TILE, D_MODEL), BF16),
        compiler_params=_cparams("arbitrary"),
        name="moe_experts",
    )(tile_expert, n_tiles, xs, gs, w1, w3, w2)


def _combine_kernel(off_ref, cum_ref, ys_ref, rank_ref, x_ref, gate_ref, lng_ref, lnb_ref, op_ref, os_ref, acc_scr):
    c = pl.program_id(0)
    acc_scr[...] = jnp.zeros_like(acc_scr)
    rank = rank_ref[...]
    lane = lax.broadcasted_iota(jnp.int32, rank.shape, 1)
    cols = lax.broadcasted_iota(jnp.int32, (1, SORT_TILE), 1)
    for e in range(N_EXPERTS):
        lo = off_ref[e] + cum_ref[c * N_EXPERTS + e]
        hi = off_ref[e] + cum_ref[(c + 1) * N_EXPERTS + e]
        r = jnp.sum(jnp.where(lane == e, rank, 0.0), -1, keepdims=True)
        pos = jnp.where(r >= 0.0, r + jnp.full((1, 1), off_ref[e], jnp.int32).astype(F32), -1.0)
        first = lo // SORT_TILE
        for k in range(2):
            s = first + k

            @pl.when((hi > lo) & (s * SORT_TILE < hi))
            def _():
                pick = pos == (s * SORT_TILE + cols).astype(F32)
                rows = ys_ref[pl.ds(pl.multiple_of(s * SORT_TILE, SORT_TILE), SORT_TILE), :]
                acc_scr[...] += jnp.dot(jnp.where(pick, 1.0, 0.0).astype(BF16), rows, preferred_element_type=F32)

    g = _group_of_tile(c, TOK_TILE)
    z = ALPHA * x_ref[...] + gate_ref[pl.ds(g, 1), :] * acc_scr[...]
    y = _layer_norm(z, lng_ref[...], lnb_ref[...])

    @pl.when(c < N_PROMPT_TOK_TILES)
    def _():
        op_ref[...] = y

    @pl.when(c >= N_PROMPT_TOK_TILES)
    def _():
        os_ref[...] = y


def _combine(off, cum, ys, rank, x, mod, ln_g, ln_b):
    const = lambda shape: pl.BlockSpec(shape, lambda c, *_: (0, 0))
    last_p = N_PROMPT_TOK_TILES - 1
    return pl.pallas_call(
        _combine_kernel,
        grid_spec=pltpu.PrefetchScalarGridSpec(
            num_scalar_prefetch=2,
            grid=(N_TOK_TILES,),
            in_specs=[pl.BlockSpec(ys.shape, lambda c, *_: (0, 0), pipeline_mode=pl.Buffered(1)),
                      pl.BlockSpec((TOK_TILE, LANES), lambda c, *_: (c, 0)),
                      pl.BlockSpec((TOK_TILE, D_MODEL), lambda c, *_: (c, 0)),
                      pl.BlockSpec((MOD_ROWS, D_MODEL), lambda c, *_: (0, 5)),
                      const((1, D_MODEL)), const((1, D_MODEL))],
            out_specs=[pl.BlockSpec((TOK_TILE, D_MODEL), lambda c, *_: (jnp.minimum(c, last_p), 0)),
                       pl.BlockSpec((TOK_TILE, D_MODEL), lambda c, *_: (jnp.maximum(c - last_p - 1, 0), 0))],
            scratch_shapes=[pltpu.VMEM((TOK_TILE, D_MODEL), F32)]),
        out_shape=[jax.ShapeDtypeStruct((T_PROMPT, D_MODEL), F32), jax.ShapeDtypeStruct((T_SAMPLE, D_MODEL), F32)],
        compiler_params=_cparams("arbitrary"),
        name="moe_combine_ln",
    )(off, cum, ys, rank, x, mod, ln_g.reshape(1, -1), ln_b.reshape(1, -1))


def _moe(x, mod, ln_g, ln_b, router, w1, w3, w2):
    h, rank, rank_t, gate_t, cum = _route(x, mod, router)
    counts = cum[N_TOK_TILES, :N_EXPERTS]
    tiles = (counts + EXP_TILE - 1) // EXP_TILE
    ends = jnp.cumsum(tiles)
    off = ((ends - tiles) * EXP_TILE).astype(jnp.int32)
    n_tiles = ends[-1:].astype(jnp.int32)
    tile_ids = jnp.minimum(jnp.arange(N_EXP_TILES, dtype=jnp.int32), n_tiles - 1)
    tile_expert = jnp.sum((tile_ids[:, None] >= ends[None, :]).astype(jnp.int32), -1)
    sub = EXP_TILE // SORT_TILE
    cum_flat = cum[:N_TOK_TILES + 1, :N_EXPERTS].reshape(-1)
    xs, gs = _gather(jnp.repeat(tile_expert, sub), off, cum_flat, n_tiles * sub, h, rank_t, gate_t)
    ys = _experts(tile_expert, n_tiles, xs, gs, w1, w3, w2)
    return _combine(off, cum_flat, ys, rank, x, mod, ln_g, ln_b)


FF_CHUNK = 256


def _ffn_kernel(x_ref, sh_ref, sc_ref, gate_ref, lng_ref, lnb_ref, w1_ref, w3_ref, w2_ref, o_ref, h_scr, acc_scr,
                *, tm, ff):
    i, j = pl.program_id(0), pl.program_id(1)
    g = _group_of_tile(i, tm)

    @pl.when(j == 0)
    def _():
        h_scr[...] = (x_ref[...] * (1.0 + sc_ref[pl.ds(g, 1), :]) + sh_ref[pl.ds(g, 1), :]).astype(BF16)
        acc_scr[...] = jnp.zeros_like(acc_scr)

    h = h_scr[...]
    for c0 in range(0, ff, FF_CHUNK):
        cs = slice(c0, min(c0 + FF_CHUNK, ff))
        a = jnp.dot(h, w1_ref[:, cs].astype(BF16), preferred_element_type=F32)
        b = jnp.dot(h, w3_ref[:, cs].astype(BF16), preferred_element_type=F32)
        act = _silu(a) * b
        acc_scr[...] += jnp.dot(act.astype(BF16), w2_ref[cs, :].astype(BF16), preferred_element_type=F32)

    @pl.when(j == pl.num_programs(1) - 1)
    def _():
        z = ALPHA * x_ref[...] + gate_ref[pl.ds(g, 1), :] * acc_scr[...]
        o_ref[...] = _layer_norm(z, lng_ref[...], lnb_ref[...])


def _ffn(x, mod, ln_g, ln_b, w1, w3, w2):
    tm = 1024
    ff = FF_CHUNK
    n_j = D_FF // ff
    w_in_spec = pl.BlockSpec((D_MODEL, ff), lambda i, j: (0, j))
    w_out_spec = pl.BlockSpec((ff, D_MODEL), lambda i, j: (j, 0))
    mod_spec = lambda col: pl.BlockSpec((MOD_ROWS, D_MODEL), lambda i, j: (0, col))
    const = lambda shape: pl.BlockSpec(shape, lambda i, j: (0, 0))
    in_specs = [pl.BlockSpec((tm, D_MODEL), lambda i, j: (i, 0)),
                mod_spec(3), mod_spec(4), mod_spec(5),
                const((1, D_MODEL)), const((1, D_MODEL)),
                w_in_spec, w_in_spec, w_out_spec]
    args = [x, mod, mod, mod, ln_g.reshape(1, -1), ln_b.reshape(1, -1), w1, w3, w2]
    return pl.pallas_call(
        functools.partial(_ffn_kernel, tm=tm, ff=ff),
        grid=(T_ALL // tm, n_j),
        in_specs=in_specs,
        out_specs=pl.BlockSpec((tm, D_MODEL), lambda i, j: (i, 0)),
        out_shape=jax.ShapeDtypeStruct((T_ALL, D_MODEL), F32),
        scratch_shapes=[pltpu.VMEM((tm, D_MODEL), BF16), pltpu.VMEM((tm, D_MODEL), F32)],
        compiler_params=_cparams("parallel", "arbitrary"),
        name="ffn",
    )(*args)


def kernel(x_prompt, x_sample, c, cache_k_b, cache_v_b, cache_k_c, cache_v_c, state_ret, c_ctx, ada_w, ada_b, ln_g, ln_b, w_in_ab, hy_conv_w, hy_conv_b, hf_w1, hf_b1, hf_freq, hf_w2, hf_b2, hf_w3, hy_skip, sink_b, w_out_ab, ffn_w1, ffn_w3, ffn_w2, w_in_cd, qn_g, kn_g, ret_decay, w_out_cd, moe_router, moe_w1, moe_w3, moe_w2):
    x_in = (x_prompt.reshape(T_PROMPT, D_MODEL), x_sample.reshape(T_SAMPLE, D_MODEL))
    cvec = jnp.concatenate([c_ctx[None], c, jnp.zeros((MOD_ROWS - 1 - DEC_BATCH, D_MODEL), F32)], 0)
    mod = _modulation(cvec, ada_w, ada_b)
    kw = N_KV * HEAD_DIM
    prompt = dict(row0=0, n_seq=BATCH, l=SEQ)
    sample = dict(row0=T_PROMPT, n_seq=DEC_BATCH, l=DEC_SEQ)

    u, k_b, v_b = _inproj(x_in, mod[0], w_in_ab[0], tn=1152, taps=(AB_K, AB_V))
    filt_args = (hf_w1[0], hf_b1[0], hf_freq[0], hf_w2[0], hf_b2[0], hf_w3[0])
    hy_args = (hy_conv_w[0], hy_conv_b[0], hy_skip[0])
    blank = lambda w: jnp.zeros((T_ALL, w), BF16)
    ya = _hyena(u, 0, BATCH, SEQ, *hy_args, _hyena_filter(SEQ, *filt_args), into=blank(HY_W), nb=4)
    ya = _hyena(u, T_PROMPT // DEC_SEQ, DEC_BATCH, DEC_SEQ, *hy_args, _hyena_filter(DEC_SEQ, *filt_args), into=ya)
    cols_b = dict(q_col=AB_Q, k_col=AB_K, v_col=AB_V)
    ctx_b = (cache_k_b[:, 0].reshape(DEC_BATCH, PAST_LEN, kw), cache_v_b[:, 0].reshape(DEC_BATCH, PAST_LEN, kw))
    yb, = _attention(u, **prompt, **cols_b, qb=SEQ, sink=sink_b[0], into=blank(N_HEADS * HEAD_DIM))
    yb, = _attention(u, **sample, **cols_b, sink=sink_b[0], ctx=ctx_b, rope=True, band=True, into=yb)
    k_b = k_b.reshape(BATCH, 1, SEQ, N_KV, HEAD_DIM)
    v_b = v_b.reshape(BATCH, 1, SEQ, N_KV, HEAD_DIM)
    x = _outproj(ya, yb, w_out_ab[0], x_in, mod[0], ln_g[0, 0], ln_b[0, 0])
    x = _ffn(x, mod[0], ln_g[0, 1], ln_b[0, 1], ffn_w1[0], ffn_w3[0], ffn_w2[0])

    u, v_c = _inproj((x,), mod[1], w_in_cd[0], tn=1664, taps=(CD_V,))
    cols_c = dict(q_col=CD_Q, k_col=CD_K, v_col=CD_V)
    gains = (qn_g[0], kn_g[0])
    ctx_c = (cache_k_c[:, 0].reshape(DEC_BATCH, PAST_LEN, kw), cache_v_c[:, 0].reshape(DEC_BATCH, PAST_LEN, kw))
    yc, k_c = _attention(u, **prompt, **cols_c, qb=SEQ, qk_gain=gains, emit_k=True, into=blank(N_HEADS * HEAD_DIM))
    yc, = _attention(u, **sample, **cols_c, qb=2 * BLOCK, qk_gain=gains, ctx=ctx_c, rope=True, into=yc)
    yd, s_r = _retention(u, ret_decay[0], **prompt, nb=4, emit_state=True, into=blank(RET_HEADS * RET_D))
    yd, = _retention(u, ret_decay[0], **sample, nb=DEC_BATCH, rope=True, s0=state_ret[:, 0], into=yd)
    k_c = k_c.reshape(BATCH, 1, SEQ, N_KV, HEAD_DIM)
    v_c = v_c.reshape(BATCH, 1, SEQ, N_KV, HEAD_DIM)
    x = _outproj(yc, yd, w_out_cd[0], (x,), mod[1], ln_g[1, 0], ln_b[1, 0])
    y_prompt, y_sample = _moe(x, mod[1], ln_g[1, 1], ln_b[1, 1], moe_router[0], moe_w1[0], moe_w3[0], moe_w2[0])

    return (y_prompt.reshape(BATCH, SEQ, D_MODEL), y_sample.reshape(DEC_BATCH, DEC_SEQ, D_MODEL),
            k_b, v_b, k_c, v_c, s_r[:, None])
```

```python
import functools
import math

import numpy as np
import jax
import jax.numpy as jnp
from jax import lax
from jax.experimental import pallas as pl
from jax.experimental.pallas import tpu as pltpu

F32 = jnp.float32
BF16 = jnp.bfloat16

D_MODEL = 1024
BATCH = 16
SEQ = 256
DEC_BATCH = 2
DEC_SEQ = 1024
PAST_LEN = 512
GRID_W = 64
HEAD_DIM = 64
BLOCK = 128
HY_W = 512
POS_BANDS = 16
POS_EMB = 1 + 2 * POS_BANDS
FILT_HID = 64
HY_FAST_DECAY = 0.3
HY_SLOW_DECAY = 1.5
HY_TARGET = 1e-2
N_HEADS = 8
N_KV = 2
GROUPS = N_HEADS // N_KV
RET_HEADS = 4
RET_D = 128
CHUNK = 128
ROPE_BASE = 10000.0
D_FF = 2816
N_EXPERTS = 8
EXPERT_FF = 1408
DEPTH = 2
ALPHA = (2 * DEPTH) ** 0.25
EPS = 1e-6

T_PROMPT = BATCH * SEQ
T_SAMPLE = DEC_BATCH * DEC_SEQ
T_ALL = T_PROMPT + T_SAMPLE
GROUP_ROWS = 1024
N_PROMPT_GROUPS = T_PROMPT // GROUP_ROWS
MOD_ROWS = 16
LANES = 128
VMEM_LIMIT = 58 * 1024 * 1024

AB_Q = 3 * HY_W
AB_K = AB_Q + N_HEADS * HEAD_DIM
AB_V = AB_K + N_KV * HEAD_DIM
IN_AB = AB_V + N_KV * HEAD_DIM
CD_Q = 0
CD_K = N_HEADS * HEAD_DIM
CD_V = CD_K + N_KV * HEAD_DIM
CD_RQ = CD_V + N_KV * HEAD_DIM
CD_RK = CD_RQ + RET_HEADS * RET_D
CD_RV = CD_RK + RET_HEADS * RET_D
CD_GF = CD_RV + RET_HEADS * RET_D
CD_GB = CD_GF + RET_HEADS * RET_D
IN_CD = CD_GB + RET_HEADS * RET_D


def _cparams(*sem):
    return pltpu.CompilerParams(dimension_semantics=sem, vmem_limit_bytes=VMEM_LIMIT)


def _silu(x):
    return x * jax.nn.sigmoid(x)


def _bdot(a, b):
    return jnp.dot(a.astype(BF16), b.astype(BF16), preferred_element_type=F32)


def _bdot_nt(a, b):
    return lax.dot_general(a.astype(BF16), b.astype(BF16), (((1,), (1,)), ((), ())),
                           preferred_element_type=F32)


def _bdot_tn(a, b):
    return lax.dot_general(a.astype(BF16), b.astype(BF16), (((0,), (0,)), ((), ())),
                           preferred_element_type=F32)


def _layer_norm(z, g, b):
    mu = jnp.mean(z, -1, keepdims=True)
    zc = z - mu
    var = jnp.mean(zc * zc, -1, keepdims=True)
    return zc * lax.rsqrt(var + EPS) * g + b


def _fill_rows(kern, in_specs, args, into):
    n_in = len(args)
    in_specs.append(pl.BlockSpec(memory_space=pl.ANY))
    args.append(into)

    def kern_into(*refs):
        return kern(*refs[:n_in], *refs[n_in + 1:])

    return kern_into, {n_in: 0}


def _group_of_tile(i, tm):
    return jnp.maximum(i // (GROUP_ROWS // tm) - (N_PROMPT_GROUPS - 1), 0)


def _rope_tables(n_tokens, d, reps):
    nf = d // 4
    inv = ROPE_BASE ** (-np.arange(nf, dtype=np.float64) / nf)
    pos = np.arange(n_tokens)
    row, col = pos // GRID_W, pos % GRID_W
    ang_r = row[:, None] * inv[None, :]
    ang_c = col[:, None] * inv[None, :]
    zeros = np.zeros_like(ang_r)
    cos = np.concatenate([np.cos(ang_r), np.cos(ang_r), np.cos(ang_c), np.cos(ang_c)], -1)
    sin_a = np.concatenate([-np.sin(ang_r), zeros, -np.sin(ang_c), zeros], -1)
    sin_b = np.concatenate([zeros, np.sin(ang_r), zeros, np.sin(ang_c)], -1)
    tile = lambda a: jnp.asarray(np.tile(a, (1, reps)), F32)
    return tile(cos), tile(sin_a), tile(sin_b)


def _dft_mats(l):
    n = 2 * l
    k = np.arange(l, dtype=np.float64)
    ang = 2.0 * np.pi * np.outer(k, k) / n
    fc = np.cos(ang)
    fs = np.sin(ang)
    fs[0, :] = np.cos(np.pi * k)
    fwd = np.concatenate([fc, fs], 0)
    wk = np.full((l,), 2.0)
    wk[0] = 1.0
    inv = np.concatenate([fc.T * wk[None, :], fs.T * wk[None, :]], 1)
    return jnp.asarray(fwd, F32), jnp.asarray(inv, F32)


def _filter_consts(l):
    t = np.linspace(0.0, 1.0, l, dtype=np.float32).astype(np.float64)[:, None]
    w = (2.0 * math.pi * np.arange(l, dtype=np.float64) / l)[:, None]
    bands = np.linspace(1e-4, POS_BANDS - 1.0, POS_BANDS, dtype=np.float32).astype(np.float64)[None, :]
    z = np.concatenate([t, np.cos(bands * w), -np.sin(bands * w)], -1)
    z = np.pad(z, ((0, 0), (0, LANES - POS_EMB)))
    max_decay = math.log(HY_TARGET) / HY_FAST_DECAY
    min_decay = math.log(HY_TARGET) / HY_SLOW_DECAY
    deltas = np.linspace(min_decay, max_decay, HY_W, dtype=np.float32).astype(np.float64)
    window = np.exp(-t * np.abs(deltas)[None, :])
    return jnp.asarray(z, F32), jnp.asarray(window, F32)


def _mod_kernel(c_ref, w_ref, b_ref, o_ref):
    o_ref[...] = _bdot(_silu(c_ref[...]), w_ref[...]) + b_ref[...]


def _modulation(cvec, ada_w, ada_b):
    tn = 1536
    n = ada_w.shape[-1]
    return pl.pallas_call(
        _mod_kernel,
        grid=(DEPTH, n // tn),
        in_specs=[pl.BlockSpec((MOD_ROWS, D_MODEL), lambda l, j: (0, 0)),
                  pl.BlockSpec((None, D_MODEL, tn), lambda l, j: (l, 0, j)),
                  pl.BlockSpec((None, 1, tn), lambda l, j: (l, 0, j))],
        out_specs=pl.BlockSpec((None, MOD_ROWS, tn), lambda l, j: (l, 0, j)),
        out_shape=jax.ShapeDtypeStruct((DEPTH, MOD_ROWS, n), F32),
        compiler_params=_cparams("parallel", "parallel"),
        name="mod",
    )(cvec, ada_w, ada_b.reshape(DEPTH, 1, n))


def _slab_tiles(xs, tm):
    ends = np.cumsum([x.shape[0] // tm for x in xs])
    return [(int(e - x.shape[0] // tm), int(e)) for x, e in zip(xs, ends)]


def _inproj_kernel(*refs, tm, tn, slabs, taps):
    x_refs = refs[:len(slabs)]
    sh_ref, sc_ref, w_ref, o_ref = refs[len(slabs):len(slabs) + 4]
    tap_refs = refs[len(slabs) + 4:len(slabs) + 4 + len(taps)]
    h_scr, w_scr = refs[len(slabs) + 4 + len(taps):]
    j, i = pl.program_id(0), pl.program_id(1)
    rows = pl.ds(pl.multiple_of(i * tm, tm), tm)
    g = _group_of_tile(i, tm)

    for x_ref, (first, end) in zip(x_refs, slabs):
        @pl.when((j == 0) & (i >= first) & (i < end))
        def _():
            sc = sc_ref[pl.ds(g, 1), :]
            sh = sh_ref[pl.ds(g, 1), :]
            h_scr[rows, :] = (x_ref[...] * (1.0 + sc) + sh).astype(BF16)

    @pl.when(i == 0)
    def _():
        w_scr[...] = w_ref[...].astype(BF16)

    y = jnp.dot(h_scr[rows, :], w_scr[...], preferred_element_type=F32)
    o_ref[...] = y

    for tap_ref, col in zip(tap_refs, taps):
        @pl.when((j == col // tn) & (i < T_PROMPT // tm))
        def _():
            tap_ref[...] = y[:, col % tn:col % tn + tap_ref.shape[1]]


def _inproj(xs, mod, w, tn, taps):
    tm = 1024
    n = w.shape[1]
    slabs = _slab_tiles(xs, tm)
    tap_w = N_KV * HEAD_DIM
    last_p = T_PROMPT // tm - 1

    def tap_spec(col):
        jt = col // tn
        return pl.BlockSpec((tm, tap_w), lambda j, i: (
            jnp.where(j < jt, 0, jnp.where(j == jt, jnp.minimum(i, last_p), last_p)), 0))

    def slab_spec(first, end):
        last = end - first - 1
        return pl.BlockSpec((tm, D_MODEL), lambda j, i: (jnp.where(j == 0, jnp.clip(i - first, 0, last), last), 0))

    return pl.pallas_call(
        functools.partial(_inproj_kernel, tm=tm, tn=tn, slabs=slabs, taps=taps),
        grid=(n // tn, T_ALL // tm),
        in_specs=[slab_spec(*s) for s in slabs]
        + [pl.BlockSpec((MOD_ROWS, D_MODEL), lambda j, i: (0, 0)),
           pl.BlockSpec((MOD_ROWS, D_MODEL), lambda j, i: (0, 1)),
           pl.BlockSpec((D_MODEL, tn), lambda j, i: (0, j))],
        out_specs=[pl.BlockSpec((tm, tn), lambda j, i: (i, j))] + [tap_spec(c) for c in taps],
        out_shape=[jax.ShapeDtypeStruct((T_ALL, n), F32)]
        + [jax.ShapeDtypeStruct((T_PROMPT, tap_w), F32) for _ in taps],
        scratch_shapes=[pltpu.VMEM((T_ALL, D_MODEL), BF16), pltpu.VMEM((D_MODEL, tn), BF16)],
        compiler_params=_cparams("arbitrary", "arbitrary"),
        name="inproj",
    )(*xs, mod, mod, w)


def _hyena_filter_kernel(z_ref, w1_ref, b1_ref, fr_ref, w2_ref, b2_ref, w3_ref, win_ref, fwd32_ref, inv32_ref,
                         hc_ref, hs_ref, hc2_ref, fwd_ref, inv_ref, *, l):
    hi = lax.Precision.HIGHEST
    fwd = fwd32_ref[...].astype(BF16)
    fwd_ref[...] = fwd
    inv_ref[...] = inv32_ref[...].astype(BF16)
    fr = fr_ref[...]
    h = jnp.sin(fr * (jnp.dot(z_ref[...], w1_ref[...], precision=hi, preferred_element_type=F32) + b1_ref[...]))
    h = jnp.sin(fr * (jnp.dot(h, w2_ref[...], precision=hi, preferred_element_type=F32) + b2_ref[...]))
    h = jnp.dot(h, w3_ref[...], precision=hi, preferred_element_type=F32)
    win = win_ref[...]
    hf = h[:, :HY_W] * win
    hb = h[:, HY_W:] * win
    pq = _bdot(fwd, jnp.concatenate([hf + hb, hf - hb], 1))
    p = pq[:, :HY_W]
    q = pq[:, HY_W:]
    row0 = lax.broadcasted_iota(jnp.int32, (l, 1), 0) == 0
    hc = p[:l]
    hc_ref[...] = hc
    hs_ref[...] = jnp.where(row0, 0.0, q[l:])
    hc2_ref[...] = jnp.where(row0, p[l:l + 1], hc)


def _hyena_filter(l, fw1, fb1, ffreq, fw2, fb2, fw3):
    z, window = _filter_consts(l)
    fwd, inv = _dft_mats(l)
    pad_c = LANES - FILT_HID
    w1 = jnp.pad(fw1, ((0, LANES - POS_EMB), (0, pad_c)))
    w2 = jnp.pad(fw2, ((0, pad_c), (0, pad_c)))
    w3 = jnp.pad(fw3, ((0, pad_c), (0, 0)))
    row = lambda a: jnp.pad(a, (0, pad_c)).reshape(1, LANES)
    shp = jax.ShapeDtypeStruct((l, HY_W), F32)
    return pl.pallas_call(
        functools.partial(_hyena_filter_kernel, l=l),
        out_shape=(shp, shp, shp, jax.ShapeDtypeStruct(fwd.shape, BF16), jax.ShapeDtypeStruct(inv.shape, BF16)),
        compiler_params=pltpu.CompilerParams(vmem_limit_bytes=VMEM_LIMIT),
        name=f"hyena_filter_{l}",
    )(z, w1, row(fb1), row(ffreq), w2, row(fb2), w3, window, fwd, inv)


def _hyena_kernel(u_ref, cw_ref, cb_ref, skip_ref, fwd_ref, inv_ref, hc_ref, hs_ref, hc2_ref, o_ref, *, l, nb):
    rows = lax.broadcasted_iota(jnp.int32, (l, 1), 0)
    hs = hs_ref[...]
    for s in range(nb):
        seq = slice(s * l, (s + 1) * l)
        u = u_ref[seq, :]
        prev = jnp.where(rows == 0, 0.0, pltpu.roll(u, 1, 0))
        nxt = jnp.where(rows == l - 1, 0.0, pltpu.roll(u, l - 1, 0))
        uc = prev * cw_ref[0:1, :] + u * cw_ref[1:2, :] + nxt * cw_ref[2:3, :] + cb_ref[...]
        x0 = uc[:, :HY_W]
        x1 = uc[:, HY_W:2 * HY_W]
        v = uc[:, 2 * HY_W:] * x1
        ab = _bdot(fwd_ref[...], v)
        a, b = ab[:l], ab[l:]
        re = a * hc_ref[...] - b * hs
        im = a * hs + b * hc2_ref[...]
        y = _bdot(inv_ref[...], jnp.concatenate([re, im], 0)) * (1.0 / (2 * l))
        o_ref[seq, :] = ((y + skip_ref[...] * v) * x0).astype(o_ref.dtype)


def _hyena(u_all, row_block0, n_seq, l, conv_w, conv_b, skip, filt, into, nb=1):
    hc, hs, hc2, fwd, inv = filt
    const = lambda shape: pl.BlockSpec(shape, lambda s: (0, 0))
    in_specs = [pl.BlockSpec((nb * l, 3 * HY_W), lambda s: (row_block0 + s, 0)),
                const((3, 3 * HY_W)), const((1, 3 * HY_W)), const((1, HY_W)),
                const((2 * l, l)), const((l, 2 * l)),
                const((l, HY_W)), const((l, HY_W)), const((l, HY_W))]
    args = [u_all, conv_w, conv_b.reshape(1, -1), skip.reshape(1, -1), fwd, inv, hc, hs, hc2]
    kern, alias = _fill_rows(functools.partial(_hyena_kernel, l=l, nb=nb), in_specs, args, into)
    return pl.pallas_call(
        kern,
        grid=(n_seq // nb,),
        in_specs=in_specs,
        out_specs=pl.BlockSpec((nb * l, HY_W), lambda s: (row_block0 + s, 0)),
        out_shape=jax.ShapeDtypeStruct((T_ALL, HY_W), BF16),
        input_output_aliases=alias,
        compiler_params=_cparams("parallel"),
        name=f"hyena_{l}",
    )(*args)


def _seg_rms_norm(x, bd_ref, g):
    sq = x * x
    hi = sq.astype(BF16)
    lo = (sq - hi.astype(F32)).astype(BF16)
    bd = bd_ref[...]
    ss = (jnp.dot(hi, bd, preferred_element_type=F32) + jnp.dot(lo, bd, preferred_element_type=F32))
    return x * lax.rsqrt(ss * (1.0 / HEAD_DIM) + EPS) * g


def _rope(x, cos, sin_a, sin_b, quarter):
    w = x.shape[-1]
    return x * cos + pltpu.roll(x, w - quarter, 1) * sin_a + pltpu.roll(x, quarter, 1) * sin_b


def _attn_kernel(*refs, l, qb, lc, rope, qknorm, band, has_sink, emit_k):
    it = iter(refs)
    q_ref, k_ref, v_ref = next(it), next(it), next(it)
    if lc:
        ck_ref, cv_ref = next(it), next(it)
    if rope:
        cq_ref, saq_ref, sbq_ref = next(it), next(it), next(it)
        ckk_ref, sak_ref, sbk_ref = next(it), next(it), next(it)
    if qknorm:
        qg_ref, kg_ref, bdq_ref, bdk_ref = next(it), next(it), next(it), next(it)
    if has_sink:
        sink_ref = next(it)
    o_ref = next(it)
    if emit_k:
        kout_ref = next(it)
    kp_scr = next(it)
    if lc:
        ckp_scr, cvp_scr = next(it), next(it)

    qi = pl.program_id(1)

    @pl.when(qi == 0)
    def _():
        k = k_ref[...]
        if qknorm:
            k = _seg_rms_norm(k, bdk_ref, kg_ref[...])
        if emit_k:
            kout_ref[...] = k
        if rope:
            k = _rope(k, ckk_ref[...], sak_ref[...], sbk_ref[...], HEAD_DIM // 4)
        for kv in range(N_KV):
            kp_scr[kv] = k[:, kv * HEAD_DIM:(kv + 1) * HEAD_DIM].astype(BF16)
            if lc:
                ckp_scr[kv] = ck_ref[:, kv, :].astype(BF16)
                cvp_scr[kv] = cv_ref[:, kv, :].astype(BF16)

    q = q_ref[...]
    if qknorm:
        q = _seg_rms_norm(q, bdq_ref, qg_ref[...])
    if rope:
        q = _rope(q, cq_ref[...], saq_ref[...], sbq_ref[...], HEAD_DIM // 4)
    q = q * HEAD_DIM ** -0.5

    rows = GROUPS * qb
    if band:
        n_loc = 3 * BLOCK
        start = pl.multiple_of(jnp.clip((qi - 1) * BLOCK, 0, l - n_loc), BLOCK)
        keys = pl.ds(start, n_loc)
        tq = qi * qb + lax.broadcasted_iota(jnp.int32, (rows, 1), 0) % qb
        tk = start + lax.broadcasted_iota(jnp.int32, (1, n_loc), 1)
        valid = jnp.abs(tq - tk) <= BLOCK
    else:
        keys = slice(None)

    outs = []
    for kv in range(N_KV):
        lanes = slice(kv * HEAD_DIM, (kv + 1) * HEAD_DIM)
        qs = jnp.concatenate([q[:, (kv * GROUPS + g) * HEAD_DIM:(kv * GROUPS + g + 1) * HEAD_DIM]
                              for g in range(GROUPS)], 0)
        s = _bdot_nt(qs, kp_scr[kv, keys, :])
        if band:
            s = jnp.where(valid, s, -jnp.inf)
        if lc:
            s_c = _bdot_nt(qs, ckp_scr[kv])
        e_parts, ec_parts, dens = [], [], []
        for g in range(GROUPS):
            r = slice(g * qb, (g + 1) * qb)
            m = jnp.max(s[r], -1, keepdims=True)
            if lc:
                m = jnp.maximum(m, jnp.max(s_c[r], -1, keepdims=True))
            if has_sink:
                sink = sink_ref[kv * GROUPS + g]
                m = jnp.maximum(m, sink)
            e = jnp.exp(s[r] - m)
            den = jnp.sum(e, -1, keepdims=True)
            e_parts.append(e)
            if lc:
                e_c = jnp.exp(s_c[r] - m)
                den = den + jnp.sum(e_c, -1, keepdims=True)
                ec_parts.append(e_c)
            if has_sink:
                den = den + jnp.exp(sink - m)
            dens.append(den)
        o = _bdot(jnp.concatenate(e_parts, 0), v_ref[keys, lanes])
        if lc:
            o = o + _bdot(jnp.concatenate(ec_parts, 0), cvp_scr[kv])
        outs.extend(o[g * qb:(g + 1) * qb] / dens[g] for g in range(GROUPS))
    o_ref[...] = jnp.concatenate(outs, 1).astype(o_ref.dtype)


def _attention(u, *, row0, n_seq, l, q_col, k_col, v_col, into, qb=BLOCK, ctx=None, rope=False, qk_gain=None,
               band=False, sink=None, emit_k=False):
    nq = l // qb
    qw = N_HEADS * HEAD_DIM
    kw = N_KV * HEAD_DIM
    qb0, sb0 = row0 // qb, row0 // l
    seq_spec = lambda col: pl.BlockSpec((l, kw), lambda b, i: (sb0 + b, col // kw))
    const = lambda shape: pl.BlockSpec(shape, lambda b, i: (0, 0))
    in_specs = [pl.BlockSpec((qb, qw), lambda b, i: (qb0 + b * nq + i, q_col // qw)),
                seq_spec(k_col), seq_spec(v_col)]
    args = [u, u, u]
    lc = 0
    if ctx is not None:
        lc = ctx[0].shape[2]
        in_specs += [pl.BlockSpec((None, None, lc, N_KV, HEAD_DIM), lambda b, i: (b, 0, 0, 0, 0))] * 2
        args += list(ctx)
    if rope:
        tabs = _rope_tables(l, HEAD_DIM, N_HEADS)
        in_specs += [pl.BlockSpec((qb, qw), lambda b, i: (i, 0))] * 3 + [const((l, kw))] * 3
        args += list(tabs) + list(tabs)
    if qk_gain is not None:
        bd = np.kron(np.eye(N_HEADS), np.ones((HEAD_DIM, HEAD_DIM)))
        in_specs += [const((1, qw)), const((1, kw)), const((qw, qw)), const((kw, kw))]
        args += [jnp.tile(qk_gain[0], N_HEADS).reshape(1, qw), jnp.tile(qk_gain[1], N_KV).reshape(1, kw),
                 jnp.asarray(bd, BF16), jnp.asarray(bd[:kw, :kw], BF16)]
    if sink is not None:
        in_specs.append(pl.BlockSpec(memory_space=pltpu.SMEM))
        args.append(sink)
    out_specs = [pl.BlockSpec((qb, qw), lambda b, i: (qb0 + b * nq + i, 0))]
    out_shape = [jax.ShapeDtypeStruct((T_ALL, qw), BF16)]
    if emit_k:
        out_specs.append(pl.BlockSpec((l, kw), lambda b, i: (b, 0)))
        out_shape.append(jax.ShapeDtypeStruct((n_seq * l, kw), F32))
    kern = functools.partial(_attn_kernel, l=l, qb=qb, lc=lc, rope=rope, qknorm=qk_gain is not None, band=band,
                             has_sink=sink is not None, emit_k=emit_k)
    kern, alias = _fill_rows(kern, in_specs, args, into)
    return pl.pallas_call(
        kern,
        grid=(n_seq, nq),
        in_specs=in_specs,
        out_specs=out_specs,
        out_shape=out_shape,
        input_output_aliases=alias,
        scratch_shapes=[pltpu.VMEM((N_KV, l, HEAD_DIM), BF16)]
        + [pltpu.VMEM((N_KV, lc, HEAD_DIM), BF16)] * (2 if lc else 0),
        compiler_params=_cparams("parallel", "arbitrary"),
        name=f"attn_{l}_{'b' if sink is not None else 'c'}",
    )(*args)


def _ret_kernel(*refs, l, nb, rope, has_s0, emit_state):
    it = iter(refs)
    dec_ref = next(it)
    rq_ref, rk_ref, rv_ref, gf_ref, gb_ref = next(it), next(it), next(it), next(it), next(it)
    if rope:
        cos_ref, sa_ref, sb_ref = next(it), next(it), next(it)
    if has_s0:
        s0_ref = next(it)
    o_ref = next(it)
    if emit_state:
        st_ref = next(it)

    h = pl.program_id(1)
    qs, ks, vs = [], [], []
    for s in range(nb):
        seq = slice(s * l, (s + 1) * l)
        q = rq_ref[seq, :] * RET_D ** -0.5
        k = rk_ref[seq, :]
        if rope:
            q = _rope(q, cos_ref[...], sa_ref[...], sb_ref[...], RET_D // 4)
            k = _rope(k, cos_ref[...], sa_ref[...], sb_ref[...], RET_D // 4)
        qs.append(q)
        ks.append(k)
        vs.append(rv_ref[seq, :])
    n = l // CHUNK
    ii = lax.broadcasted_iota(jnp.int32, (CHUNK, 1), 0).astype(F32)
    jj = lax.broadcasted_iota(jnp.int32, (1, CHUNK), 1).astype(F32)
    diff = ii - jj
    ys = [None] * nb
    for d in range(2):
        log_g = jnp.log(jax.nn.sigmoid(jnp.full((1, 1), dec_ref[d, h], F32)))
        if d == 0:
            mask = jnp.exp(jnp.where(diff >= 0, diff * log_g, -jnp.inf))
            q_dec = jnp.exp((ii + 1.0) * log_g)
            k_dec = jnp.exp((CHUNK - 1.0 - ii) * log_g)
            order = range(n)
        else:
            mask = jnp.exp(jnp.where(diff <= 0, -diff * log_g, -jnp.inf))
            q_dec = jnp.exp((CHUNK - ii) * log_g)
            k_dec = jnp.exp(ii * log_g)
            order = reversed(range(n))
        c_dec = jnp.exp(CHUNK * log_g)
        order = list(order)
        for s in range(nb):
            seq = slice(s * l, (s + 1) * l)
            q, k, v = qs[s], ks[s], vs[s]
            state = s0_ref[s, d] if has_s0 else jnp.zeros((RET_D, RET_D), F32)
            o_chunks = [None] * n
            for c in order:
                sl = slice(c * CHUNK, (c + 1) * CHUNK)
                qc, kc, vc = q[sl], k[sl], v[sl]
                inner = _bdot_nt(qc, kc) * mask
                o_chunks[c] = _bdot(inner, vc) + _bdot(qc * q_dec, state)
                state = state * c_dec + _bdot_tn(kc * k_dec, vc)
            if emit_state:
                st_ref[s, d] = state
            o = jnp.concatenate(o_chunks, 0)
            o = o * lax.rsqrt(jnp.mean(o * o, -1, keepdims=True) + EPS)
            gate = _silu((gf_ref if d == 0 else gb_ref)[seq, :])
            ys[s] = gate * o if ys[s] is None else ys[s] + gate * o
    for s in range(nb):
        o_ref[s * l:(s + 1) * l, :] = ys[s].astype(o_ref.dtype)


def _retention(u, ret_decay, *, row0, n_seq, l, into, nb=1, rope=False, s0=None, emit_state=False):
    sb0 = row0 // (nb * l)
    col = lambda c0: pl.BlockSpec((nb * l, RET_D), lambda b, h: (sb0 + b, c0 // RET_D + h))
    in_specs = [pl.BlockSpec(memory_space=pltpu.SMEM),
                col(CD_RQ), col(CD_RK), col(CD_RV), col(CD_GF), col(CD_GB)]
    args = [ret_decay, u, u, u, u, u]
    if rope:
        in_specs += [pl.BlockSpec((l, RET_D), lambda b, h: (0, 0))] * 3
        args += list(_rope_tables(l, RET_D, 1))
    state_spec = pl.BlockSpec((nb, 2, None, RET_D, RET_D), lambda b, h: (b, 0, h, 0, 0))
    if s0 is not None:
        in_specs.append(state_spec)
        args.append(s0)
    out_specs = [pl.BlockSpec((nb * l, RET_D), lambda b, h: (sb0 + b, h))]
    out_shape = [jax.ShapeDtypeStruct((T_ALL, RET_HEADS * RET_D), BF16)]
    if emit_state:
        out_specs.append(state_spec)
        out_shape.append(jax.ShapeDtypeStruct((n_seq, 2, RET_HEADS, RET_D, RET_D), F32))
    kern = functools.partial(_ret_kernel, l=l, nb=nb, rope=rope, has_s0=s0 is not None, emit_state=emit_state)
    kern, alias = _fill_rows(kern, in_specs, args, into)
    return pl.pallas_call(
        kern,
        grid=(n_seq // nb, RET_HEADS),
        in_specs=in_specs,
        out_specs=out_specs,
        out_shape=out_shape,
        input_output_aliases=alias,
        compiler_params=_cparams("parallel", "parallel"),
        name=f"retention_{l}",
    )(*args)


def _outproj_kernel(*refs, tm, slabs):
    ya_ref, yb_ref, w_ref = refs[:3]
    x_refs = refs[3:3 + len(slabs)]
    gate_ref, lng_ref, lnb_ref, o_ref, w_scr = refs[3 + len(slabs):]
    i = pl.program_id(0)

    @pl.when(i == 0)
    def _():
        w_scr[...] = w_ref[...].astype(BF16)

    half = ya_ref.shape[1]
    m = (jnp.dot(ya_ref[...], w_scr[:half], preferred_element_type=F32)
         + jnp.dot(yb_ref[...], w_scr[half:], preferred_element_type=F32))
    gm = gate_ref[pl.ds(_group_of_tile(i, tm), 1), :] * m

    for x_ref, (first, end) in zip(x_refs, slabs):
        @pl.when((i >= first) & (i < end))
        def _():
            o_ref[...] = _layer_norm(ALPHA * x_ref[...] + gm, lng_ref[...], lnb_ref[...])


def _outproj(ya, yb, w, xs, mod, ln_g, ln_b):
    tm = 1024
    half = ya.shape[1]
    slabs = _slab_tiles(xs, tm)
    const = lambda shape: pl.BlockSpec(shape, lambda i: (0, 0))

    def slab_spec(first, end):
        return pl.BlockSpec((tm, D_MODEL), lambda i: (jnp.clip(i - first, 0, end - first - 1), 0))

    return pl.pallas_call(
        functools.partial(_outproj_kernel, tm=tm, slabs=slabs),
        grid=(T_ALL // tm,),
        in_specs=[pl.BlockSpec((tm, half), lambda i: (i, 0)),
                  pl.BlockSpec((tm, half), lambda i: (i, 0)),
                  const((2 * half, D_MODEL))]
        + [slab_spec(*s) for s in slabs]
        + [pl.BlockSpec((MOD_ROWS, D_MODEL), lambda i: (0, 2)),
           const((1, D_MODEL)), const((1, D_MODEL))],
        out_specs=pl.BlockSpec((tm, D_MODEL), lambda i: (i, 0)),
        out_shape=jax.ShapeDtypeStruct((T_ALL, D_MODEL), F32),
        scratch_shapes=[pltpu.VMEM((2 * half, D_MODEL), BF16)],
        compiler_params=_cparams("arbitrary"),
        name="outproj_ln",
    )(ya, yb, w, *xs, mod, ln_g.reshape(1, -1), ln_b.reshape(1, -1))


TOK_TILE = 256
N_TOK_TILES = T_ALL // TOK_TILE
SORT_TILE = 256
EXP_TILE = 512
N_EXP_TILES = (2 * T_ALL) // EXP_TILE + N_EXPERTS
N_SORT_TILES = N_EXP_TILES * (EXP_TILE // SORT_TILE)
N_PROMPT_TOK_TILES = T_PROMPT // TOK_TILE
CUM_ROWS = 32


def _route_kernel(x_ref, sh_ref, sc_ref, r_ref, h_ref, rank_ref, rank_t_ref, gate_t_ref, cum_ref,
                  carry_row, carry_col):
    c = pl.program_id(0)

    @pl.when(c == 0)
    def _():
        carry_row[...] = jnp.zeros_like(carry_row)
        carry_col[...] = jnp.zeros_like(carry_col)
        cum_ref[...] = jnp.zeros_like(cum_ref)

    g = _group_of_tile(c, TOK_TILE)
    h = x_ref[...] * (1.0 + sc_ref[pl.ds(g, 1), :]) + sh_ref[pl.ds(g, 1), :]
    h_hi = h.astype(BF16)
    h_ref[...] = h_hi
    r = r_ref[...]
    r_hi = r.astype(BF16)
    h_lo = (h - h_hi.astype(F32)).astype(BF16)
    r_lo = (r - r_hi.astype(F32)).astype(BF16)
    logits = (jnp.dot(h_hi, r_hi, preferred_element_type=F32) + jnp.dot(h_lo, r_hi, preferred_element_type=F32)
              + jnp.dot(h_hi, r_lo, preferred_element_type=F32))
    lane = lax.broadcasted_iota(jnp.int32, logits.shape, 1)
    logits = jnp.where(lane < N_EXPERTS, logits, -jnp.inf)
    m1 = jnp.max(logits, -1, keepdims=True)
    i1 = jnp.min(jnp.where(logits == m1, lane, LANES), -1, keepdims=True)
    rest = jnp.where(lane == i1, -jnp.inf, logits)
    m2 = jnp.max(rest, -1, keepdims=True)
    i2 = jnp.min(jnp.where(rest == m2, lane, LANES), -1, keepdims=True)
    e2 = jnp.exp(m2 - m1)
    den = 1.0 + e2
    gates = jnp.where(lane == i1, 1.0 / den, 0.0) + jnp.where(lane == i2, e2 / den, 0.0)
    sel = jnp.where((lane == i1) | (lane == i2), 1.0, 0.0)
    sel_t = sel.T
    ti = lax.broadcasted_iota(jnp.int32, (TOK_TILE, TOK_TILE), 0)
    tj = lax.broadcasted_iota(jnp.int32, (TOK_TILE, TOK_TILE), 1)
    before = jnp.where(tj < ti, 1.0, 0.0).astype(BF16)
    rank = jnp.dot(before, sel.astype(BF16), preferred_element_type=F32) + carry_row[...]
    rank_t = lax.dot_general(sel_t.astype(BF16), before, (((1,), (1,)), ((), ())),
                             preferred_element_type=F32) + carry_col[...]
    rank_ref[...] = jnp.where(sel > 0.0, rank, -1.0)
    rank_t_ref[...] = jnp.where(sel_t > 0.0, rank_t, -1.0)[:N_EXPERTS]
    gate_t_ref[...] = gates.T[:N_EXPERTS]
    cum_ref[pl.ds(c, 1), :] = carry_row[...].astype(jnp.int32)
    carry_row[...] += jnp.sum(sel, 0, keepdims=True)
    carry_col[...] += jnp.sum(sel_t, 1, keepdims=True)

    @pl.when(c == N_TOK_TILES - 1)
    def _():
        cum_ref[pl.ds(N_TOK_TILES, 1), :] = carry_row[...].astype(jnp.int32)


def _route(x, mod, router):
    tile = lambda w: pl.BlockSpec((TOK_TILE, w), lambda c: (c, 0))
    tile_t = pl.BlockSpec((N_EXPERTS, TOK_TILE), lambda c: (0, c))
    return pl.pallas_call(
        _route_kernel,
        grid=(N_TOK_TILES,),
        in_specs=[tile(D_MODEL),
                  pl.BlockSpec((MOD_ROWS, D_MODEL), lambda c: (0, 3)),
                  pl.BlockSpec((MOD_ROWS, D_MODEL), lambda c: (0, 4)),
                  pl.BlockSpec((D_MODEL, LANES), lambda c: (0, 0))],
        out_specs=[tile(D_MODEL), tile(LANES), tile_t, tile_t,
                   pl.BlockSpec((CUM_ROWS, LANES), lambda c: (0, 0))],
        out_shape=[jax.ShapeDtypeStruct((T_ALL, D_MODEL), BF16),
                   jax.ShapeDtypeStruct((T_ALL, LANES), F32),
                   jax.ShapeDtypeStruct((N_EXPERTS, T_ALL), F32),
                   jax.ShapeDtypeStruct((N_EXPERTS, T_ALL), F32),
                   jax.ShapeDtypeStruct((CUM_ROWS, LANES), jnp.int32)],
        scratch_shapes=[pltpu.VMEM((1, LANES), F32), pltpu.VMEM((LANES, 1), F32)],
        compiler_params=_cparams("arbitrary"),
        name="route",
    )(x, mod, mod, jnp.pad(router, ((0, 0), (0, LANES - N_EXPERTS))))


def _gather_kernel(te_ref, off_ref, cum_ref, nt_ref, h_ref, rank_t_ref, gate_t_ref, xs_ref, gs_ref, acc_scr, g_scr):
    i = pl.program_id(0)
    e = te_ref[i]
    r0 = i * SORT_TILE - off_ref[e]
    acc_scr[...] = jnp.zeros_like(acc_scr)
    g_scr[...] = jnp.zeros_like(g_scr)
    want = (r0 + lax.broadcasted_iota(jnp.int32, (SORT_TILE, 1), 0)).astype(F32)
    for c in range(N_TOK_TILES):
        lo = cum_ref[c * N_EXPERTS + e]
        hi = cum_ref[(c + 1) * N_EXPERTS + e]

        @pl.when((i < nt_ref[0]) & (lo < r0 + SORT_TILE) & (hi > r0))
        def _():
            cols = slice(c * TOK_TILE, (c + 1) * TOK_TILE)
            pick = rank_t_ref[pl.ds(e, 1), cols] == want
            acc_scr[...] += jnp.dot(jnp.where(pick, 1.0, 0.0).astype(BF16), h_ref[cols, :],
                                    preferred_element_type=F32)
            g_scr[...] += jnp.sum(jnp.where(pick, gate_t_ref[pl.ds(e, 1), cols], 0.0), -1, keepdims=True)

    xs_ref[...] = acc_scr[...].astype(BF16)
    gs_ref[...] = jnp.broadcast_to(g_scr[...], gs_ref.shape)


def _gather(tile_expert, off, cum, n_tiles, h, rank_t, gate_t):
    const = lambda shape: pl.BlockSpec(shape, lambda i, *_: (0, 0))
    return pl.pallas_call(
        _gather_kernel,
        grid_spec=pltpu.PrefetchScalarGridSpec(
            num_scalar_prefetch=4,
            grid=(N_SORT_TILES,),
            in_specs=[const((T_ALL, D_MODEL)), const((N_EXPERTS, T_ALL)), const((N_EXPERTS, T_ALL))],
            out_specs=[pl.BlockSpec((SORT_TILE, D_MODEL), lambda i, *_: (i, 0)),
                       pl.BlockSpec((SORT_TILE, LANES), lambda i, *_: (i, 0))],
            scratch_shapes=[pltpu.VMEM((SORT_TILE, D_MODEL), F32), pltpu.VMEM((SORT_TILE, 1), F32)]),
        out_shape=[jax.ShapeDtypeStruct((N_SORT_TILES * SORT_TILE, D_MODEL), BF16),
                   jax.ShapeDtypeStruct((N_SORT_TILES * SORT_TILE, LANES), F32)],
        compiler_params=_cparams("parallel"),
        name="moe_gather",
    )(tile_expert, off, cum, n_tiles, h, rank_t, gate_t)


def _expert_kernel(te_ref, nt_ref, xs_ref, gs_ref, w1_ref, w3_ref, w2_ref, ys_ref, w1_scr, w3_scr, w2_scr, acc_scr):
    i = pl.program_id(0)

    @pl.when((i == 0) | (te_ref[i] != te_ref[jnp.maximum(i - 1, 0)]))
    def _():
        w1_scr[...] = w1_ref[...].astype(BF16)
        w3_scr[...] = w3_ref[...].astype(BF16)
        w2_scr[...] = w2_ref[...].astype(BF16)

    @pl.when(i < nt_ref[0])
    def _():
        x = xs_ref[...]
        gate = gs_ref[:, 0:1]
        acc_scr[...] = jnp.zeros_like(acc_scr)
        for c0 in range(0, EXPERT_FF, FF_CHUNK):
            cs = slice(c0, min(c0 + FF_CHUNK, EXPERT_FF))
            a = jnp.dot(x, w1_scr[:, cs], preferred_element_type=F32)
            b = jnp.dot(x, w3_scr[:, cs], preferred_element_type=F32)
            act = (_silu(a) * b * gate).astype(BF16)
            acc_scr[...] += jnp.dot(act, w2_scr[cs, :], preferred_element_type=F32)
        ys_ref[...] = acc_scr[...].astype(BF16)

    @pl.when(i >= nt_ref[0])
    def _():
        ys_ref[...] = jnp.zeros_like(ys_ref)


def _experts(tile_expert, n_tiles, xs, gs, w1, w3, w2):
    w_in = pl.BlockSpec((None, D_MODEL, EXPERT_FF), lambda i, te, nt: (te[i], 0, 0))
    w_out = pl.BlockSpec((None, EXPERT_FF, D_MODEL), lambda i, te, nt: (te[i], 0, 0))
    return pl.pallas_call(
        _expert_kernel,
        grid_spec=pltpu.PrefetchScalarGridSpec(
            num_scalar_prefetch=2,
            grid=(N_EXP_TILES,),
            in_specs=[pl.BlockSpec((EXP_TILE, D_MODEL), lambda i, te, nt: (i, 0)),
                      pl.BlockSpec((EXP_TILE, LANES), lambda i, te, nt: (i, 0)),
                      w_in, w_in, w_out],
            out_specs=pl.BlockSpec((EXP_TILE, D_MODEL), lambda i, te, nt: (i, 0)),
            scratch_shapes=[pltpu.VMEM((D_MODEL, EXPERT_FF), BF16), pltpu.VMEM((D_MODEL, EXPERT_FF), BF16),
                            pltpu.VMEM((EXPERT_FF, D_MODEL), BF16), pltpu.VMEM((EXP_TILE, D_MODEL), F32)]),
        out_shape=jax.ShapeDtypeStruct((N_EXP_TILES * EXP_TILE, D_MODEL), BF16),
        compiler_params=_cparams("arbitrary"),
        name="moe_experts",
    )(tile_expert, n_tiles, xs, gs, w1, w3, w2)


def _combine_kernel(off_ref, cum_ref, ys_ref, rank_ref, x_ref, gate_ref, lng_ref, lnb_ref, op_ref, os_ref, acc_scr):
    c = pl.program_id(0)
    acc_scr[...] = jnp.zeros_like(acc_scr)
    rank = rank_ref[...]
    lane = lax.broadcasted_iota(jnp.int32, rank.shape, 1)
    cols = lax.broadcasted_iota(jnp.int32, (1, SORT_TILE), 1)
    for e in range(N_EXPERTS):
        lo = off_ref[e] + cum_ref[c * N_EXPERTS + e]
        hi = off_ref[e] + cum_ref[(c + 1) * N_EXPERTS + e]
        r = jnp.sum(jnp.where(lane == e, rank, 0.0), -1, keepdims=True)
        pos = jnp.where(r >= 0.0, r + jnp.full((1, 1), off_ref[e], jnp.int32).astype(F32), -1.0)
        first = lo // SORT_TILE
        for k in range(2):
            s = first + k

            @pl.when((hi > lo) & (s * SORT_TILE < hi))
            def _():
                pick = pos == (s * SORT_TILE + cols).astype(F32)
                rows = ys_ref[pl.ds(pl.multiple_of(s * SORT_TILE, SORT_TILE), SORT_TILE), :]
                acc_scr[...] += jnp.dot(jnp.where(pick, 1.0, 0.0).astype(BF16), rows, preferred_element_type=F32)

    g = _group_of_tile(c, TOK_TILE)
    z = ALPHA * x_ref[...] + gate_ref[pl.ds(g, 1), :] * acc_scr[...]
    y = _layer_norm(z, lng_ref[...], lnb_ref[...])

    @pl.when(c < N_PROMPT_TOK_TILES)
    def _():
        op_ref[...] = y

    @pl.when(c >= N_PROMPT_TOK_TILES)
    def _():
        os_ref[...] = y


def _combine(off, cum, ys, rank, x, mod, ln_g, ln_b):
    const = lambda shape: pl.BlockSpec(shape, lambda c, *_: (0, 0))
    last_p = N_PROMPT_TOK_TILES - 1
    return pl.pallas_call(
        _combine_kernel,
        grid_spec=pltpu.PrefetchScalarGridSpec(
            num_scalar_prefetch=2,
            grid=(N_TOK_TILES,),
            in_specs=[pl.BlockSpec(ys.shape, lambda c, *_: (0, 0), pipeline_mode=pl.Buffered(1)),
                      pl.BlockSpec((TOK_TILE, LANES), lambda c, *_: (c, 0)),
                      pl.BlockSpec((TOK_TILE, D_MODEL), lambda c, *_: (c, 0)),
                      pl.BlockSpec((MOD_ROWS, D_MODEL), lambda c, *_: (0, 5)),
                      const((1, D_MODEL)), const((1, D_MODEL))],
            out_specs=[pl.BlockSpec((TOK_TILE, D_MODEL), lambda c, *_: (jnp.minimum(c, last_p), 0)),
                       pl.BlockSpec((TOK_TILE, D_MODEL), lambda c, *_: (jnp.maximum(c - last_p - 1, 0), 0))],
            scratch_shapes=[pltpu.VMEM((TOK_TILE, D_MODEL), F32)]),
        out_shape=[jax.ShapeDtypeStruct((T_PROMPT, D_MODEL), F32), jax.ShapeDtypeStruct((T_SAMPLE, D_MODEL), F32)],
        compiler_params=_cparams("arbitrary"),
        name="moe_combine_ln",
    )(off, cum, ys, rank, x, mod, ln_g.reshape(1, -1), ln_b.reshape(1, -1))


def _moe(x, mod, ln_g, ln_b, router, w1, w3, w2):
    h, rank, rank_t, gate_t, cum = _route(x, mod, router)
    counts = cum[N_TOK_TILES, :N_EXPERTS]
    tiles = (counts + EXP_TILE - 1) // EXP_TILE
    ends = jnp.cumsum(tiles)
    off = ((ends - tiles) * EXP_TILE).astype(jnp.int32)
    n_tiles = ends[-1:].astype(jnp.int32)
    tile_ids = jnp.minimum(jnp.arange(N_EXP_TILES, dtype=jnp.int32), n_tiles - 1)
    tile_expert = jnp.sum((tile_ids[:, None] >= ends[None, :]).astype(jnp.int32), -1)
    sub = EXP_TILE // SORT_TILE
    cum_flat = cum[:N_TOK_TILES + 1, :N_EXPERTS].reshape(-1)
    xs, gs = _gather(jnp.repeat(tile_expert, sub), off, cum_flat, n_tiles * sub, h, rank_t, gate_t)
    ys = _experts(tile_expert, n_tiles, xs, gs, w1, w3, w2)
    return _combine(off, cum_flat, ys, rank, x, mod, ln_g, ln_b)


FF_CHUNK = 256


def _ffn_kernel(x_ref, sh_ref, sc_ref, gate_ref, lng_ref, lnb_ref, w1_ref, w3_ref, w2_ref, o_ref, h_scr, acc_scr,
                *, tm, ff):
    i, j = pl.program_id(0), pl.program_id(1)
    g = _group_of_tile(i, tm)

    @pl.when(j == 0)
    def _():
        h_scr[...] = (x_ref[...] * (1.0 + sc_ref[pl.ds(g, 1), :]) + sh_ref[pl.ds(g, 1), :]).astype(BF16)
        acc_scr[...] = jnp.zeros_like(acc_scr)

    h = h_scr[...]
    for c0 in range(0, ff, FF_CHUNK):
        cs = slice(c0, min(c0 + FF_CHUNK, ff))
        a = jnp.dot(h, w1_ref[:, cs].astype(BF16), preferred_element_type=F32)
        b = jnp.dot(h, w3_ref[:, cs].astype(BF16), preferred_element_type=F32)
        act = _silu(a) * b
        acc_scr[...] += jnp.dot(act.astype(BF16), w2_ref[cs, :].astype(BF16), preferred_element_type=F32)

    @pl.when(j == pl.num_programs(1) - 1)
    def _():
        z = ALPHA * x_ref[...] + gate_ref[pl.ds(g, 1), :] * acc_scr[...]
        o_ref[...] = _layer_norm(z, lng_ref[...], lnb_ref[...])


def _ffn(x, mod, ln_g, ln_b, w1, w3, w2):
    tm = 1024
    ff = FF_CHUNK
    n_j = D_FF // ff
    w_in_spec = pl.BlockSpec((D_MODEL, ff), lambda i, j: (0, j))
    w_out_spec = pl.BlockSpec((ff, D_MODEL), lambda i, j: (j, 0))
    mod_spec = lambda col: pl.BlockSpec((MOD_ROWS, D_MODEL), lambda i, j: (0, col))
    const = lambda shape: pl.BlockSpec(shape, lambda i, j: (0, 0))
    in_specs = [pl.BlockSpec((tm, D_MODEL), lambda i, j: (i, 0)),
                mod_spec(3), mod_spec(4), mod_spec(5),
                const((1, D_MODEL)), const((1, D_MODEL)),
                w_in_spec, w_in_spec, w_out_spec]
    args = [x, mod, mod, mod, ln_g.reshape(1, -1), ln_b.reshape(1, -1), w1, w3, w2]
    return pl.pallas_call(
        functools.partial(_ffn_kernel, tm=tm, ff=ff),
        grid=(T_ALL // tm, n_j),
        in_specs=in_specs,
        out_specs=pl.BlockSpec((tm, D_MODEL), lambda i, j: (i, 0)),
        out_shape=jax.ShapeDtypeStruct((T_ALL, D_MODEL), F32),
        scratch_shapes=[pltpu.VMEM((tm, D_MODEL), BF16), pltpu.VMEM((tm, D_MODEL), F32)],
        compiler_params=_cparams("parallel", "arbitrary"),
        name="ffn",
    )(*args)


def kernel(x_prompt, x_sample, c, cache_k_b, cache_v_b, cache_k_c, cache_v_c, state_ret, c_ctx, ada_w, ada_b, ln_g, ln_b, w_in_ab, hy_conv_w, hy_conv_b, hf_w1, hf_b1, hf_freq, hf_w2, hf_b2, hf_w3, hy_skip, sink_b, w_out_ab, ffn_w1, ffn_w3, ffn_w2, w_in_cd, qn_g, kn_g, ret_decay, w_out_cd, moe_router, moe_w1, moe_w3, moe_w2):
    x_in = (x_prompt.reshape(T_PROMPT, D_MODEL), x_sample.reshape(T_SAMPLE, D_MODEL))
    cvec = jnp.concatenate([c_ctx[None], c, jnp.zeros((MOD_ROWS - 1 - DEC_BATCH, D_MODEL), F32)], 0)
    mod = _modulation(cvec, ada_w, ada_b)
    kw = N_KV * HEAD_DIM
    prompt = dict(row0=0, n_seq=BATCH, l=SEQ)
    sample = dict(row0=T_PROMPT, n_seq=DEC_BATCH, l=DEC_SEQ)

    u, k_b, v_b = _inproj(x_in, mod[0], w_in_ab[0], tn=1152, taps=(AB_K, AB_V))
    filt_args = (hf_w1[0], hf_b1[0], hf_freq[0], hf_w2[0], hf_b2[0], hf_w3[0])
    hy_args = (hy_conv_w[0], hy_conv_b[0], hy_skip[0])
    blank = lambda w: jnp.zeros((T_ALL, w), BF16)
    ya = _hyena(u, 0, BATCH, SEQ, *hy_args, _hyena_filter(SEQ, *filt_args), into=blank(HY_W), nb=4)
    ya = _hyena(u, T_PROMPT // DEC_SEQ, DEC_BATCH, DEC_SEQ, *hy_args, _hyena_filter(DEC_SEQ, *filt_args), into=ya)
    cols_b = dict(q_col=AB_Q, k_col=AB_K, v_col=AB_V)
    ctx_b = (cache_k_b, cache_v_b)
    yb, = _attention(u, **prompt, **cols_b, qb=SEQ, sink=sink_b[0], into=blank(N_HEADS * HEAD_DIM))
    yb, = _attention(u, **sample, **cols_b, sink=sink_b[0], ctx=ctx_b, rope=True, band=True, into=yb)
    k_b = k_b.reshape(BATCH, 1, SEQ, N_KV, HEAD_DIM)
    v_b = v_b.reshape(BATCH, 1, SEQ, N_KV, HEAD_DIM)
    x = _outproj(ya, yb, w_out_ab[0], x_in, mod[0], ln_g[0, 0], ln_b[0, 0])
    x = _ffn(x, mod[0], ln_g[0, 1], ln_b[0, 1], ffn_w1[0], ffn_w3[0], ffn_w2[0])

    u, v_c = _inproj((x,), mod[1], w_in_cd[0], tn=1664, taps=(CD_V,))
    cols_c = dict(q_col=CD_Q, k_col=CD_K, v_col=CD_V)
    gains = (qn_g[0], kn_g[0])
    ctx_c = (cache_k_c, cache_v_c)
    yc, k_c = _attention(u, **prompt, **cols_c, qb=SEQ, qk_gain=gains, emit_k=True, into=blank(N_HEADS * HEAD_DIM))
    yc, = _attention(u, **sample, **cols_c, qb=2 * BLOCK, qk_gain=gains, ctx=ctx_c, rope=True, into=yc)
    yd, s_r = _retention(u, ret_decay[0], **prompt, nb=4, emit_state=True, into=blank(RET_HEADS * RET_D))
    yd, = _retention(u, ret_decay[0], **sample, nb=DEC_BATCH, rope=True, s0=state_ret[:, 0], into=yd)
    k_c = k_c.reshape(BATCH, 1, SEQ, N_KV, HEAD_DIM)
    v_c = v_c.reshape(BATCH, 1, SEQ, N_KV, HEAD_DIM)
    x = _outproj(yc, yd, w_out_cd[0], (x,), mod[1], ln_g[1, 0], ln_b[1, 0])
    y_prompt, y_sample = _moe(x, mod[1], ln_g[1, 1], ln_b[1, 1], moe_router[0], moe_w1[0], moe_w3[0], moe_w2[0])

    return (y_prompt.reshape(BATCH, SEQ, D_MODEL), y_sample.reshape(DEC_BATCH, DEC_SEQ, D_MODEL),
            k_b, v_b, k_c, v_c, s_r[:, None])
```
